```python
import jax, jax.numpy as jnp
from jax import lax
import numpy as np

D_MODEL = 2048
BATCH = 1
SEQ = 8192
DEPTH = 2

GRID_W = 64
CTX_LEN = 256
HEAD_DIM = 128
MIX_WIDTH = D_MODEL
N_GROUP_HEADS = MIX_WIDTH // (2 * HEAD_DIM)
EPS = 1e-6
A_HEADS = N_GROUP_HEADS
A_WIDTH = A_HEADS * HEAD_DIM
CONV_W = 3
B_HEADS = N_GROUP_HEADS
B_KEY_DIM = HEAD_DIM // 2
B_KEY_WIDTH = B_HEADS * B_KEY_DIM
B_VAL_WIDTH = B_HEADS * HEAD_DIM
GLA_RANK = 16
GLA_TAU = 16.0
SCAN_CHUNK = 64
C_HEADS = N_GROUP_HEADS
C_WIDTH = C_HEADS * HEAD_DIM
WIN_H = 8
WIN_W = 16
D_HEADS = N_GROUP_HEADS
D_KV_HEADS = 2
D_GROUP = D_HEADS // D_KV_HEADS
D_WIDTH = D_HEADS * HEAD_DIM
D_KV_WIDTH = D_KV_HEADS * HEAD_DIM
WINDOW = 128
ROPE_BASE = 10000.0
PEER_HEADS = 8
PEER_NKEYS = 128
PEER_EXPERTS = PEER_NKEYS * PEER_NKEYS
PEER_QDIM = 256
PEER_TOPK = 16
PEER_BLOCK = 128
N_EVEN = (DEPTH + 1) // 2
N_ODD = DEPTH // 2
AB_SPLITS = (A_WIDTH, A_WIDTH, A_WIDTH, A_WIDTH, 4 * A_HEADS,
             B_KEY_WIDTH, B_KEY_WIDTH, B_VAL_WIDTH, B_VAL_WIDTH, 2 * GLA_RANK)
AB_COLS = sum(AB_SPLITS)
CD_SPLITS = (C_WIDTH, C_WIDTH, D_KV_WIDTH, D_KV_WIDTH, C_WIDTH, D_WIDTH)
CD_KV_COLS = sum(CD_SPLITS[:4])
CD_COLS = sum(CD_SPLITS)

kernel_name = 'hybrid_mlstm_gla_natten_swa_peer'


def split_cols(p, sizes):
    return jnp.split(p, np.cumsum(sizes)[:-1].tolist(), axis=-1)


def rmsnorm(x, g):
    xf = x.astype(jnp.float32)
    y = xf * lax.rsqrt(jnp.mean(xf * xf, axis=-1, keepdims=True) + EPS)
    return (y * g.astype(jnp.float32)).astype(x.dtype)


def modulate(x, g, shift, scale):
    return rmsnorm(x, g) * (1 + scale) + shift


def heads(t, hd):
    return t.reshape(t.shape[0], t.shape[1], -1, hd)


def short_conv(x, w, b):
    pad = CONV_W // 2
    n = x.shape[1]
    xp = jnp.pad(x, ((0, 0), (pad, pad), (0, 0)))
    return sum(w[j] * xp[:, j:j + n] for j in range(CONV_W)) + b


def to_chunks(t):
    B_, H_, n = t.shape[:3]
    return jnp.moveaxis(t.reshape(B_, H_, n // SCAN_CHUNK, SCAN_CHUNK, *t.shape[3:]), 2, 0)


def from_chunks(t):
    nc, B_, H_, L = t.shape[:4]
    return jnp.moveaxis(t, 0, 2).reshape(B_, H_, nc * L, *t.shape[4:])


def mlstm_scan(q, k, v, ig, lf, state):
    q, k, v, ig, lf = (t.astype(jnp.float32) for t in (q, k, v, ig, lf))
    mask = jnp.tril(jnp.ones((SCAN_CHUNK, SCAN_CHUNK), bool))

    def step(carry, inp):
        C, nv, m = carry
        qc, kc, vc, ic, fc = inp
        b = jnp.cumsum(fc, axis=-1)
        dlog = jnp.where(mask, b[..., :, None] - b[..., None, :] + ic[..., None, :], -jnp.inf)
        m_inter = b + m[..., None]
        m_t = jnp.maximum(m_inter, jnp.max(dlog, axis=-1))
        sc = jnp.einsum('bhtd,bhsd->bhts', qc, kc) * jnp.exp(dlog - m_t[..., None])
        w_inter = jnp.exp(m_inter - m_t)
        num = jnp.einsum('bhts,bhse->bhte', sc, vc) + w_inter[..., None] * jnp.einsum('bhtd,bhde->bhte', qc, C)
        den = jnp.sum(sc, axis=-1) + w_inter * jnp.einsum('bhtd,bhd->bht', qc, nv)
        h = num / jnp.maximum(jnp.abs(den), jnp.exp(-m_t))[..., None]
        m_new = m_t[..., -1]
        w_end = jnp.exp(b[..., -1:] - b + ic - m_new[..., None])
        decay = jnp.exp(b[..., -1] + m - m_new)
        C = decay[..., None, None] * C + jnp.einsum('bhs,bhsd,bhse->bhde', w_end, kc, vc)
        nv = decay[..., None] * nv + jnp.einsum('bhs,bhsd->bhd', w_end, kc)
        return (C, nv, m_new), h

    state, h = lax.scan(step, state, tuple(to_chunks(t) for t in (q, k, v, ig, lf)))
    return from_chunks(h), state


def gla_scan(q, k, v, lg, S0):
    q, k, v, lg = (t.astype(jnp.float32) for t in (q, k, v, lg))
    mask = jnp.tril(jnp.ones((SCAN_CHUNK, SCAN_CHUNK), bool))[:, :, None]

    def step(S, inp):
        qc, kc, vc, gc = inp
        b = jnp.cumsum(gc, axis=2)
        rel = jnp.where(mask, b[:, :, :, None, :] - b[:, :, None, :, :], -jnp.inf)
        attn = jnp.einsum('bhtd,bhtsd,bhsd->bhts', qc, jnp.exp(rel), kc)
        o = jnp.einsum('bhts,bhse->bhte', attn, vc) + jnp.einsum('bhtd,bhde->bhte', qc * jnp.exp(b), S)
        b_end = b[:, :, -1:, :]
        S = jnp.exp(b_end[:, :, 0])[..., None] * S + jnp.einsum('bhsd,bhse->bhde', kc * jnp.exp(b_end - b), vc)
        return S, o

    S, o = lax.scan(step, S0, tuple(to_chunks(t) for t in (q, k, v, lg)))
    return from_chunks(o), S


def bidirectional(scan_fn, init, ctx_seq, ctx_gates, lat_seq, lat_gates):
    y_c, y_l = None, None
    for z in range(2):
        fc = tuple(ctx_seq) + tuple(g[z] for g in ctx_gates)
        fl = tuple(lat_seq) + tuple(g[z] for g in lat_gates)
        if z == 1:
            fc = tuple(jnp.flip(t, axis=2) for t in fc)
            fl = tuple(jnp.flip(t, axis=2) for t in fl)
        oc, st = scan_fn(*fc, init)
        ol, _ = scan_fn(*fl, st)
        if z == 1:
            oc, ol = jnp.flip(oc, axis=2), jnp.flip(ol, axis=2)
        y_c = oc if y_c is None else y_c + oc
        y_l = ol if y_l is None else y_l + ol
    return y_c, y_l


def head_out(y, g, gate):
    B_, H_, n, hd = y.shape
    yn = rmsnorm(y.transpose(0, 2, 1, 3), g)
    return (yn * gate.reshape(B_, n, H_, hd)).reshape(B_, n, H_ * hd)


def ab_features(h, w_in, conv_w, conv_b, gate_b, alpha_w2, alpha_b):
    B_, n, _ = h.shape
    a_q, a_k, a_v, a_o, a_gate, b_q, b_k, b_v, b_r, b_lr = split_cols(h @ w_in, AB_SPLITS)
    a_qk = jax.nn.silu(short_conv(jnp.concatenate([a_q, a_k], axis=-1), conv_w, conv_b))
    a_q, a_k = jnp.split(a_qk, 2, axis=-1)
    bhnd = lambda t, hd: heads(t, hd).transpose(0, 2, 1, 3)
    g = (a_gate.reshape(B_, n, 2, 2, A_HEADS).astype(jnp.float32) + gate_b.astype(jnp.float32)).transpose(2, 3, 0, 4, 1)
    ig, lf = g[:, 0], jax.nn.log_sigmoid(g[:, 1])
    za = jnp.einsum('bnzr,zrc->zbnc', b_lr.reshape(B_, n, 2, GLA_RANK), alpha_w2) + alpha_b[:, None, None, :]
    lg = (jax.nn.log_sigmoid(za.astype(jnp.float32)) / GLA_TAU).reshape(2, B_, n, B_HEADS, B_KEY_DIM).transpose(0, 1, 3, 2, 4)
    a_seq = (bhnd(a_q, HEAD_DIM), bhnd(a_k, HEAD_DIM) * HEAD_DIM ** -0.5, bhnd(a_v, HEAD_DIM))
    b_seq = (bhnd(b_q, B_KEY_DIM) * B_KEY_DIM ** -0.5, bhnd(b_k, B_KEY_DIM), bhnd(b_v, HEAD_DIM))
    return a_seq, (ig, lf), a_o, b_seq, (lg,), b_r


def mixer_ab(h_c, h_l, w_in, conv_w, conv_b, gate_b, alpha_w2, alpha_b, head_g, w_out, need_ctx):
    fc = ab_features(h_c, w_in, conv_w, conv_b, gate_b, alpha_w2, alpha_b)
    fl = ab_features(h_l, w_in, conv_w, conv_b, gate_b, alpha_w2, alpha_b)
    B_ = h_l.shape[0]
    a_init = (jnp.zeros((B_, A_HEADS, HEAD_DIM, HEAD_DIM), jnp.float32),
              jnp.zeros((B_, A_HEADS, HEAD_DIM), jnp.float32),
              jnp.zeros((B_, A_HEADS), jnp.float32))
    b_init = jnp.zeros((B_, B_HEADS, B_KEY_DIM, HEAD_DIM), jnp.float32)
    a_c, a_l = bidirectional(mlstm_scan, a_init, fc[0], fc[1], fl[0], fl[1])
    b_c, b_l = bidirectional(gla_scan, b_init, fc[3], fc[4], fl[3], fl[4])

    def merge(a, b, a_o, b_r):
        ya = head_out(a, head_g[0], jax.nn.sigmoid(a_o))
        yb = head_out(b, head_g[1], jax.nn.silu(b_r))
        return jnp.concatenate([ya, yb], axis=-1).astype(a_o.dtype) @ w_out

    y_l = merge(a_l, b_l, fl[2], fl[5])
    y_c = merge(a_c, b_c, fc[2], fc[5]) if need_ctx else None
    return y_c, y_l


def rope_2d(x, row, col):
    d = x.shape[-1]

    def rope_axis(t, pos):
        half = t.shape[-1] // 2
        freqs = ROPE_BASE ** (-jnp.arange(half, dtype=jnp.float32) / half)
        ang = pos.astype(jnp.float32)[:, None] * freqs
        cos, sin = jnp.cos(ang)[:, None, :], jnp.sin(ang)[:, None, :]
        t1, t2 = t[..., :half], t[..., half:]
        return jnp.concatenate([t1 * cos - t2 * sin, t1 * sin + t2 * cos], axis=-1)

    return jnp.concatenate([rope_axis(x[..., :d // 2], row), rope_axis(x[..., d // 2:], col)], axis=-1).astype(x.dtype)


def ctx_attention(q, k, v, sink=None):
    B_, m, hq, d = q.shape
    hkv = k.shape[2]
    qg = q.reshape(B_, m, hkv, hq // hkv, d)
    s = jnp.einsum('bqkgd,bckd->bkgqc', qg, k).astype(jnp.float32)
    if sink is not None:
        s_sink = jnp.broadcast_to(sink.reshape(hkv, hq // hkv)[None, :, :, None, None].astype(jnp.float32), s.shape[:-1] + (1,))
        s = jnp.concatenate([s, s_sink], axis=-1)
    p = jax.nn.softmax(s, axis=-1)[..., :k.shape[1]].astype(v.dtype)
    return jnp.einsum('bkgqc,bckd->bqkgd', p, v).reshape(B_, m, hq, d)


def neighbourhood_attention(q, k, v, k_ctx, v_ctx, rpb):
    B_, n, H_, d = q.shape
    rows = n // GRID_W
    kh, kw = min(WIN_H, rows), WIN_W
    qg, kg, vg = (t.reshape(B_, rows, GRID_W, H_, d) for t in (q, k, v))
    cols = jnp.arange(GRID_W)
    col_start = jnp.clip(cols - kw // 2, 0, GRID_W - kw)
    col_idx = col_start[:, None] + jnp.arange(kw)[None, :]
    col_bias_idx = col_idx - cols[:, None] + (WIN_W - 1)

    def one_row(r):
        rs = jnp.clip(r - kh // 2, 0, rows - kh)
        k_nb = lax.dynamic_slice_in_dim(kg, rs, kh, axis=1)[:, :, col_idx]
        v_nb = lax.dynamic_slice_in_dim(vg, rs, kh, axis=1)[:, :, col_idx]
        q_r = lax.dynamic_index_in_dim(qg, r, axis=1, keepdims=False)
        row_bias_idx = rs + jnp.arange(kh) - r + (WIN_H - 1)
        bias = rpb[:, row_bias_idx][:, :, col_bias_idx].transpose(0, 2, 1, 3).reshape(H_, GRID_W, kh * kw)
        s_nb = jnp.einsum('bwhd,biwjhd->bhwij', q_r, k_nb).reshape(B_, H_, GRID_W, kh * kw)
        s_ctx = jnp.einsum('bwhd,bchd->bhwc', q_r, k_ctx)
        s = jnp.concatenate([s_nb.astype(jnp.float32) + bias.astype(jnp.float32), s_ctx.astype(jnp.float32)], axis=-1)
        p = jax.nn.softmax(s, axis=-1).astype(v.dtype)
        p_nb = p[..., :kh * kw].reshape(B_, H_, GRID_W, kh, kw)
        return (jnp.einsum('bhwij,biwjhd->bwhd', p_nb, v_nb)
                + jnp.einsum('bhwc,bchd->bwhd', p[..., kh * kw:], v_ctx))

    out = lax.map(one_row, jnp.arange(rows))
    return jnp.moveaxis(out, 0, 1).reshape(B_, n, H_, d)


def window_gqa(q, k, v, k_ctx, v_ctx, sink):
    B_, n, _, d = q.shape
    nb = n // WINDOW

    def bands(t):
        tp = jnp.pad(t, ((0, 0), (WINDOW, WINDOW), (0, 0), (0, 0))).reshape(B_, nb + 2, WINDOW, D_KV_HEADS, d)
        return jnp.concatenate([tp[:, :-2], tp[:, 1:-1], tp[:, 2:]], axis=2)

    kb, vb = bands(k), bands(v)
    qb = q.reshape(B_, nb, WINDOW, D_KV_HEADS, D_GROUP, d)
    s_win = jnp.einsum('bnqkgd,bnjkd->bnkgqj', qb, kb).astype(jnp.float32)
    s_ctx = jnp.einsum('bnqkgd,bckd->bnkgqc', qb, k_ctx).astype(jnp.float32)
    qi = jnp.arange(WINDOW)[:, None]
    kj = jnp.arange(3 * WINDOW)[None, :]
    key_pos = (jnp.arange(nb)[:, None, None] - 1) * WINDOW + kj[None]
    allowed = (jnp.abs(kj - qi - WINDOW) <= WINDOW)[None] & (key_pos >= 0) & (key_pos < n)
    s_win = jnp.where(allowed[None, :, None, None], s_win, -jnp.inf)
    s_sink = jnp.broadcast_to(sink.reshape(D_KV_HEADS, D_GROUP)[None, None, :, :, None, None].astype(jnp.float32),
                              s_win.shape[:-1] + (1,))
    p = jax.nn.softmax(jnp.concatenate([s_win, s_ctx, s_sink], axis=-1), axis=-1).astype(v.dtype)
    nw, nc = 3 * WINDOW, k_ctx.shape[1]
    out = (jnp.einsum('bnkgqj,bnjkd->bnqkgd', p[..., :nw], vb)
           + jnp.einsum('bnkgqc,bckd->bnqkgd', p[..., nw:nw + nc], v_ctx))
    return out.reshape(B_, n, D_HEADS, d)


def mixer_cd(h_c, h_l, w_in, qk_g, rpb, sink, w_out, need_ctx):
    B_, n, _ = h_l.shape
    scale = HEAD_DIM ** -0.5
    ck_l, cv_l, dk_l, dv_l, cq_l, dq_l = split_cols(h_l @ w_in, CD_SPLITS)
    if need_ctx:
        ck_c, cv_c, dk_c, dv_c, cq_c, dq_c = split_cols(h_c @ w_in, CD_SPLITS)
    else:
        ck_c, cv_c, dk_c, dv_c = split_cols(h_c @ w_in[:, :CD_KV_COLS], CD_SPLITS[:4])
    ck_c, dk_c = rmsnorm(heads(ck_c, HEAD_DIM), qk_g[1]), rmsnorm(heads(dk_c, HEAD_DIM), qk_g[3])
    cv_c, dv_c = heads(cv_c, HEAD_DIM), heads(dv_c, HEAD_DIM)
    t = jnp.arange(n)
    row, col = t // GRID_W, t % GRID_W
    cq = rmsnorm(heads(cq_l, HEAD_DIM), qk_g[0]) * scale
    ck = rmsnorm(heads(ck_l, HEAD_DIM), qk_g[1])
    dq = rope_2d(rmsnorm(heads(dq_l, HEAD_DIM), qk_g[2]), row, col) * scale
    dk = rope_2d(rmsnorm(heads(dk_l, HEAD_DIM), qk_g[3]), row, col)
    c_out = neighbourhood_attention(cq, ck, heads(cv_l, HEAD_DIM), ck_c, cv_c, rpb)
    d_out = window_gqa(dq, dk, heads(dv_l, HEAD_DIM), dk_c, dv_c, sink)
    y_l = jnp.concatenate([c_out.reshape(B_, n, C_WIDTH), d_out.reshape(B_, n, D_WIDTH)], axis=-1) @ w_out
    y_c = None
    if need_ctx:
        m = h_c.shape[1]
        cq_c = rmsnorm(heads(cq_c, HEAD_DIM), qk_g[0]) * scale
        dq_c = rmsnorm(heads(dq_c, HEAD_DIM), qk_g[2]) * scale
        c_ctx_out = ctx_attention(cq_c, ck_c, cv_c)
        d_ctx_out = ctx_attention(dq_c, dk_c, dv_c, sink)
        y_c = jnp.concatenate([c_ctx_out.reshape(B_, m, C_WIDTH), d_ctx_out.reshape(B_, m, D_WIDTH)], axis=-1) @ w_out
    return y_c, y_l


def peer(h, w_q, sub_keys, u, v_tab):
    B_, n, D = h.shape
    q = (h @ w_q).reshape(B_, n, PEER_HEADS, 2, PEER_QDIM // 2)
    s = jnp.einsum('bnhzd,hzkd->bnhzk', q, sub_keys).astype(jnp.float32)
    s1, i1 = lax.top_k(s[..., 0, :], PEER_TOPK)
    s2, i2 = lax.top_k(s[..., 1, :], PEER_TOPK)
    cand = (s1[..., :, None] + s2[..., None, :]).reshape(B_, n, PEER_HEADS, PEER_TOPK * PEER_TOPK)
    cidx = (i1[..., :, None] * PEER_NKEYS + i2[..., None, :]).reshape(B_, n, PEER_HEADS, PEER_TOPK * PEER_TOPK)
    best, pos = lax.top_k(cand, PEER_TOPK)
    eidx = jnp.take_along_axis(cidx, pos, axis=-1)
    g = jax.nn.softmax(best, axis=-1).astype(h.dtype)
    nb = (B_ * n) // PEER_BLOCK

    def block(args):
        hb, ib, gb = args
        act = jax.nn.gelu(jnp.einsum('td,ted->te', hb, u[ib]))
        return jnp.einsum('te,ted->td', gb * act, v_tab[ib])

    out = lax.map(block, (h.reshape(nb, PEER_BLOCK, D),
                          eidx.reshape(nb, PEER_BLOCK, PEER_HEADS * PEER_TOPK),
                          g.reshape(nb, PEER_BLOCK, PEER_HEADS * PEER_TOPK)))
    return out.reshape(B_, n, D)


def setup_inputs(seed: int = 0) -> dict:
    key = jax.random.key(seed)
    ks = jax.random.split(key, 24)
    D = D_MODEL
    nrm = lambda k, shape, s: jax.random.normal(k, shape, jnp.float32) * s
    f_bias = jnp.linspace(3.0, 6.0, A_HEADS, dtype=jnp.float32)
    if_bias = jnp.stack([jnp.zeros((A_HEADS,), jnp.float32), f_bias])
    return {
        'x': nrm(ks[0], (BATCH, SEQ, D), 1.0),
        'c': nrm(ks[1], (BATCH, D), 1.0),
        'ctx': nrm(ks[2], (BATCH, CTX_LEN, D), 1.0),
        'c_ctx': nrm(ks[3], (D,), 1.0),
        'mod_w': nrm(ks[4], (DEPTH, D, 6 * D), 0.5 * D ** -0.5),
        'mod_b': nrm(ks[5], (DEPTH, 6 * D), 0.02),
        'norm_g': 1.0 + nrm(ks[6], (DEPTH, 2, D), 0.02),
        'ab_w_in': nrm(ks[7], (N_EVEN, D, AB_COLS), D ** -0.5),
        'ab_conv_w': nrm(ks[8], (N_EVEN, CONV_W, 2 * A_WIDTH), CONV_W ** -0.5),
        'ab_conv_b': nrm(ks[9], (N_EVEN, 2 * A_WIDTH), 0.02),
        'ab_gate_b': nrm(ks[10], (N_EVEN, 2, 2, A_HEADS), 0.1) + if_bias[None, None],
        'ab_alpha_w2': nrm(ks[11], (N_EVEN, 2, GLA_RANK, B_KEY_WIDTH), GLA_RANK ** -0.5),
        'ab_alpha_b': 1.0 + nrm(ks[12], (N_EVEN, 2, B_KEY_WIDTH), 0.1),
        'ab_head_g': 1.0 + nrm(ks[13], (N_EVEN, 2, HEAD_DIM), 0.02),
        'ab_w_out': nrm(ks[14], (N_EVEN, MIX_WIDTH, D), MIX_WIDTH ** -0.5),
        'cd_w_in': nrm(ks[15], (N_ODD, D, CD_COLS), D ** -0.5),
        'cd_qk_g': 1.0 + nrm(ks[16], (N_ODD, 4, HEAD_DIM), 0.02),
        'cd_rpb': nrm(ks[17], (N_ODD, C_HEADS, 2 * WIN_H - 1, 2 * WIN_W - 1), 0.1),
        'cd_sink': nrm(ks[18], (N_ODD, D_HEADS), 1.0),
        'cd_w_out': nrm(ks[19], (N_ODD, MIX_WIDTH, D), MIX_WIDTH ** -0.5),
        'peer_w_q': nrm(ks[20], (DEPTH, D, PEER_HEADS * PEER_QDIM), D ** -0.5),
        'peer_sub_keys': nrm(ks[21], (DEPTH, PEER_HEADS, 2, PEER_NKEYS, PEER_QDIM // 2), (PEER_QDIM // 2) ** -0.5),
        'peer_u': nrm(ks[22], (DEPTH, PEER_EXPERTS, D), D ** -0.5),
        'peer_v': nrm(ks[23], (DEPTH, PEER_EXPERTS, D), 0.5),
    }


def reference(x, c, ctx, c_ctx, mod_w, mod_b, norm_g, ab_w_in, ab_conv_w, ab_conv_b, ab_gate_b, ab_alpha_w2,
              ab_alpha_b, ab_head_g, ab_w_out, cd_w_in, cd_qk_g, cd_rpb, cd_sink, cd_w_out, peer_w_q,
              peer_sub_keys, peer_u, peer_v):
    xl, xc = x, ctx
    for l in range(DEPTH):
        need_ctx = l < DEPTH - 1
        j = l // 2
        mod_l = jnp.split((jax.nn.silu(c) @ mod_w[l] + mod_b[l])[:, None, :], 6, axis=-1)
        mod_c = jnp.split((jax.nn.silu(c_ctx) @ mod_w[l] + mod_b[l])[None, None, :], 6, axis=-1)
        hl = modulate(xl, norm_g[l, 0], mod_l[0], mod_l[1])
        hc = modulate(xc, norm_g[l, 0], mod_c[0], mod_c[1])
        if l % 2 == 0:
            yc, yl = mixer_ab(hc, hl, ab_w_in[j], ab_conv_w[j], ab_conv_b[j], ab_gate_b[j], ab_alpha_w2[j],
                              ab_alpha_b[j], ab_head_g[j], ab_w_out[j], need_ctx)
        else:
            yc, yl = mixer_cd(hc, hl, cd_w_in[j], cd_qk_g[j], cd_rpb[j], cd_sink[j], cd_w_out[j], need_ctx)
        xl = xl + mod_l[2] * yl
        hl = modulate(xl, norm_g[l, 1], mod_l[3], mod_l[4])
        xl = xl + mod_l[5] * peer(hl, peer_w_q[l], peer_sub_keys[l], peer_u[l], peer_v[l])
        if need_ctx:
            xc = xc + mod_c[2] * yc
            hc = modulate(xc, norm_g[l, 1], mod_c[3], mod_c[4])
            xc = xc + mod_c[5] * peer(hc, peer_w_q[l], peer_sub_keys[l], peer_u[l], peer_v[l])
    return xl
```

```python
import functools

import numpy as np
import jax
import jax.numpy as jnp
from jax import lax
from jax.experimental import pallas as pl
from jax.experimental.pallas import tpu as pltpu

D_MODEL = 2048
SEQ = 8192
DEPTH = 2
GRID_W = 64
CTX_LEN = 256
HEAD_DIM = 128
N_GROUP_HEADS = 8
EPS = 1e-6
A_HEADS = 8
A_WIDTH = 1024
CONV_W = 3
B_HEADS = 8
B_KEY_DIM = 64
B_KEY_WIDTH = 512
B_VAL_WIDTH = 1024
GLA_RANK = 16
GLA_TAU = 16.0
SCAN_CHUNK = 64
C_HEADS = 8
C_WIDTH = 1024
WIN_H = 8
WIN_W = 16
D_HEADS = 8
D_KV_HEADS = 2
D_GROUP = 4
D_WIDTH = 1024
D_KV_WIDTH = 256
WINDOW = 128
ROPE_BASE = 10000.0
PEER_HEADS = 8
PEER_NKEYS = 128
PEER_EXPERTS = PEER_NKEYS * PEER_NKEYS
PEER_QDIM = 256
PEER_TOPK = 16
PEER_SLOTS = PEER_HEADS * PEER_TOPK
AB_SPLITS = (A_WIDTH, A_WIDTH, A_WIDTH, A_WIDTH, 4 * A_HEADS,
             B_KEY_WIDTH, B_KEY_WIDTH, B_VAL_WIDTH, B_VAL_WIDTH, 2 * GLA_RANK)
CD_SPLITS = (C_WIDTH, C_WIDTH, D_KV_WIDTH, D_KV_WIDTH, C_WIDTH, D_WIDTH)
CD_KV_COLS = sum(CD_SPLITS[:4])

VMEM_LIMIT_BYTES = 56 * 1024 * 1024
LANES = 128


def _row_tile(m):
    for t in (1024, 768, 512, 256, 128):
        if m % t == 0:
            return t
    raise ValueError(f"unsupported row count {m}")


def _mm_kernel(x_ref, w_ref, o_ref):
    o_ref[...] = jnp.dot(x_ref[...], w_ref[...], preferred_element_type=jnp.float32).astype(o_ref.dtype)


def _matmul(x, w, out_dtype=jnp.float32):
    m, k = x.shape
    n = w.shape[1]
    bn = 512
    n_pad = -(-n // bn) * bn
    if n_pad != n:
        w = jnp.pad(w, ((0, 0), (0, n_pad - n)))
    bm = _row_tile(m)
    out = pl.pallas_call(
        _mm_kernel,
        grid=(m // bm, n_pad // bn),
        in_specs=[pl.BlockSpec((bm, k), lambda i, j: (i, 0)),
                  pl.BlockSpec((k, bn), lambda i, j: (0, j))],
        out_specs=pl.BlockSpec((bm, bn), lambda i, j: (i, j)),
        out_shape=jax.ShapeDtypeStruct((m, n_pad), out_dtype),
        compiler_params=pltpu.CompilerParams(
            dimension_semantics=("parallel", "arbitrary"), vmem_limit_bytes=VMEM_LIMIT_BYTES),
        name="matmul",
    )(x.astype(jnp.bfloat16), w.astype(jnp.bfloat16))
    return out[:, :n] if n_pad != n else out


def _gate_matrix_kernel(i1_ref, i2_ref, g_ref, o_ref, *, tb):
    row_id = lax.broadcasted_iota(jnp.int32, (PEER_NKEYS, PEER_SLOTS), 0)

    def body(t, carry):
        i1 = i1_ref[pl.ds(t, 1), :]
        i2 = i2_ref[pl.ds(t, 1), :]
        g = g_ref[pl.ds(t, 1), :]
        c = jnp.where(row_id == i1, g, 0.0).astype(jnp.bfloat16)
        o2 = jnp.where(row_id == i2, 1.0, 0.0).astype(jnp.bfloat16)
        gt = lax.dot_general(c, o2, (((1,), (1,)), ((), ())), preferred_element_type=jnp.float32)
        o_ref[t] = gt.astype(o_ref.dtype)
        return carry

    lax.fori_loop(0, tb, body, 0)


def _gate_matrix(i1, i2, g):
    t = i1.shape[0]
    tb = 64
    out = pl.pallas_call(
        functools.partial(_gate_matrix_kernel, tb=tb),
        grid=(t // tb,),
        in_specs=[pl.BlockSpec((tb, PEER_SLOTS), lambda i: (i, 0))] * 3,
        out_specs=pl.BlockSpec((tb, PEER_NKEYS, PEER_NKEYS), lambda i: (i, 0, 0)),
        out_shape=jax.ShapeDtypeStruct((t, PEER_NKEYS, PEER_NKEYS), jnp.bfloat16),
        compiler_params=pltpu.CompilerParams(
            dimension_semantics=("parallel",), vmem_limit_bytes=VMEM_LIMIT_BYTES),
        name="peer_gate_matrix",
    )(i1, i2, g)
    return out.reshape(t, PEER_EXPERTS)


def _gelu_tanh(x):
    return 0.5 * x * (1.0 + jnp.tanh(0.7978845608028654 * (x + 0.044715 * x * x * x)))


def _peer_dense_kernel(h_ref, ut_ref, v_ref, g_ref, o_ref):
    j = pl.program_id(1)
    act = _gelu_tanh(jnp.dot(h_ref[...], ut_ref[...], preferred_element_type=jnp.float32))
    w = (act * g_ref[...].astype(jnp.float32)).astype(jnp.bfloat16)
    contrib = jnp.dot(w, v_ref[...], preferred_element_type=jnp.float32)

    @pl.when(j == 0)
    def _():
        o_ref[...] = contrib

    @pl.when(j != 0)
    def _():
        o_ref[...] += contrib


def _peer_dense(h, ut, v, gmat):
    t, d = h.shape
    e = v.shape[0]
    bm = _row_tile(t)
    be = 512
    return pl.pallas_call(
        _peer_dense_kernel,
        grid=(t // bm, e // be),
        in_specs=[pl.BlockSpec((bm, d), lambda i, j: (i, 0)),
                  pl.BlockSpec((d, be), lambda i, j: (0, j)),
                  pl.BlockSpec((be, d), lambda i, j: (j, 0)),
                  pl.BlockSpec((bm, be), lambda i, j: (i, j))],
        out_specs=pl.BlockSpec((bm, d), lambda i, j: (i, 0)),
        out_shape=jax.ShapeDtypeStruct((t, d), jnp.float32),
        compiler_params=pltpu.CompilerParams(
            dimension_semantics=("parallel", "arbitrary"), vmem_limit_bytes=VMEM_LIMIT_BYTES),
        name="peer_dense",
    )(h, ut, v, gmat)


def _peer(h, w_q, sub_keys, ut, v):
    t = h.shape[0]
    hb = h.astype(jnp.bfloat16)
    q = _matmul(hb, w_q).reshape(t, PEER_HEADS, 2, PEER_QDIM // 2)
    s = jnp.einsum('nhzd,hzkd->nhzk', q, sub_keys).astype(jnp.float32)
    s1, i1 = lax.top_k(s[..., 0, :], PEER_TOPK)
    s2, i2 = lax.top_k(s[..., 1, :], PEER_TOPK)
    cand = (s1[..., :, None] + s2[..., None, :]).reshape(t, PEER_HEADS, PEER_TOPK * PEER_TOPK)
    best, pos = lax.top_k(cand, PEER_TOPK)
    e1 = jnp.take_along_axis(i1, pos // PEER_TOPK, axis=-1).reshape(t, PEER_SLOTS)
    e2 = jnp.take_along_axis(i2, pos % PEER_TOPK, axis=-1).reshape(t, PEER_SLOTS)
    g = jax.nn.softmax(best, axis=-1).reshape(t, PEER_SLOTS)
    gmat = _gate_matrix(e1.astype(jnp.int32), e2.astype(jnp.int32), g)
    return _peer_dense(hb, ut, v, gmat)


def _split_cols(p, sizes):
    return jnp.split(p, np.cumsum(sizes)[:-1].tolist(), axis=-1)


def _rmsnorm(x, g):
    xf = x.astype(jnp.float32)
    y = xf * lax.rsqrt(jnp.mean(xf * xf, axis=-1, keepdims=True) + EPS)
    return (y * g.astype(jnp.float32)).astype(x.dtype)


def _modulate(x, g, shift, scale):
    return _rmsnorm(x, g) * (1 + scale) + shift


def _heads(t, hd):
    return t.reshape(t.shape[0], t.shape[1], -1, hd)


def _short_conv(x, w, b):
    pad = CONV_W // 2
    n = x.shape[1]
    xp = jnp.pad(x, ((0, 0), (pad, pad), (0, 0)))
    return sum(w[j] * xp[:, j:j + n] for j in range(CONV_W)) + b


def _to_chunks(t):
    B_, H_, n = t.shape[:3]
    return jnp.moveaxis(t.reshape(B_, H_, n // SCAN_CHUNK, SCAN_CHUNK, *t.shape[3:]), 2, 0)


def _from_chunks(t):
    nc, B_, H_, L = t.shape[:4]
    return jnp.moveaxis(t, 0, 2).reshape(B_, H_, nc * L, *t.shape[4:])


def _mlstm_scan(q, k, v, ig, lf, state):
    q, k, v, ig, lf = (t.astype(jnp.float32) for t in (q, k, v, ig, lf))
    mask = jnp.tril(jnp.ones((SCAN_CHUNK, SCAN_CHUNK), bool))

    def step(carry, inp):
        C, nv, m = carry
        qc, kc, vc, ic, fc = inp
        b = jnp.cumsum(fc, axis=-1)
        dlog = jnp.where(mask, b[..., :, None] - b[..., None, :] + ic[..., None, :], -jnp.inf)
        m_inter = b + m[..., None]
        m_t = jnp.maximum(m_inter, jnp.max(dlog, axis=-1))
        sc = jnp.einsum('bhtd,bhsd->bhts', qc, kc) * jnp.exp(dlog - m_t[..., None])
        w_inter = jnp.exp(m_inter - m_t)
        num = jnp.einsum('bhts,bhse->bhte', sc, vc) + w_inter[..., None] * jnp.einsum('bhtd,bhde->bhte', qc, C)
        den = jnp.sum(sc, axis=-1) + w_inter * jnp.einsum('bhtd,bhd->bht', qc, nv)
        h = num / jnp.maximum(jnp.abs(den), jnp.exp(-m_t))[..., None]
        m_new = m_t[..., -1]
        w_end = jnp.exp(b[..., -1:] - b + ic - m_new[..., None])
        decay = jnp.exp(b[..., -1] + m - m_new)
        C = decay[..., None, None] * C + jnp.einsum('bhs,bhsd,bhse->bhde', w_end, kc, vc)
        nv = decay[..., None] * nv + jnp.einsum('bhs,bhsd->bhd', w_end, kc)
        return (C, nv, m_new), h

    state, h = lax.scan(step, state, tuple(_to_chunks(t) for t in (q, k, v, ig, lf)))
    return _from_chunks(h), state


def _gla_scan(q, k, v, lg, S0):
    q, k, v, lg = (t.astype(jnp.float32) for t in (q, k, v, lg))
    mask = jnp.tril(jnp.ones((SCAN_CHUNK, SCAN_CHUNK), bool))[:, :, None]

    def step(S, inp):
        qc, kc, vc, gc = inp
        b = jnp.cumsum(gc, axis=2)
        rel = jnp.where(mask, b[:, :, :, None, :] - b[:, :, None, :, :], -jnp.inf)
        attn = jnp.einsum('bhtd,bhtsd,bhsd->bhts', qc, jnp.exp(rel), kc)
        o = jnp.einsum('bhts,bhse->bhte', attn, vc) + jnp.einsum('bhtd,bhde->bhte', qc * jnp.exp(b), S)
        b_end = b[:, :, -1:, :]
        S = jnp.exp(b_end[:, :, 0])[..., None] * S + jnp.einsum('bhsd,bhse->bhde', kc * jnp.exp(b_end - b), vc)
        return S, o

    S, o = lax.scan(step, S0, tuple(_to_chunks(t) for t in (q, k, v, lg)))
    return _from_chunks(o), S


def _bidirectional(scan_fn, init, ctx_seq, ctx_gates, lat_seq, lat_gates):
    y_c, y_l = None, None
    for z in range(2):
        fc = tuple(ctx_seq) + tuple(g[z] for g in ctx_gates)
        fl = tuple(lat_seq) + tuple(g[z] for g in lat_gates)
        if z == 1:
            fc = tuple(jnp.flip(t, axis=2) for t in fc)
            fl = tuple(jnp.flip(t, axis=2) for t in fl)
        oc, st = scan_fn(*fc, init)
        ol, _ = scan_fn(*fl, st)
        if z == 1:
            oc, ol = jnp.flip(oc, axis=2), jnp.flip(ol, axis=2)
        y_c = oc if y_c is None else y_c + oc
        y_l = ol if y_l is None else y_l + ol
    return y_c, y_l


def _head_out(y, g, gate):
    B_, H_, n, hd = y.shape
    yn = _rmsnorm(y.transpose(0, 2, 1, 3), g)
    return (yn * gate.reshape(B_, n, H_, hd)).reshape(B_, n, H_ * hd)


def _ab_features(h, w_in, conv_w, conv_b, gate_b, alpha_w2, alpha_b):
    B_, n, _ = h.shape
    p = _matmul(h.reshape(B_ * n, -1), w_in).reshape(B_, n, -1)
    a_q, a_k, a_v, a_o, a_gate, b_q, b_k, b_v, b_r, b_lr = _split_cols(p, AB_SPLITS)
    a_qk = jax.nn.silu(_short_conv(jnp.concatenate([a_q, a_k], axis=-1), conv_w, conv_b))
    a_q, a_k = jnp.split(a_qk, 2, axis=-1)
    bhnd = lambda t, hd: _heads(t, hd).transpose(0, 2, 1, 3)
    g = (a_gate.reshape(B_, n, 2, 2, A_HEADS).astype(jnp.float32) + gate_b.astype(jnp.float32)).transpose(2, 3, 0, 4, 1)
    ig, lf = g[:, 0], jax.nn.log_sigmoid(g[:, 1])
    za = jnp.einsum('bnzr,zrc->zbnc', b_lr.reshape(B_, n, 2, GLA_RANK), alpha_w2,
                    precision=lax.Precision.HIGHEST) + alpha_b[:, None, None, :]
    lg = (jax.nn.log_sigmoid(za.astype(jnp.float32)) / GLA_TAU).reshape(2, B_, n, B_HEADS, B_KEY_DIM).transpose(0, 1, 3, 2, 4)
    a_seq = (bhnd(a_q, HEAD_DIM), bhnd(a_k, HEAD_DIM) * HEAD_DIM ** -0.5, bhnd(a_v, HEAD_DIM))
    b_seq = (bhnd(b_q, B_KEY_DIM) * B_KEY_DIM ** -0.5, bhnd(b_k, B_KEY_DIM), bhnd(b_v, HEAD_DIM))
    return a_seq, (ig, lf), a_o, b_seq, (lg,), b_r


def _mixer_ab(h_c, h_l, w_in, conv_w, conv_b, gate_b, alpha_w2, alpha_b, head_g, w_out, need_ctx):
    fc = _ab_features(h_c, w_in, conv_w, conv_b, gate_b, alpha_w2, alpha_b)
    fl = _ab_features(h_l, w_in, conv_w, conv_b, gate_b, alpha_w2, alpha_b)
    B_ = h_l.shape[0]
    a_init = (jnp.zeros((B_, A_HEADS, HEAD_DIM, HEAD_DIM), jnp.float32),
              jnp.zeros((B_, A_HEADS, HEAD_DIM), jnp.float32),
              jnp.zeros((B_, A_HEADS), jnp.float32))
    b_init = jnp.zeros((B_, B_HEADS, B_KEY_DIM, HEAD_DIM), jnp.float32)
    a_c, a_l = _bidirectional(_mlstm_scan, a_init, fc[0], fc[1], fl[0], fl[1])
    b_c, b_l = _bidirectional(_gla_scan, b_init, fc[3], fc[4], fl[3], fl[4])

    def merge(a, b, a_o, b_r):
        ya = _head_out(a, head_g[0], jax.nn.sigmoid(a_o))
        yb = _head_out(b, head_g[1], jax.nn.silu(b_r))
        y = jnp.concatenate([ya, yb], axis=-1)
        return _matmul(y.reshape(-1, y.shape[-1]), w_out).reshape(y.shape[0], y.shape[1], -1)

    y_l = merge(a_l, b_l, fl[2], fl[5])
    y_c = merge(a_c, b_c, fc[2], fc[5]) if need_ctx else None
    return y_c, y_l


def _rope_2d(x, row, col):
    d = x.shape[-1]

    def rope_axis(t, pos):
        half = t.shape[-1] // 2
        freqs = ROPE_BASE ** (-jnp.arange(half, dtype=jnp.float32) / half)
        ang = pos.astype(jnp.float32)[:, None] * freqs
        cos, sin = jnp.cos(ang)[:, None, :], jnp.sin(ang)[:, None, :]
        t1, t2 = t[..., :half], t[..., half:]
        return jnp.concatenate([t1 * cos - t2 * sin, t1 * sin + t2 * cos], axis=-1)

    return jnp.concatenate([rope_axis(x[..., :d // 2], row), rope_axis(x[..., d // 2:], col)], axis=-1).astype(x.dtype)


def _neighbourhood_attention(q, k, v, k_ctx, v_ctx, rpb):
    B_, n, H_, d = q.shape
    rows = n // GRID_W
    kh, kw = min(WIN_H, rows), WIN_W
    qg, kg, vg = (t.reshape(B_, rows, GRID_W, H_, d) for t in (q, k, v))
    cols = jnp.arange(GRID_W)
    col_start = jnp.clip(cols - kw // 2, 0, GRID_W - kw)
    col_idx = col_start[:, None] + jnp.arange(kw)[None, :]
    col_bias_idx = col_idx - cols[:, None] + (WIN_W - 1)

    def one_row(r):
        rs = jnp.clip(r - kh // 2, 0, rows - kh)
        k_nb = lax.dynamic_slice_in_dim(kg, rs, kh, axis=1)[:, :, col_idx]
        v_nb = lax.dynamic_slice_in_dim(vg, rs, kh, axis=1)[:, :, col_idx]
        q_r = lax.dynamic_index_in_dim(qg, r, axis=1, keepdims=False)
        row_bias_idx = rs + jnp.arange(kh) - r + (WIN_H - 1)
        bias = rpb[:, row_bias_idx][:, :, col_bias_idx].transpose(0, 2, 1, 3).reshape(H_, GRID_W, kh * kw)
        s_nb = jnp.einsum('bwhd,biwjhd->bhwij', q_r, k_nb).reshape(B_, H_, GRID_W, kh * kw)
        s_ctx = jnp.einsum('bwhd,bchd->bhwc', q_r, k_ctx)
        s = jnp.concatenate([s_nb.astype(jnp.float32) + bias.astype(jnp.float32), s_ctx.astype(jnp.float32)], axis=-1)
        p = jax.nn.softmax(s, axis=-1).astype(v.dtype)
        p_nb = p[..., :kh * kw].reshape(B_, H_, GRID_W, kh, kw)
        return (jnp.einsum('bhwij,biwjhd->bwhd', p_nb, v_nb)
                + jnp.einsum('bhwc,bchd->bwhd', p[..., kh * kw:], v_ctx))

    out = lax.map(one_row, jnp.arange(rows))
    return jnp.moveaxis(out, 0, 1).reshape(B_, n, H_, d)


def _window_gqa(q, k, v, k_ctx, v_ctx, sink):
    B_, n, _, d = q.shape
    nb = n // WINDOW

    def bands(t):
        tp = jnp.pad(t, ((0, 0), (WINDOW, WINDOW), (0, 0), (0, 0))).reshape(B_, nb + 2, WINDOW, D_KV_HEADS, d)
        return jnp.concatenate([tp[:, :-2], tp[:, 1:-1], tp[:, 2:]], axis=2)

    kb, vb = bands(k), bands(v)
    qb = q.reshape(B_, nb, WINDOW, D_KV_HEADS, D_GROUP, d)
    s_win = jnp.einsum('bnqkgd,bnjkd->bnkgqj', qb, kb).astype(jnp.float32)
    s_ctx = jnp.einsum('bnqkgd,bckd->bnkgqc', qb, k_ctx).astype(jnp.float32)
    qi = jnp.arange(WINDOW)[:, None]
    kj = jnp.arange(3 * WINDOW)[None, :]
    key_pos = (jnp.arange(nb)[:, None, None] - 1) * WINDOW + kj[None]
    allowed = (jnp.abs(kj - qi - WINDOW) <= WINDOW)[None] & (key_pos >= 0) & (key_pos < n)
    s_win = jnp.where(allowed[None, :, None, None], s_win, -jnp.inf)
    s_sink = jnp.broadcast_to(sink.reshape(D_KV_HEADS, D_GROUP)[None, None, :, :, None, None].astype(jnp.float32),
                              s_win.shape[:-1] + (1,))
    p = jax.nn.softmax(jnp.concatenate([s_win, s_ctx, s_sink], axis=-1), axis=-1).astype(v.dtype)
    nw, nc = 3 * WINDOW, k_ctx.shape[1]
    out = (jnp.einsum('bnkgqj,bnjkd->bnqkgd', p[..., :nw], vb)
           + jnp.einsum('bnkgqc,bckd->bnqkgd', p[..., nw:nw + nc], v_ctx))
    return out.reshape(B_, n, D_HEADS, d)


def _mixer_cd(h_c, h_l, w_in, qk_g, rpb, sink, w_out):
    B_, n, _ = h_l.shape
    scale = HEAD_DIM ** -0.5
    pl_ = _matmul(h_l.reshape(B_ * n, -1), w_in).reshape(B_, n, -1)
    ck_l, cv_l, dk_l, dv_l, cq_l, dq_l = _split_cols(pl_, CD_SPLITS)
    pc_ = _matmul(h_c.reshape(-1, h_c.shape[-1]), w_in[:, :CD_KV_COLS]).reshape(B_, h_c.shape[1], -1)
    ck_c, cv_c, dk_c, dv_c = _split_cols(pc_, CD_SPLITS[:4])
    ck_c, dk_c = _rmsnorm(_heads(ck_c, HEAD_DIM), qk_g[1]), _rmsnorm(_heads(dk_c, HEAD_DIM), qk_g[3])
    cv_c, dv_c = _heads(cv_c, HEAD_DIM), _heads(dv_c, HEAD_DIM)
    t = jnp.arange(n)
    row, col = t // GRID_W, t % GRID_W
    cq = _rmsnorm(_heads(cq_l, HEAD_DIM), qk_g[0]) * scale
    ck = _rmsnorm(_heads(ck_l, HEAD_DIM), qk_g[1])
    dq = _rope_2d(_rmsnorm(_heads(dq_l, HEAD_DIM), qk_g[2]), row, col) * scale
    dk = _rope_2d(_rmsnorm(_heads(dk_l, HEAD_DIM), qk_g[3]), row, col)
    c_out = _neighbourhood_attention(cq, ck, _heads(cv_l, HEAD_DIM), ck_c, cv_c, rpb)
    d_out = _window_gqa(dq, dk, _heads(dv_l, HEAD_DIM), dk_c, dv_c, sink)
    y = jnp.concatenate([c_out.reshape(B_, n, C_WIDTH), d_out.reshape(B_, n, D_WIDTH)], axis=-1)
    return _matmul(y.reshape(B_ * n, -1), w_out).reshape(B_, n, -1)


def kernel(x, c, ctx, c_ctx, mod_w, mod_b, norm_g, ab_w_in, ab_conv_w, ab_conv_b, ab_gate_b, ab_alpha_w2,
           ab_alpha_b, ab_head_g, ab_w_out, cd_w_in, cd_qk_g, cd_rpb, cd_sink, cd_w_out, peer_w_q,
           peer_sub_keys, peer_u, peer_v):
    xl, xc = x, ctx
    for l in range(DEPTH):
        need_ctx = l < DEPTH - 1
        j = l // 2
        mod_l = jnp.split((jax.nn.silu(c) @ mod_w[l] + mod_b[l])[:, None, :], 6, axis=-1)
        mod_c = jnp.split((jax.nn.silu(c_ctx) @ mod_w[l] + mod_b[l])[None, None, :], 6, axis=-1)
        hl = _modulate(xl, norm_g[l, 0], mod_l[0], mod_l[1])
        hc = _modulate(xc, norm_g[l, 0], mod_c[0], mod_c[1])
        if l % 2 == 0:
            yc, yl = _mixer_ab(hc, hl, ab_w_in[j], ab_conv_w[j], ab_conv_b[j], ab_gate_b[j], ab_alpha_w2[j],
                               ab_alpha_b[j], ab_head_g[j], ab_w_out[j], need_ctx)
        else:
            yc, yl = None, _mixer_cd(hc, hl, cd_w_in[j], cd_qk_g[j], cd_rpb[j], cd_sink[j], cd_w_out[j])
        ut = peer_u[l].T.astype(jnp.bfloat16)
        vt = peer_v[l].astype(jnp.bfloat16)
        xl = xl + mod_l[2] * yl
        hl = _modulate(xl, norm_g[l, 1], mod_l[3], mod_l[4])
        xl = xl + mod_l[5] * _peer(hl[0], peer_w_q[l], peer_sub_keys[l], ut, vt)[None]
        if need_ctx:
            xc = xc + mod_c[2] * yc
            hc = _modulate(xc, norm_g[l, 1], mod_c[3], mod_c[4])
            xc = xc + mod_c[5] * _peer(hc[0], peer_w_q[l], peer_sub_keys[l], ut, vt)[None]
    return xl
```

```python
import functools

import numpy as np
import jax
import jax.numpy as jnp
from jax import lax
from jax.experimental import pallas as pl
from jax.experimental.pallas import tpu as pltpu

D_MODEL = 2048
SEQ = 8192
DEPTH = 2
GRID_W = 64
CTX_LEN = 256
HEAD_DIM = 128
N_GROUP_HEADS = 8
EPS = 1e-6
A_HEADS = 8
A_WIDTH = 1024
CONV_W = 3
B_HEADS = 8
B_KEY_DIM = 64
B_KEY_WIDTH = 512
B_VAL_WIDTH = 1024
GLA_RANK = 16
GLA_TAU = 16.0
SCAN_CHUNK = 64
C_HEADS = 8
C_WIDTH = 1024
WIN_H = 8
WIN_W = 16
D_HEADS = 8
D_KV_HEADS = 2
D_GROUP = 4
D_WIDTH = 1024
D_KV_WIDTH = 256
WINDOW = 128
ROPE_BASE = 10000.0
PEER_HEADS = 8
PEER_NKEYS = 128
PEER_EXPERTS = PEER_NKEYS * PEER_NKEYS
PEER_QDIM = 256
PEER_TOPK = 16
PEER_SLOTS = PEER_HEADS * PEER_TOPK
AB_SPLITS = (A_WIDTH, A_WIDTH, A_WIDTH, A_WIDTH, 4 * A_HEADS,
             B_KEY_WIDTH, B_KEY_WIDTH, B_VAL_WIDTH, B_VAL_WIDTH, 2 * GLA_RANK)
CD_SPLITS = (C_WIDTH, C_WIDTH, D_KV_WIDTH, D_KV_WIDTH, C_WIDTH, D_WIDTH)
CD_KV_COLS = sum(CD_SPLITS[:4])

VMEM_LIMIT_BYTES = 56 * 1024 * 1024
LANES = 128


def _row_tile(m):
    for t in (1024, 768, 512, 256, 128):
        if m % t == 0:
            return t
    raise ValueError(f"unsupported row count {m}")


def _mm_kernel(x_ref, w_ref, o_ref):
    o_ref[...] = jnp.dot(x_ref[...], w_ref[...], preferred_element_type=jnp.float32).astype(o_ref.dtype)


def _matmul(x, w, out_dtype=jnp.float32):
    m, k = x.shape
    n = w.shape[1]
    bn = 512
    n_pad = -(-n // bn) * bn
    if n_pad != n:
        w = jnp.pad(w, ((0, 0), (0, n_pad - n)))
    bm = _row_tile(m)
    out = pl.pallas_call(
        _mm_kernel,
        grid=(m // bm, n_pad // bn),
        in_specs=[pl.BlockSpec((bm, k), lambda i, j: (i, 0)),
                  pl.BlockSpec((k, bn), lambda i, j: (0, j))],
        out_specs=pl.BlockSpec((bm, bn), lambda i, j: (i, j)),
        out_shape=jax.ShapeDtypeStruct((m, n_pad), out_dtype),
        compiler_params=pltpu.CompilerParams(
            dimension_semantics=("parallel", "arbitrary"), vmem_limit_bytes=VMEM_LIMIT_BYTES),
        name="matmul",
    )(x.astype(jnp.bfloat16), w.astype(jnp.bfloat16))
    return out[:, :n] if n_pad != n else out


ROUTE_TOKENS = 256
GATE_TOKENS = 128
NEG_INF = float("-inf")


def _top16(s, key, big):
    slot = lax.broadcasted_iota(jnp.int32, (PEER_TOPK, s.shape[1]), 0)
    vals = jnp.zeros((PEER_TOPK, s.shape[1]), jnp.float32)
    keys = jnp.zeros((PEER_TOPK, s.shape[1]), jnp.int32)
    for it in range(PEER_TOPK):
        m = jnp.max(s, axis=0, keepdims=True)
        k = jnp.min(jnp.where(s == m, key, big), axis=0, keepdims=True)
        vals = jnp.where(slot == it, m, vals)
        keys = jnp.where(slot == it, k, keys)
        s = jnp.where(key == k, NEG_INF, s)
    return vals, keys


def _route_kernel(h_ref, wq_ref, k_ref, i1_ref, i2_ref, g_ref):
    bt = h_ref.shape[0]
    q = jnp.dot(h_ref[...], wq_ref[...], preferred_element_type=jnp.float32).astype(jnp.bfloat16)
    nt = (((1,), (1,)), ((), ()))
    s1 = lax.dot_general(k_ref[0, 0], q[:, :PEER_NKEYS], nt, preferred_element_type=jnp.float32)
    s2 = lax.dot_general(k_ref[0, 1], q[:, PEER_NKEYS:], nt, preferred_element_type=jnp.float32)
    row = lax.broadcasted_iota(jnp.int32, (PEER_NKEYS, bt), 0)
    v1, r1 = _top16(s1, row, PEER_NKEYS)
    v2, r2 = _top16(s2, row, PEER_NKEYS)
    i16 = lax.broadcasted_iota(jnp.int32, (16, bt), 0)
    i8 = lax.broadcasted_iota(jnp.int32, (8, bt), 0)
    cand = [v1[0:1] + v2]
    flat = [i16]
    for a in range(1, 8):
        cand.append(v1[a:a + 1] + v2[0:8])
        flat.append(i8 + a * PEER_TOPK)
    cand.append(v1[8:16] + v2[0:1])
    flat.append((i8 + 8) * PEER_TOPK)
    best, key = _top16(jnp.concatenate(cand, axis=0), jnp.concatenate(flat, axis=0), PEER_TOPK * PEER_TOPK)
    a_sel = jnp.right_shift(key, 4)
    b_sel = jnp.bitwise_and(key, PEER_TOPK - 1)
    e1 = jnp.zeros_like(key)
    e2 = jnp.zeros_like(key)
    for r in range(PEER_TOPK):
        e1 = jnp.where(a_sel == r, r1[r:r + 1], e1)
        e2 = jnp.where(b_sel == r, r2[r:r + 1], e2)
    ex = jnp.exp(best - best[0:1])
    i1_ref[...] = e1
    i2_ref[...] = e2
    g_ref[...] = ex / jnp.sum(ex, axis=0, keepdims=True)


def _route(hb, wq, keys):
    t, d = hb.shape
    bt = ROUTE_TOKENS
    slot_spec = pl.BlockSpec((PEER_TOPK, bt), lambda i, h: (h, i))
    return pl.pallas_call(
        _route_kernel,
        grid=(t // bt, PEER_HEADS),
        in_specs=[pl.BlockSpec((bt, d), lambda i, h: (i, 0)),
                  pl.BlockSpec((d, PEER_QDIM), lambda i, h: (0, h)),
                  pl.BlockSpec((1, 2, PEER_NKEYS, PEER_QDIM // 2), lambda i, h: (h, 0, 0, 0))],
        out_specs=[slot_spec, slot_spec, slot_spec],
        out_shape=[jax.ShapeDtypeStruct((PEER_SLOTS, t), jnp.int32),
                   jax.ShapeDtypeStruct((PEER_SLOTS, t), jnp.int32),
                   jax.ShapeDtypeStruct((PEER_SLOTS, t), jnp.float32)],
        compiler_params=pltpu.CompilerParams(
            dimension_semantics=("parallel", "arbitrary"), vmem_limit_bytes=VMEM_LIMIT_BYTES),
        name="peer_route",
    )(hb, wq, keys)


def _gate_matrix_kernel(i1_ref, i2_ref, g_ref, o_ref, i1_t, i2_t, g_t):
    tb = o_ref.shape[0]
    i1_t[...] = i1_ref[...].T
    i2_t[...] = i2_ref[...].T
    g_t[...] = g_ref[...].T
    row_id = lax.broadcasted_iota(jnp.int32, (PEER_NKEYS, PEER_SLOTS), 0)
    zero = jnp.zeros((PEER_NKEYS, PEER_SLOTS), jnp.bfloat16)

    def body(tp, carry):
        ta = 2 * tp
        tb_ = ta + 1
        ca = jnp.where(row_id == i1_t[pl.ds(ta, 1), :], g_t[pl.ds(ta, 1), :], 0.0).astype(jnp.bfloat16)
        cb = jnp.where(row_id == i1_t[pl.ds(tb_, 1), :], g_t[pl.ds(tb_, 1), :], 0.0).astype(jnp.bfloat16)
        oa = jnp.where(row_id == i2_t[pl.ds(ta, 1), :], 1.0, 0.0).astype(jnp.bfloat16)
        ob = jnp.where(row_id == i2_t[pl.ds(tb_, 1), :], 1.0, 0.0).astype(jnp.bfloat16)
        lhs = jnp.concatenate([ca, cb], axis=1)
        rhs = jnp.concatenate([jnp.concatenate([oa, zero], axis=1),
                               jnp.concatenate([zero, ob], axis=1)], axis=0)
        out = lax.dot_general(lhs, rhs, (((1,), (1,)), ((), ())), preferred_element_type=jnp.float32)
        o_ref[ta] = out[:, :PEER_NKEYS].astype(o_ref.dtype)
        o_ref[tb_] = out[:, PEER_NKEYS:].astype(o_ref.dtype)
        return carry

    lax.fori_loop(0, tb // 2, body, 0, unroll=4)


def _gate_matrix(i1, i2, g):
    t = i1.shape[1]
    tb = GATE_TOKENS
    out = pl.pallas_call(
        _gate_matrix_kernel,
        grid=(t // tb,),
        in_specs=[pl.BlockSpec((PEER_SLOTS, tb), lambda i: (0, i))] * 3,
        out_specs=pl.BlockSpec((tb, PEER_NKEYS, PEER_NKEYS), lambda i: (i, 0, 0)),
        out_shape=jax.ShapeDtypeStruct((t, PEER_NKEYS, PEER_NKEYS), jnp.bfloat16),
        scratch_shapes=[pltpu.VMEM((tb, PEER_SLOTS), jnp.int32),
                        pltpu.VMEM((tb, PEER_SLOTS), jnp.int32),
                        pltpu.VMEM((tb, PEER_SLOTS), jnp.float32)],
        compiler_params=pltpu.CompilerParams(
            dimension_semantics=("parallel",), vmem_limit_bytes=VMEM_LIMIT_BYTES),
        name="peer_gate_matrix",
    )(i1, i2, g)
    return out.reshape(t, PEER_EXPERTS)


def _gelu_tanh(x):
    return 0.5 * x * (1.0 + jnp.tanh(0.7978845608028654 * (x + 0.044715 * x * x * x)))


def _peer_dense_kernel(h_ref, ut_ref, v_ref, g_ref, o_ref):
    j = pl.program_id(1)
    act = _gelu_tanh(jnp.dot(h_ref[...], ut_ref[...], preferred_element_type=jnp.float32))
    w = (act * g_ref[...].astype(jnp.float32)).astype(jnp.bfloat16)
    contrib = jnp.dot(w, v_ref[...], preferred_element_type=jnp.float32)

    @pl.when(j == 0)
    def _():
        o_ref[...] = contrib

    @pl.when(j != 0)
    def _():
        o_ref[...] += contrib


def _peer_dense(h, ut, v, gmat):
    t, d = h.shape
    e = v.shape[0]
    bm = _row_tile(t)
    be = 512
    return pl.pallas_call(
        _peer_dense_kernel,
        grid=(t // bm, e // be),
        in_specs=[pl.BlockSpec((bm, d), lambda i, j: (i, 0)),
                  pl.BlockSpec((d, be), lambda i, j: (0, j)),
                  pl.BlockSpec((be, d), lambda i, j: (j, 0)),
                  pl.BlockSpec((bm, be), lambda i, j: (i, j))],
        out_specs=pl.BlockSpec((bm, d), lambda i, j: (i, 0)),
        out_shape=jax.ShapeDtypeStruct((t, d), jnp.float32),
        compiler_params=pltpu.CompilerParams(
            dimension_semantics=("parallel", "arbitrary"), vmem_limit_bytes=VMEM_LIMIT_BYTES),
        name="peer_dense",
    )(h, ut, v, gmat)


def _peer(h, wq, keys, ut, v):
    hb = h.astype(jnp.bfloat16)
    i1, i2, g = _route(hb, wq, keys)
    return _peer_dense(hb, ut, v, _gate_matrix(i1, i2, g))


ROPE_HALF = 32
NA_QROWS = 8
NA_KROWS = 16
SWA_QB = 256
SWA_KB = SWA_QB + 2 * WINDOW

P_CK, P_CV, P_DK, P_DV, P_CQ, P_DQ = 0, 8, 16, 18, 20, 28
QK_CK, QK_DK, QK_CQ, QK_DQ = 0, 8, 10, 20


def _rope_tables(n):
    t = np.arange(n)
    freqs = (np.float32(ROPE_BASE) ** (-np.arange(ROPE_HALF, dtype=np.float32) / np.float32(ROPE_HALF))).astype(np.float32)
    ang_r = (t // GRID_W).astype(np.float32)[:, None] * freqs
    ang_c = (t % GRID_W).astype(np.float32)[:, None] * freqs
    cos = np.concatenate([np.cos(ang_r), np.cos(ang_r), np.cos(ang_c), np.cos(ang_c)], axis=-1)
    sin = np.concatenate([-np.sin(ang_r), np.sin(ang_r), -np.sin(ang_c), np.sin(ang_c)], axis=-1)
    return cos.astype(np.float32), sin.astype(np.float32)


def _qk_prep_kernel(tbl_ref, p_ref, g_ref, cos_ref, sin_ref, o_ref):
    del tbl_ref
    x = p_ref[...].astype(jnp.float32)
    y = x * lax.rsqrt(jnp.mean(x * x, axis=-1, keepdims=True) + EPS) * g_ref[0]
    lane = lax.broadcasted_iota(jnp.int32, y.shape, 1)
    first = jnp.bitwise_and(lane, 2 * ROPE_HALF - 1) < ROPE_HALF
    partner = jnp.where(first, pltpu.roll(y, LANES - ROPE_HALF, 1), pltpu.roll(y, ROPE_HALF, 1))
    o_ref[...] = (y * cos_ref[0] + partner * sin_ref[0]).astype(o_ref.dtype)


def _qk_prep(p, table, gains, cos, sin):
    n = p.shape[0]
    ncol = table.shape[1]
    bt = 512 if n % 512 == 0 else n
    grid_spec = pltpu.PrefetchScalarGridSpec(
        num_scalar_prefetch=1,
        grid=(n // bt, ncol),
        in_specs=[pl.BlockSpec((bt, HEAD_DIM), lambda i, j, tbl: (i, tbl[0, j])),
                  pl.BlockSpec((1, 1, HEAD_DIM), lambda i, j, tbl: (tbl[1, j], 0, 0)),
                  pl.BlockSpec((1, bt, HEAD_DIM), lambda i, j, tbl: (tbl[2, j], i, 0)),
                  pl.BlockSpec((1, bt, HEAD_DIM), lambda i, j, tbl: (tbl[2, j], i, 0))],
        out_specs=pl.BlockSpec((bt, HEAD_DIM), lambda i, j, tbl: (i, j)),
    )
    return pl.pallas_call(
        _qk_prep_kernel,
        grid_spec=grid_spec,
        out_shape=jax.ShapeDtypeStruct((n, ncol * HEAD_DIM), jnp.bfloat16),
        compiler_params=pltpu.CompilerParams(
            dimension_semantics=("parallel", "arbitrary"), vmem_limit_bytes=VMEM_LIMIT_BYTES),
        name="qk_prep",
    )(table, p, gains, cos, sin)


def _na_bias(rpb, n):
    rows = n // GRID_W
    out = []
    for r0, rlo in ((NA_QROWS, NA_QROWS - 4), (0, 0), (rows - NA_QROWS, rows - NA_KROWS)):
        r = r0 + np.arange(NA_QROWS)[:, None, None, None]
        c = np.arange(GRID_W)[None, :, None, None]
        kr = rlo + np.arange(NA_KROWS)[None, None, :, None]
        kc = np.arange(GRID_W)[None, None, None, :]
        rs = np.clip(r - WIN_H // 2, 0, rows - WIN_H)
        cs = np.clip(c - WIN_W // 2, 0, GRID_W - WIN_W)
        ok = (kr >= rs) & (kr < rs + WIN_H) & (kc >= cs) & (kc < cs + WIN_W)
        dr = np.clip(kr - r + WIN_H - 1, 0, 2 * WIN_H - 2)
        dc = np.clip(kc - c + WIN_W - 1, 0, 2 * WIN_W - 2)
        shape = (NA_QROWS * GRID_W, NA_KROWS * GRID_W)
        ok, dr, dc = (np.broadcast_to(a, (NA_QROWS, GRID_W, NA_KROWS, GRID_W)).reshape(shape) for a in (ok, dr, dc))
        out.append(jnp.where(ok[None], rpb[:, dr, dc].astype(jnp.float32), NEG_INF))
    return jnp.stack(out, axis=1)


def _softmax_pv(parts, extra_logit=None):
    m = parts[0][0].max(axis=-1, keepdims=True)
    for s, _ in parts[1:]:
        m = jnp.maximum(m, s.max(axis=-1, keepdims=True))
    if extra_logit is not None:
        m = jnp.maximum(m, extra_logit)
    den = jnp.exp(extra_logit - m) if extra_logit is not None else 0.0
    acc = None
    for s, v in parts:
        p = jnp.exp(s - m)
        den = den + p.sum(axis=-1, keepdims=True)
        pv = jnp.dot(p.astype(jnp.bfloat16), v, preferred_element_type=jnp.float32)
        acc = pv if acc is None else acc + pv
    return acc / den


def _na_kernel(q_ref, k_ref, v_ref, kc_ref, vc_ref, bias_ref, o_ref):
    n = q_ref.shape[0]
    qb = NA_QROWS * GRID_W
    kb = NA_KROWS * GRID_W
    nblk = n // qb
    nt = (((1,), (1,)), ((), ()))
    kc = kc_ref[...]
    vc = vc_ref[...]

    def body(b, carry):
        q0 = pl.multiple_of(b * qb, qb)
        ks = pl.multiple_of(jnp.clip(b * qb - (NA_KROWS - NA_QROWS) // 2 * GRID_W, 0, n - kb), 2 * LANES)
        kind = jnp.where(b == 0, 1, jnp.where(b == nblk - 1, 2, 0))
        q = q_ref[pl.ds(q0, qb), :]
        s_w = lax.dot_general(q, k_ref[pl.ds(ks, kb), :], nt, preferred_element_type=jnp.float32) + bias_ref[0, kind]
        s_c = lax.dot_general(q, kc, nt, preferred_element_type=jnp.float32)
        o = _softmax_pv([(s_w, v_ref[pl.ds(ks, kb), :]), (s_c, vc)])
        o_ref[pl.ds(q0, qb), :] = o.astype(o_ref.dtype)
        return carry

    lax.fori_loop(0, nblk, body, 0)


def _neighbourhood_attn(qk, p, qk_c, p_c, bias):
    n = qk.shape[0]
    m = qk_c.shape[0]

    def full(col0):
        return pl.BlockSpec((n, HEAD_DIM), lambda h: (0, col0 + h))

    def ctx(col0):
        return pl.BlockSpec((m, HEAD_DIM), lambda h: (0, col0 + h))

    return pl.pallas_call(
        _na_kernel,
        grid=(C_HEADS,),
        in_specs=[full(QK_CQ), full(QK_CK), full(P_CV), ctx(QK_CK), ctx(P_CV),
                  pl.BlockSpec((1, 3, NA_QROWS * GRID_W, NA_KROWS * GRID_W), lambda h: (h, 0, 0, 0))],
        out_specs=pl.BlockSpec((n, HEAD_DIM), lambda h: (0, h)),
        out_shape=jax.ShapeDtypeStruct((n, C_WIDTH), jnp.bfloat16),
        compiler_params=pltpu.CompilerParams(
            dimension_semantics=("parallel",), vmem_limit_bytes=VMEM_LIMIT_BYTES),
        name="neighbourhood_attn",
    )(qk, qk, p, qk_c, p_c, bias)


def _swa_kernel(q_ref, k_ref, v_ref, kc_ref, vc_ref, sink_ref, o_ref):
    n = q_ref.shape[0]
    nblk = n // SWA_QB
    nt = (((1,), (1,)), ((), ()))
    kc = kc_ref[...]
    vc = vc_ref[...]
    kv = pl.program_id(0)
    sink = jnp.concatenate(
        [jnp.broadcast_to(sink_ref[pl.ds(kv * D_GROUP + g, 1), 0:1], (SWA_QB, 1)) for g in range(D_GROUP)], axis=0)
    q_off = jnp.bitwise_and(lax.broadcasted_iota(jnp.int32, (D_GROUP * SWA_QB, SWA_KB), 0), SWA_QB - 1)
    k_off = lax.broadcasted_iota(jnp.int32, (D_GROUP * SWA_QB, SWA_KB), 1)
    rel = q_off - k_off

    def body(b, carry):
        q0 = pl.multiple_of(b * SWA_QB, SWA_QB)
        ks = pl.multiple_of(jnp.clip(q0 - WINDOW, 0, n - SWA_KB), LANES)
        q = jnp.concatenate([q_ref[pl.ds(q0, SWA_QB), g * HEAD_DIM:(g + 1) * HEAD_DIM] for g in range(D_GROUP)], axis=0)
        s_w = lax.dot_general(q, k_ref[pl.ds(ks, SWA_KB), :], nt, preferred_element_type=jnp.float32)
        s_w = jnp.where(jnp.abs(rel + (q0 - ks)) <= WINDOW, s_w, NEG_INF)
        s_c = lax.dot_general(q, kc, nt, preferred_element_type=jnp.float32)
        o = _softmax_pv([(s_w, v_ref[pl.ds(ks, SWA_KB), :]), (s_c, vc)], extra_logit=sink)
        for g in range(D_GROUP):
            o_ref[pl.ds(q0, SWA_QB), g * HEAD_DIM:(g + 1) * HEAD_DIM] = o[g * SWA_QB:(g + 1) * SWA_QB].astype(o_ref.dtype)
        return carry

    lax.fori_loop(0, nblk, body, 0)


def _window_attn(qk, p, qk_c, p_c, sink_rows):
    n = qk.shape[0]
    m = qk_c.shape[0]
    gw = D_GROUP * HEAD_DIM
    return pl.pallas_call(
        _swa_kernel,
        grid=(D_KV_HEADS,),
        in_specs=[pl.BlockSpec((n, gw), lambda kv: (0, QK_DQ // D_GROUP + kv)),
                  pl.BlockSpec((n, HEAD_DIM), lambda kv: (0, QK_DK + kv)),
                  pl.BlockSpec((n, HEAD_DIM), lambda kv: (0, P_DV + kv)),
                  pl.BlockSpec((m, HEAD_DIM), lambda kv: (0, QK_DK + kv)),
                  pl.BlockSpec((m, HEAD_DIM), lambda kv: (0, P_DV + kv)),
                  pl.BlockSpec((D_HEADS, LANES), lambda kv: (0, 0))],
        out_specs=pl.BlockSpec((n, gw), lambda kv: (0, kv)),
        out_shape=jax.ShapeDtypeStruct((n, D_WIDTH), jnp.bfloat16),
        compiler_params=pltpu.CompilerParams(
            dimension_semantics=("parallel",), vmem_limit_bytes=VMEM_LIMIT_BYTES),
        name="window_attn",
    )(qk, qk, p, qk_c, p_c, sink_rows)


def _mixer_cd(h_c, h_l, w_in, qk_g, rpb, sink, w_out):
    n = h_l.shape[0]
    m = h_c.shape[0]
    scale = HEAD_DIM ** -0.5
    p_l = _matmul(h_l, w_in, jnp.bfloat16)
    p_c = _matmul(h_c, w_in[:, :CD_KV_COLS], jnp.bfloat16)
    gains = jnp.stack([qk_g[0] * scale, qk_g[1], qk_g[2] * scale, qk_g[3]]).astype(jnp.float32)[:, None, :]
    cos, sin = _rope_tables(n)
    cs_l = (np.stack([np.ones_like(cos), cos]), np.stack([np.zeros_like(sin), sin]))
    cs_c = (np.ones((1, m, HEAD_DIM), np.float32), np.zeros((1, m, HEAD_DIM), np.float32))
    src = list(range(P_CK, P_CK + 8)) + [P_DK, P_DK + 1] + list(range(P_CQ, P_CQ + 8)) + [P_CQ, P_CQ] + list(range(P_DQ, P_DQ + 8))
    gain = [1] * 8 + [3] * 2 + [0] * 8 + [0, 0] + [2] * 8
    rope = [0] * 8 + [1] * 2 + [0] * 8 + [0, 0] + [1] * 8
    tbl_l = jnp.asarray(np.array([src, gain, rope], np.int32))
    tbl_c = jnp.asarray(np.array([src[:10], gain[:10], [0] * 10], np.int32))
    qk_l = _qk_prep(p_l, tbl_l, gains, jnp.asarray(cs_l[0]), jnp.asarray(cs_l[1]))
    qk_c = _qk_prep(p_c, tbl_c, gains, jnp.asarray(cs_c[0]), jnp.asarray(cs_c[1]))
    c_out = _neighbourhood_attn(qk_l, p_l, qk_c, p_c, _na_bias(rpb, n))
    sink_rows = jnp.broadcast_to(sink.astype(jnp.float32)[:, None], (D_HEADS, LANES))
    d_out = _window_attn(qk_l, p_l, qk_c, p_c, sink_rows)
    return _matmul(jnp.concatenate([c_out, d_out], axis=-1), w_out)


def _split_cols(p, sizes):
    return jnp.split(p, np.cumsum(sizes)[:-1].tolist(), axis=-1)


def _rmsnorm(x, g):
    xf = x.astype(jnp.float32)
    y = xf * lax.rsqrt(jnp.mean(xf * xf, axis=-1, keepdims=True) + EPS)
    return (y * g.astype(jnp.float32)).astype(x.dtype)


def _modulate(x, g, shift, scale):
    return _rmsnorm(x, g) * (1 + scale) + shift


def _heads(t, hd):
    return t.reshape(t.shape[0], t.shape[1], -1, hd)


def _short_conv(x, w, b):
    pad = CONV_W // 2
    n = x.shape[1]
    xp = jnp.pad(x, ((0, 0), (pad, pad), (0, 0)))
    return sum(w[j] * xp[:, j:j + n] for j in range(CONV_W)) + b


def _to_chunks(t):
    B_, H_, n = t.shape[:3]
    return jnp.moveaxis(t.reshape(B_, H_, n // SCAN_CHUNK, SCAN_CHUNK, *t.shape[3:]), 2, 0)


def _from_chunks(t):
    nc, B_, H_, L = t.shape[:4]
    return jnp.moveaxis(t, 0, 2).reshape(B_, H_, nc * L, *t.shape[4:])


def _mlstm_scan(q, k, v, ig, lf, state):
    q, k, v, ig, lf = (t.astype(jnp.float32) for t in (q, k, v, ig, lf))
    mask = jnp.tril(jnp.ones((SCAN_CHUNK, SCAN_CHUNK), bool))

    def step(carry, inp):
        C, nv, m = carry
        qc, kc, vc, ic, fc = inp
        b = jnp.cumsum(fc, axis=-1)
        dlog = jnp.where(mask, b[..., :, None] - b[..., None, :] + ic[..., None, :], -jnp.inf)
        m_inter = b + m[..., None]
        m_t = jnp.maximum(m_inter, jnp.max(dlog, axis=-1))
        sc = jnp.einsum('bhtd,bhsd->bhts', qc, kc) * jnp.exp(dlog - m_t[..., None])
        w_inter = jnp.exp(m_inter - m_t)
        num = jnp.einsum('bhts,bhse->bhte', sc, vc) + w_inter[..., None] * jnp.einsum('bhtd,bhde->bhte', qc, C)
        den = jnp.sum(sc, axis=-1) + w_inter * jnp.einsum('bhtd,bhd->bht', qc, nv)
        h = num / jnp.maximum(jnp.abs(den), jnp.exp(-m_t))[..., None]
        m_new = m_t[..., -1]
        w_end = jnp.exp(b[..., -1:] - b + ic - m_new[..., None])
        decay = jnp.exp(b[..., -1] + m - m_new)
        C = decay[..., None, None] * C + jnp.einsum('bhs,bhsd,bhse->bhde', w_end, kc, vc)
        nv = decay[..., None] * nv + jnp.einsum('bhs,bhsd->bhd', w_end, kc)
        return (C, nv, m_new), h

    state, h = lax.scan(step, state, tuple(_to_chunks(t) for t in (q, k, v, ig, lf)))
    return _from_chunks(h), state


def _gla_scan(q, k, v, lg, S0):
    q, k, v, lg = (t.astype(jnp.float32) for t in (q, k, v, lg))
    mask = jnp.tril(jnp.ones((SCAN_CHUNK, SCAN_CHUNK), bool))[:, :, None]

    def step(S, inp):
        qc, kc, vc, gc = inp
        b = jnp.cumsum(gc, axis=2)
        rel = jnp.where(mask, b[:, :, :, None, :] - b[:, :, None, :, :], -jnp.inf)
        attn = jnp.einsum('bhtd,bhtsd,bhsd->bhts', qc, jnp.exp(rel), kc)
        o = jnp.einsum('bhts,bhse->bhte', attn, vc) + jnp.einsum('bhtd,bhde->bhte', qc * jnp.exp(b), S)
        b_end = b[:, :, -1:, :]
        S = jnp.exp(b_end[:, :, 0])[..., None] * S + jnp.einsum('bhsd,bhse->bhde', kc * jnp.exp(b_end - b), vc)
        return S, o

    S, o = lax.scan(step, S0, tuple(_to_chunks(t) for t in (q, k, v, lg)))
    return _from_chunks(o), S


def _bidirectional(scan_fn, init, ctx_seq, ctx_gates, lat_seq, lat_gates):
    y_c, y_l = None, None
    for z in range(2):
        fc = tuple(ctx_seq) + tuple(g[z] for g in ctx_gates)
        fl = tuple(lat_seq) + tuple(g[z] for g in lat_gates)
        if z == 1:
            fc = tuple(jnp.flip(t, axis=2) for t in fc)
            fl = tuple(jnp.flip(t, axis=2) for t in fl)
        oc, st = scan_fn(*fc, init)
        ol, _ = scan_fn(*fl, st)
        if z == 1:
            oc, ol = jnp.flip(oc, axis=2), jnp.flip(ol, axis=2)
        y_c = oc if y_c is None else y_c + oc
        y_l = ol if y_l is None else y_l + ol
    return y_c, y_l


def _head_out(y, g, gate):
    B_, H_, n, hd = y.shape
    yn = _rmsnorm(y.transpose(0, 2, 1, 3), g)
    return (yn * gate.reshape(B_, n, H_, hd)).reshape(B_, n, H_ * hd)


def _ab_features(h, w_in, conv_w, conv_b, gate_b, alpha_w2, alpha_b):
    B_, n, _ = h.shape
    p = _matmul(h.reshape(B_ * n, -1), w_in).reshape(B_, n, -1)
    a_q, a_k, a_v, a_o, a_gate, b_q, b_k, b_v, b_r, b_lr = _split_cols(p, AB_SPLITS)
    a_qk = jax.nn.silu(_short_conv(jnp.concatenate([a_q, a_k], axis=-1), conv_w, conv_b))
    a_q, a_k = jnp.split(a_qk, 2, axis=-1)
    bhnd = lambda t, hd: _heads(t, hd).transpose(0, 2, 1, 3)
    g = (a_gate.reshape(B_, n, 2, 2, A_HEADS).astype(jnp.float32) + gate_b.astype(jnp.float32)).transpose(2, 3, 0, 4, 1)
    ig, lf = g[:, 0], jax.nn.log_sigmoid(g[:, 1])
    za = jnp.einsum('bnzr,zrc->zbnc', b_lr.reshape(B_, n, 2, GLA_RANK), alpha_w2,
                    precision=lax.Precision.HIGHEST) + alpha_b[:, None, None, :]
    lg = (jax.nn.log_sigmoid(za.astype(jnp.float32)) / GLA_TAU).reshape(2, B_, n, B_HEADS, B_KEY_DIM).transpose(0, 1, 3, 2, 4)
    a_seq = (bhnd(a_q, HEAD_DIM), bhnd(a_k, HEAD_DIM) * HEAD_DIM ** -0.5, bhnd(a_v, HEAD_DIM))
    b_seq = (bhnd(b_q, B_KEY_DIM) * B_KEY_DIM ** -0.5, bhnd(b_k, B_KEY_DIM), bhnd(b_v, HEAD_DIM))
    return a_seq, (ig, lf), a_o, b_seq, (lg,), b_r


def _mixer_ab(h_c, h_l, w_in, conv_w, conv_b, gate_b, alpha_w2, alpha_b, head_g, w_out, need_ctx):
    fc = _ab_features(h_c, w_in, conv_w, conv_b, gate_b, alpha_w2, alpha_b)
    fl = _ab_features(h_l, w_in, conv_w, conv_b, gate_b, alpha_w2, alpha_b)
    B_ = h_l.shape[0]
    a_init = (jnp.zeros((B_, A_HEADS, HEAD_DIM, HEAD_DIM), jnp.float32),
              jnp.zeros((B_, A_HEADS, HEAD_DIM), jnp.float32),
              jnp.zeros((B_, A_HEADS), jnp.float32))
    b_init = jnp.zeros((B_, B_HEADS, B_KEY_DIM, HEAD_DIM), jnp.float32)
    a_c, a_l = _bidirectional(_mlstm_scan, a_init, fc[0], fc[1], fl[0], fl[1])
    b_c, b_l = _bidirectional(_gla_scan, b_init, fc[3], fc[4], fl[3], fl[4])

    def merge(a, b, a_o, b_r):
        ya = _head_out(a, head_g[0], jax.nn.sigmoid(a_o))
        yb = _head_out(b, head_g[1], jax.nn.silu(b_r))
        y = jnp.concatenate([ya, yb], axis=-1)
        return _matmul(y.reshape(-1, y.shape[-1]), w_out).reshape(y.shape[0], y.shape[1], -1)

    y_l = merge(a_l, b_l, fl[2], fl[5])
    y_c = merge(a_c, b_c, fc[2], fc[5]) if need_ctx else None
    return y_c, y_l


def kernel(x, c, ctx, c_ctx, mod_w, mod_b, norm_g, ab_w_in, ab_conv_w, ab_conv_b, ab_gate_b, ab_alpha_w2,
           ab_alpha_b, ab_head_g, ab_w_out, cd_w_in, cd_qk_g, cd_rpb, cd_sink, cd_w_out, peer_w_q,
           peer_sub_keys, peer_u, peer_v):
    bf = jnp.bfloat16
    xl, xc = x, ctx
    for l in range(DEPTH):
        need_ctx = l < DEPTH - 1
        j = l // 2
        mod_l = jnp.split((jax.nn.silu(c) @ mod_w[l] + mod_b[l])[:, None, :], 6, axis=-1)
        mod_c = jnp.split((jax.nn.silu(c_ctx) @ mod_w[l] + mod_b[l])[None, None, :], 6, axis=-1)
        hl = _modulate(xl, norm_g[l, 0], mod_l[0], mod_l[1])
        hc = _modulate(xc, norm_g[l, 0], mod_c[0], mod_c[1])
        if l % 2 == 0:
            yc, yl = _mixer_ab(hc, hl, ab_w_in[j], ab_conv_w[j], ab_conv_b[j], ab_gate_b[j], ab_alpha_w2[j],
                               ab_alpha_b[j], ab_head_g[j], ab_w_out[j], need_ctx)
        else:
            yc = None
            yl = _mixer_cd(hc[0], hl[0], cd_w_in[j].astype(bf), cd_qk_g[j], cd_rpb[j], cd_sink[j],
                           cd_w_out[j].astype(bf))[None]
        wq = peer_w_q[l].astype(bf)
        keys = peer_sub_keys[l].astype(bf)
        ut = peer_u[l].T.astype(bf)
        vt = peer_v[l].astype(bf)
        xl = xl + mod_l[2] * yl
        hl = _modulate(xl, norm_g[l, 1], mod_l[3], mod_l[4])
        xl = xl + mod_l[5] * _peer(hl[0], wq, keys, ut, vt)[None]
        if need_ctx:
            xc = xc + mod_c[2] * yc
            hc = _modulate(xc, norm_g[l, 1], mod_c[3], mod_c[4])
            xc = xc + mod_c[5] * _peer(hc[0], wq, keys, ut, vt)[None]
    return xl
```

```python
import functools

import numpy as np
import jax
import jax.numpy as jnp
from jax import lax
from jax.experimental import pallas as pl
from jax.experimental.pallas import tpu as pltpu

D_MODEL = 2048
SEQ = 8192
DEPTH = 2
GRID_W = 64
CTX_LEN = 256
HEAD_DIM = 128
N_GROUP_HEADS = 8
EPS = 1e-6
A_HEADS = 8
A_WIDTH = 1024
CONV_W = 3
B_HEADS = 8
B_KEY_DIM = 64
B_KEY_WIDTH = 512
B_VAL_WIDTH = 1024
GLA_RANK = 16
GLA_TAU = 16.0
SCAN_CHUNK = 64
C_HEADS = 8
C_WIDTH = 1024
WIN_H = 8
WIN_W = 16
D_HEADS = 8
D_KV_HEADS = 2
D_GROUP = 4
D_WIDTH = 1024
D_KV_WIDTH = 256
WINDOW = 128
ROPE_BASE = 10000.0
PEER_HEADS = 8
PEER_NKEYS = 128
PEER_EXPERTS = PEER_NKEYS * PEER_NKEYS
PEER_QDIM = 256
PEER_TOPK = 16
PEER_SLOTS = PEER_HEADS * PEER_TOPK
AB_SPLITS = (A_WIDTH, A_WIDTH, A_WIDTH, A_WIDTH, 4 * A_HEADS,
             B_KEY_WIDTH, B_KEY_WIDTH, B_VAL_WIDTH, B_VAL_WIDTH, 2 * GLA_RANK)
CD_SPLITS = (C_WIDTH, C_WIDTH, D_KV_WIDTH, D_KV_WIDTH, C_WIDTH, D_WIDTH)
CD_KV_COLS = sum(CD_SPLITS[:4])

VMEM_LIMIT_BYTES = 56 * 1024 * 1024
LANES = 128


def _row_tile(m):
    for t in (1024, 768, 512, 256, 128):
        if m % t == 0:
            return t
    raise ValueError(f"unsupported row count {m}")


def _mm_kernel(x_ref, w_ref, o_ref):
    o_ref[...] = jnp.dot(x_ref[...], w_ref[...], preferred_element_type=jnp.float32).astype(o_ref.dtype)


def _matmul(x, w, out_dtype=jnp.float32):
    m, k = x.shape
    n = w.shape[1]
    bn = min(512, n)
    n_pad = -(-n // bn) * bn
    if n_pad != n:
        w = jnp.pad(w, ((0, 0), (0, n_pad - n)))
    bm = _row_tile(m)
    out = pl.pallas_call(
        _mm_kernel,
        grid=(m // bm, n_pad // bn),
        in_specs=[pl.BlockSpec((bm, k), lambda i, j: (i, 0)),
                  pl.BlockSpec((k, bn), lambda i, j: (0, j))],
        out_specs=pl.BlockSpec((bm, bn), lambda i, j: (i, j)),
        out_shape=jax.ShapeDtypeStruct((m, n_pad), out_dtype),
        compiler_params=pltpu.CompilerParams(
            dimension_semantics=("parallel", "arbitrary"), vmem_limit_bytes=VMEM_LIMIT_BYTES),
        name="matmul",
    )(x.astype(jnp.bfloat16), w.astype(jnp.bfloat16))
    return out[:, :n] if n_pad != n else out


ROUTE_TOKENS = 256
GATE_TOKENS = 128
NEG_INF = float("-inf")


def _top16(s, key, big):
    slot = lax.broadcasted_iota(jnp.int32, (PEER_TOPK, s.shape[1]), 0)
    vals = jnp.zeros((PEER_TOPK, s.shape[1]), jnp.float32)
    keys = jnp.zeros((PEER_TOPK, s.shape[1]), jnp.int32)
    for it in range(PEER_TOPK):
        m = jnp.max(s, axis=0, keepdims=True)
        k = jnp.min(jnp.where(s == m, key, big), axis=0, keepdims=True)
        vals = jnp.where(slot == it, m, vals)
        keys = jnp.where(slot == it, k, keys)
        s = jnp.where(key == k, NEG_INF, s)
    return vals, keys


def _route_kernel(h_ref, wq_ref, k_ref, i1_ref, i2_ref, g_ref):
    bt = h_ref.shape[0]
    q = jnp.dot(h_ref[...], wq_ref[...], preferred_element_type=jnp.float32).astype(jnp.bfloat16)
    nt = (((1,), (1,)), ((), ()))
    s1 = lax.dot_general(k_ref[0, 0], q[:, :PEER_NKEYS], nt, preferred_element_type=jnp.float32)
    s2 = lax.dot_general(k_ref[0, 1], q[:, PEER_NKEYS:], nt, preferred_element_type=jnp.float32)
    row = lax.broadcasted_iota(jnp.int32, (PEER_NKEYS, bt), 0)
    v1, r1 = _top16(s1, row, PEER_NKEYS)
    v2, r2 = _top16(s2, row, PEER_NKEYS)
    i16 = lax.broadcasted_iota(jnp.int32, (16, bt), 0)
    i8 = lax.broadcasted_iota(jnp.int32, (8, bt), 0)
    cand = [v1[0:1] + v2]
    flat = [i16]
    for a in range(1, 8):
        cand.append(v1[a:a + 1] + v2[0:8])
        flat.append(i8 + a * PEER_TOPK)
    cand.append(v1[8:16] + v2[0:1])
    flat.append((i8 + 8) * PEER_TOPK)
    best, key = _top16(jnp.concatenate(cand, axis=0), jnp.concatenate(flat, axis=0), PEER_TOPK * PEER_TOPK)
    a_sel = jnp.right_shift(key, 4)
    b_sel = jnp.bitwise_and(key, PEER_TOPK - 1)
    e1 = jnp.zeros_like(key)
    e2 = jnp.zeros_like(key)
    for r in range(PEER_TOPK):
        e1 = jnp.where(a_sel == r, r1[r:r + 1], e1)
        e2 = jnp.where(b_sel == r, r2[r:r + 1], e2)
    ex = jnp.exp(best - best[0:1])
    i1_ref[...] = e1
    i2_ref[...] = e2
    g_ref[...] = ex / jnp.sum(ex, axis=0, keepdims=True)


def _route(hb, wq, keys):
    t, d = hb.shape
    bt = ROUTE_TOKENS
    slot_spec = pl.BlockSpec((PEER_TOPK, bt), lambda i, h: (h, i))
    return pl.pallas_call(
        _route_kernel,
        grid=(t // bt, PEER_HEADS),
        in_specs=[pl.BlockSpec((bt, d), lambda i, h: (i, 0)),
                  pl.BlockSpec((d, PEER_QDIM), lambda i, h: (0, h)),
                  pl.BlockSpec((1, 2, PEER_NKEYS, PEER_QDIM // 2), lambda i, h: (h, 0, 0, 0))],
        out_specs=[slot_spec, slot_spec, slot_spec],
        out_shape=[jax.ShapeDtypeStruct((PEER_SLOTS, t), jnp.int32),
                   jax.ShapeDtypeStruct((PEER_SLOTS, t), jnp.int32),
                   jax.ShapeDtypeStruct((PEER_SLOTS, t), jnp.float32)],
        compiler_params=pltpu.CompilerParams(
            dimension_semantics=("parallel", "arbitrary"), vmem_limit_bytes=VMEM_LIMIT_BYTES),
        name="peer_route",
    )(hb, wq, keys)


def _gate_matrix_kernel(i1_ref, i2_ref, g_ref, o_ref, i1_t, i2_t, g_t):
    tb = o_ref.shape[0]
    i1_t[...] = i1_ref[...].T
    i2_t[...] = i2_ref[...].T
    g_t[...] = g_ref[...].T
    row_id = lax.broadcasted_iota(jnp.int32, (PEER_NKEYS, PEER_SLOTS), 0)
    zero = jnp.zeros((PEER_NKEYS, PEER_SLOTS), jnp.bfloat16)

    def body(tp, carry):
        ta = 2 * tp
        tb_ = ta + 1
        ca = jnp.where(row_id == i1_t[pl.ds(ta, 1), :], g_t[pl.ds(ta, 1), :], 0.0).astype(jnp.bfloat16)
        cb = jnp.where(row_id == i1_t[pl.ds(tb_, 1), :], g_t[pl.ds(tb_, 1), :], 0.0).astype(jnp.bfloat16)
        oa = jnp.where(row_id == i2_t[pl.ds(ta, 1), :], 1.0, 0.0).astype(jnp.bfloat16)
        ob = jnp.where(row_id == i2_t[pl.ds(tb_, 1), :], 1.0, 0.0).astype(jnp.bfloat16)
        lhs = jnp.concatenate([ca, cb], axis=1)
        rhs = jnp.concatenate([jnp.concatenate([oa, zero], axis=1),
                               jnp.concatenate([zero, ob], axis=1)], axis=0)
        out = lax.dot_general(lhs, rhs, (((1,), (1,)), ((), ())), preferred_element_type=jnp.float32)
        o_ref[ta] = out[:, :PEER_NKEYS].astype(o_ref.dtype)
        o_ref[tb_] = out[:, PEER_NKEYS:].astype(o_ref.dtype)
        return carry

    lax.fori_loop(0, tb // 2, body, 0, unroll=4)


def _gate_matrix(i1, i2, g):
    t = i1.shape[1]
    tb = GATE_TOKENS
    out = pl.pallas_call(
        _gate_matrix_kernel,
        grid=(t // tb,),
        in_specs=[pl.BlockSpec((PEER_SLOTS, tb), lambda i: (0, i))] * 3,
        out_specs=pl.BlockSpec((tb, PEER_NKEYS, PEER_NKEYS), lambda i: (i, 0, 0)),
        out_shape=jax.ShapeDtypeStruct((t, PEER_NKEYS, PEER_NKEYS), jnp.bfloat16),
        scratch_shapes=[pltpu.VMEM((tb, PEER_SLOTS), jnp.int32),
                        pltpu.VMEM((tb, PEER_SLOTS), jnp.int32),
                        pltpu.VMEM((tb, PEER_SLOTS), jnp.float32)],
        compiler_params=pltpu.CompilerParams(
            dimension_semantics=("parallel",), vmem_limit_bytes=VMEM_LIMIT_BYTES),
        name="peer_gate_matrix",
    )(i1, i2, g)
    return out.reshape(t, PEER_EXPERTS)


def _gelu_tanh(x):
    return 0.5 * x * (1.0 + jnp.tanh(0.7978845608028654 * (x + 0.044715 * x * x * x)))


def _peer_dense_kernel(h_ref, ut_ref, v_ref, g_ref, o_ref):
    j = pl.program_id(1)
    act = _gelu_tanh(jnp.dot(h_ref[...], ut_ref[...], preferred_element_type=jnp.float32))
    w = (act * g_ref[...].astype(jnp.float32)).astype(jnp.bfloat16)
    contrib = jnp.dot(w, v_ref[...], preferred_element_type=jnp.float32)

    @pl.when(j == 0)
    def _():
        o_ref[...] = contrib

    @pl.when(j != 0)
    def _():
        o_ref[...] += contrib


def _peer_dense(h, ut, v, gmat):
    t, d = h.shape
    e = v.shape[0]
    bm = _row_tile(t)
    be = 512
    return pl.pallas_call(
        _peer_dense_kernel,
        grid=(t // bm, e // be),
        in_specs=[pl.BlockSpec((bm, d), lambda i, j: (i, 0)),
                  pl.BlockSpec((d, be), lambda i, j: (0, j)),
                  pl.BlockSpec((be, d), lambda i, j: (j, 0)),
                  pl.BlockSpec((bm, be), lambda i, j: (i, j))],
        out_specs=pl.BlockSpec((bm, d), lambda i, j: (i, 0)),
        out_shape=jax.ShapeDtypeStruct((t, d), jnp.float32),
        compiler_params=pltpu.CompilerParams(
            dimension_semantics=("parallel", "arbitrary"), vmem_limit_bytes=VMEM_LIMIT_BYTES),
        name="peer_dense",
    )(h, ut, v, gmat)


def _peer(h, wq, keys, ut, v):
    hb = h.astype(jnp.bfloat16)
    i1, i2, g = _route(hb, wq, keys)
    return _peer_dense(hb, ut, v, _gate_matrix(i1, i2, g))


ROPE_HALF = 32
NA_QROWS = 8
NA_KROWS = 16
SWA_QB = 256
SWA_KB = SWA_QB + 2 * WINDOW

P_CK, P_CV, P_DK, P_DV, P_CQ, P_DQ = 0, 8, 16, 18, 20, 28
QK_CK, QK_DK, QK_CQ, QK_DQ = 0, 8, 10, 20


def _rope_tables(n):
    t = np.arange(n)
    freqs = (np.float32(ROPE_BASE) ** (-np.arange(ROPE_HALF, dtype=np.float32) / np.float32(ROPE_HALF))).astype(np.float32)
    ang_r = (t // GRID_W).astype(np.float32)[:, None] * freqs
    ang_c = (t % GRID_W).astype(np.float32)[:, None] * freqs
    cos = np.concatenate([np.cos(ang_r), np.cos(ang_r), np.cos(ang_c), np.cos(ang_c)], axis=-1)
    sin = np.concatenate([-np.sin(ang_r), np.sin(ang_r), -np.sin(ang_c), np.sin(ang_c)], axis=-1)
    return cos.astype(np.float32), sin.astype(np.float32)


def _qk_prep_kernel(tbl_ref, p_ref, g_ref, cos_ref, sin_ref, o_ref):
    del tbl_ref
    x = p_ref[...].astype(jnp.float32)
    y = x * lax.rsqrt(jnp.mean(x * x, axis=-1, keepdims=True) + EPS) * g_ref[0]
    lane = lax.broadcasted_iota(jnp.int32, y.shape, 1)
    first = jnp.bitwise_and(lane, 2 * ROPE_HALF - 1) < ROPE_HALF
    partner = jnp.where(first, pltpu.roll(y, LANES - ROPE_HALF, 1), pltpu.roll(y, ROPE_HALF, 1))
    o_ref[...] = (y * cos_ref[0] + partner * sin_ref[0]).astype(o_ref.dtype)


def _qk_prep(p, table, gains, cos, sin):
    n = p.shape[0]
    ncol = table.shape[1]
    bt = 512 if n % 512 == 0 else n
    grid_spec = pltpu.PrefetchScalarGridSpec(
        num_scalar_prefetch=1,
        grid=(n // bt, ncol),
        in_specs=[pl.BlockSpec((bt, HEAD_DIM), lambda i, j, tbl: (i, tbl[0, j])),
                  pl.BlockSpec((1, 1, HEAD_DIM), lambda i, j, tbl: (tbl[1, j], 0, 0)),
                  pl.BlockSpec((1, bt, HEAD_DIM), lambda i, j, tbl: (tbl[2, j], i, 0)),
                  pl.BlockSpec((1, bt, HEAD_DIM), lambda i, j, tbl: (tbl[2, j], i, 0))],
        out_specs=pl.BlockSpec((bt, HEAD_DIM), lambda i, j, tbl: (i, j)),
    )
    return pl.pallas_call(
        _qk_prep_kernel,
        grid_spec=grid_spec,
        out_shape=jax.ShapeDtypeStruct((n, ncol * HEAD_DIM), jnp.bfloat16),
        compiler_params=pltpu.CompilerParams(
            dimension_semantics=("parallel", "arbitrary"), vmem_limit_bytes=VMEM_LIMIT_BYTES),
        name="qk_prep",
    )(table, p, gains, cos, sin)


def _na_bias(rpb, n):
    rows = n // GRID_W
    nr, nc = 2 * WIN_H - 1, 2 * WIN_W - 1
    c = np.arange(GRID_W)[:, None]
    kc = np.arange(GRID_W)[None, :]
    cs = np.clip(c - WIN_W // 2, 0, GRID_W - WIN_W)
    col_ok = (kc >= cs) & (kc < cs + WIN_W)
    oh_col = (np.clip(kc - c + WIN_W - 1, 0, nc - 1)[None] == np.arange(nc)[:, None, None]).astype(np.float32)
    oh_row, ok = [], []
    for r0, rlo in ((NA_QROWS, NA_QROWS - 4), (0, 0), (rows - NA_QROWS, rows - NA_KROWS)):
        r = r0 + np.arange(NA_QROWS)[:, None]
        kr = rlo + np.arange(NA_KROWS)[None, :]
        rs = np.clip(r - WIN_H // 2, 0, rows - WIN_H)
        row_ok = (kr >= rs) & (kr < rs + WIN_H)
        oh_row.append((np.clip(kr - r + WIN_H - 1, 0, nr - 1)[..., None] == np.arange(nr)).astype(np.float32))
        ok.append((row_ok[:, None, :, None] & col_ok[None, :, None, :]).reshape(NA_QROWS * GRID_W, NA_KROWS * GRID_W))
    hp = lax.Precision.HIGHEST
    colb = jnp.einsum('hab,bcd->hacd', rpb.astype(jnp.float32), jnp.asarray(oh_col), precision=hp)
    bias = jnp.einsum('tqka,hacd->htqckd', jnp.asarray(np.stack(oh_row)), colb, precision=hp)
    bias = bias.reshape(rpb.shape[0], 3, NA_QROWS * GRID_W, NA_KROWS * GRID_W)
    return jnp.where(jnp.asarray(np.stack(ok))[None], bias, NEG_INF)


def _softmax_pv(parts, extra_logit=None):
    m = parts[0][0].max(axis=-1, keepdims=True)
    for s, _ in parts[1:]:
        m = jnp.maximum(m, s.max(axis=-1, keepdims=True))
    if extra_logit is not None:
        m = jnp.maximum(m, extra_logit)
    den = jnp.exp(extra_logit - m) if extra_logit is not None else 0.0
    acc = None
    for s, v in parts:
        p = jnp.exp(s - m)
        den = den + p.sum(axis=-1, keepdims=True)
        pv = jnp.dot(p.astype(jnp.bfloat16), v, preferred_element_type=jnp.float32)
        acc = pv if acc is None else acc + pv
    return acc / den


def _na_kernel(q_ref, k_ref, v_ref, kc_ref, vc_ref, bias_ref, o_ref):
    n = q_ref.shape[0]
    qb = NA_QROWS * GRID_W
    kb = NA_KROWS * GRID_W
    nblk = n // qb
    nt = (((1,), (1,)), ((), ()))
    kc = kc_ref[...]
    vc = vc_ref[...]

    def body(b, carry):
        q0 = pl.multiple_of(b * qb, qb)
        ks = pl.multiple_of(jnp.clip(b * qb - (NA_KROWS - NA_QROWS) // 2 * GRID_W, 0, n - kb), 2 * LANES)
        kind = jnp.where(b == 0, 1, jnp.where(b == nblk - 1, 2, 0))
        q = q_ref[pl.ds(q0, qb), :]
        s_w = lax.dot_general(q, k_ref[pl.ds(ks, kb), :], nt, preferred_element_type=jnp.float32) + bias_ref[0, kind]
        s_c = lax.dot_general(q, kc, nt, preferred_element_type=jnp.float32)
        o = _softmax_pv([(s_w, v_ref[pl.ds(ks, kb), :]), (s_c, vc)])
        o_ref[pl.ds(q0, qb), :] = o.astype(o_ref.dtype)
        return carry

    lax.fori_loop(0, nblk, body, 0)


def _neighbourhood_attn(qk, p, qk_c, p_c, bias):
    n = qk.shape[0]
    m = qk_c.shape[0]

    def full(col0):
        return pl.BlockSpec((n, HEAD_DIM), lambda h: (0, col0 + h))

    def ctx(col0):
        return pl.BlockSpec((m, HEAD_DIM), lambda h: (0, col0 + h))

    return pl.pallas_call(
        _na_kernel,
        grid=(C_HEADS,),
        in_specs=[full(QK_CQ), full(QK_CK), full(P_CV), ctx(QK_CK), ctx(P_CV),
                  pl.BlockSpec((1, 3, NA_QROWS * GRID_W, NA_KROWS * GRID_W), lambda h: (h, 0, 0, 0))],
        out_specs=pl.BlockSpec((n, HEAD_DIM), lambda h: (0, h)),
        out_shape=jax.ShapeDtypeStruct((n, C_WIDTH), jnp.bfloat16),
        compiler_params=pltpu.CompilerParams(
            dimension_semantics=("parallel",), vmem_limit_bytes=VMEM_LIMIT_BYTES),
        name="neighbourhood_attn",
    )(qk, qk, p, qk_c, p_c, bias)


def _swa_kernel(q_ref, k_ref, v_ref, kc_ref, vc_ref, sink_ref, o_ref):
    n = q_ref.shape[0]
    nblk = n // SWA_QB
    nt = (((1,), (1,)), ((), ()))
    kc = kc_ref[...]
    vc = vc_ref[...]
    kv = pl.program_id(0)
    sink = jnp.concatenate(
        [jnp.broadcast_to(sink_ref[pl.ds(kv * D_GROUP + g, 1), 0:1], (SWA_QB, 1)) for g in range(D_GROUP)], axis=0)
    q_off = jnp.bitwise_and(lax.broadcasted_iota(jnp.int32, (D_GROUP * SWA_QB, SWA_KB), 0), SWA_QB - 1)
    k_off = lax.broadcasted_iota(jnp.int32, (D_GROUP * SWA_QB, SWA_KB), 1)
    rel = q_off - k_off

    def body(b, carry):
        q0 = pl.multiple_of(b * SWA_QB, SWA_QB)
        ks = pl.multiple_of(jnp.clip(q0 - WINDOW, 0, n - SWA_KB), LANES)
        q = jnp.concatenate([q_ref[pl.ds(q0, SWA_QB), g * HEAD_DIM:(g + 1) * HEAD_DIM] for g in range(D_GROUP)], axis=0)
        s_w = lax.dot_general(q, k_ref[pl.ds(ks, SWA_KB), :], nt, preferred_element_type=jnp.float32)
        s_w = jnp.where(jnp.abs(rel + (q0 - ks)) <= WINDOW, s_w, NEG_INF)
        s_c = lax.dot_general(q, kc, nt, preferred_element_type=jnp.float32)
        o = _softmax_pv([(s_w, v_ref[pl.ds(ks, SWA_KB), :]), (s_c, vc)], extra_logit=sink)
        for g in range(D_GROUP):
            o_ref[pl.ds(q0, SWA_QB), g * HEAD_DIM:(g + 1) * HEAD_DIM] = o[g * SWA_QB:(g + 1) * SWA_QB].astype(o_ref.dtype)
        return carry

    lax.fori_loop(0, nblk, body, 0)


def _window_attn(qk, p, qk_c, p_c, sink_rows):
    n = qk.shape[0]
    m = qk_c.shape[0]
    gw = D_GROUP * HEAD_DIM
    return pl.pallas_call(
        _swa_kernel,
        grid=(D_KV_HEADS,),
        in_specs=[pl.BlockSpec((n, gw), lambda kv: (0, QK_DQ // D_GROUP + kv)),
                  pl.BlockSpec((n, HEAD_DIM), lambda kv: (0, QK_DK + kv)),
                  pl.BlockSpec((n, HEAD_DIM), lambda kv: (0, P_DV + kv)),
                  pl.BlockSpec((m, HEAD_DIM), lambda kv: (0, QK_DK + kv)),
                  pl.BlockSpec((m, HEAD_DIM), lambda kv: (0, P_DV + kv)),
                  pl.BlockSpec((D_HEADS, LANES), lambda kv: (0, 0))],
        out_specs=pl.BlockSpec((n, gw), lambda kv: (0, kv)),
        out_shape=jax.ShapeDtypeStruct((n, D_WIDTH), jnp.bfloat16),
        compiler_params=pltpu.CompilerParams(
            dimension_semantics=("parallel",), vmem_limit_bytes=VMEM_LIMIT_BYTES),
        name="window_attn",
    )(qk, qk, p, qk_c, p_c, sink_rows)


def _mixer_cd(h_c, h_l, w_in, qk_g, rpb, sink, w_out):
    n = h_l.shape[0]
    m = h_c.shape[0]
    scale = HEAD_DIM ** -0.5
    p_l = _matmul(h_l, w_in, jnp.bfloat16)
    p_c = _matmul(h_c, w_in[:, :CD_KV_COLS], jnp.bfloat16)
    gains = jnp.stack([qk_g[0] * scale, qk_g[1], qk_g[2] * scale, qk_g[3]]).astype(jnp.float32)[:, None, :]
    cos, sin = _rope_tables(n)
    cs_l = (np.stack([np.ones_like(cos), cos]), np.stack([np.zeros_like(sin), sin]))
    cs_c = (np.ones((1, m, HEAD_DIM), np.float32), np.zeros((1, m, HEAD_DIM), np.float32))
    src = list(range(P_CK, P_CK + 8)) + [P_DK, P_DK + 1] + list(range(P_CQ, P_CQ + 8)) + [P_CQ, P_CQ] + list(range(P_DQ, P_DQ + 8))
    gain = [1] * 8 + [3] * 2 + [0] * 8 + [0, 0] + [2] * 8
    rope = [0] * 8 + [1] * 2 + [0] * 8 + [0, 0] + [1] * 8
    tbl_l = jnp.asarray(np.array([src, gain, rope], np.int32))
    tbl_c = jnp.asarray(np.array([src[:10], gain[:10], [0] * 10], np.int32))
    qk_l = _qk_prep(p_l, tbl_l, gains, jnp.asarray(cs_l[0]), jnp.asarray(cs_l[1]))
    qk_c = _qk_prep(p_c, tbl_c, gains, jnp.asarray(cs_c[0]), jnp.asarray(cs_c[1]))
    c_out = _neighbourhood_attn(qk_l, p_l, qk_c, p_c, _na_bias(rpb, n))
    sink_rows = jnp.broadcast_to(sink.astype(jnp.float32)[:, None], (D_HEADS, LANES))
    d_out = _window_attn(qk_l, p_l, qk_c, p_c, sink_rows)
    return _matmul(jnp.concatenate([c_out, d_out], axis=-1), w_out)


SCAN_L = 128
GLA_SUB = 16
N_SCAN_STATES = 2 * A_HEADS
GATE_LR_LANE = 4 * A_HEADS


def _log_sigmoid(x):
    return jnp.minimum(x, 0.0) - jnp.log(1.0 + jnp.exp(-jnp.abs(x)))


def _tri_mask(rev):
    r = lax.broadcasted_iota(jnp.int32, (SCAN_L, SCAN_L), 0)
    c = lax.broadcasted_iota(jnp.int32, (SCAN_L, SCAN_L), 1)
    return (c >= r) if rev else (c <= r)


def _cumsum_rows(tri, x):
    hi = x.astype(jnp.bfloat16)
    r1 = x - hi.astype(jnp.float32)
    mid = r1.astype(jnp.bfloat16)
    lo = (r1 - mid.astype(jnp.float32)).astype(jnp.bfloat16)
    dot = lambda p: jnp.dot(tri, p, preferred_element_type=jnp.float32)
    return dot(hi) + dot(mid) + dot(lo)


def _conv_kernel(x_ref, w_ref, b_ref, o_ref, *, seg, scale):
    x = x_ref[...].astype(jnp.float32)
    n = x.shape[0]
    t = lax.broadcasted_iota(jnp.int32, x.shape, 0)
    first = (t == 0) | (t == seg)
    last = (t == seg - 1) | (t == n - 1)
    prev = jnp.where(first, 0.0, pltpu.roll(x, 1, 0))
    nxt = jnp.where(last, 0.0, pltpu.roll(x, n - 1, 0))
    y = w_ref[0:1, :] * prev + w_ref[1:2, :] * x + w_ref[2:3, :] * nxt + b_ref[...]
    y = y * jax.nn.sigmoid(y)
    j = pl.program_id(0)
    o_ref[...] = (y * jnp.where(j >= A_HEADS, scale, 1.0)).astype(o_ref.dtype)


def _short_conv_silu(p, conv_w, conv_b, seg):
    t = p.shape[0]
    nblk = 2 * A_WIDTH // LANES
    return pl.pallas_call(
        functools.partial(_conv_kernel, seg=seg, scale=HEAD_DIM ** -0.5),
        grid=(nblk,),
        in_specs=[pl.BlockSpec((t, LANES), lambda j: (0, j)),
                  pl.BlockSpec((8, LANES), lambda j: (0, j)),
                  pl.BlockSpec((1, LANES), lambda j: (0, j))],
        out_specs=pl.BlockSpec((t, LANES), lambda j: (0, j)),
        out_shape=jax.ShapeDtypeStruct((t, 2 * A_WIDTH), jnp.bfloat16),
        compiler_params=pltpu.CompilerParams(
            dimension_semantics=("parallel",), vmem_limit_bytes=VMEM_LIMIT_BYTES),
        name="short_conv_silu",
    )(p, jnp.pad(conv_w, ((0, 8 - CONV_W), (0, 0))), conv_b[None, :])


def _scan_chunk_index(j, n_ctx_chunks, n_chunks, rev):
    if not rev:
        return j
    return jnp.where(j < n_ctx_chunks, n_ctx_chunks - 1 - j, n_chunks + n_ctx_chunks - 1 - j)


def _mlstm_kernel(qf_ref, kf_ref, vf_ref, gf_ref, qb_ref, kb_ref, vb_ref, gb_ref, gbias_ref,
                  yf_ref, yb_ref, c_ref, m_ref):
    @pl.when(pl.program_id(0) == 0)
    def _():
        c_ref[...] = jnp.zeros_like(c_ref)
        m_ref[...] = jnp.zeros_like(m_ref)

    lane = lax.broadcasted_iota(jnp.int32, (SCAN_L, LANES), 1)
    is_forget = jnp.bitwise_and(lane, 2 * A_HEADS - 1) >= A_HEADS
    ones = jnp.ones((SCAN_L, HEAD_DIM), jnp.bfloat16)
    nt = (((1,), (1,)), ((), ()))
    tn = (((0,), (0,)), ((), ()))
    for z, (q_ref, k_ref, v_ref, g_ref, y_ref) in enumerate(
            ((qf_ref, kf_ref, vf_ref, gf_ref, yf_ref), (qb_ref, kb_ref, vb_ref, gb_ref, yb_ref))):
        rev = z == 1
        mask = _tri_mask(rev)
        gates = g_ref[...] + gbias_ref[...]
        gates = jnp.where(is_forget, _log_sigmoid(gates), gates)
        csum = _cumsum_rows(mask.astype(jnp.bfloat16), jnp.where(is_forget, gates, 0.0))
        gates_t = gates.T
        csum_t = csum.T
        end = 0 if rev else SCAN_L - 1
        for h in range(A_HEADS):
            li, lf = z * 2 * A_HEADS + h, z * 2 * A_HEADS + A_HEADS + h
            sl = slice(h * HEAD_DIM, (h + 1) * HEAD_DIM)
            q, k, v = q_ref[:, sl], k_ref[:, sl], v_ref[:, sl]
            b_col, b_row = csum[:, lf:lf + 1], csum_t[lf:lf + 1, :]
            i_col, i_row = gates[:, li:li + 1], gates_t[li:li + 1, :]
            st = z * A_HEADS + h
            m_prev = m_ref[st][0:1, 0:1]
            dlog = jnp.where(mask, b_col - b_row + i_row, NEG_INF)
            m_inter = b_col + m_prev
            m_t = jnp.maximum(m_inter, jnp.max(dlog, axis=-1, keepdims=True))
            s = lax.dot_general(q, k, nt, preferred_element_type=jnp.float32)
            sc = (s * jnp.exp(dlog - m_t)).astype(jnp.bfloat16)
            w_inter = jnp.exp(m_inter - m_t)
            v_ext = jnp.concatenate([v, ones], axis=1)
            r = (jnp.dot(sc, v_ext, preferred_element_type=jnp.float32)
                 + w_inter * jnp.dot(q, c_ref[st].astype(jnp.bfloat16), preferred_element_type=jnp.float32))
            num, den = r[:, :HEAD_DIM], r[:, HEAD_DIM:]
            y_ref[:, sl] = num / jnp.maximum(jnp.abs(den), jnp.exp(-m_t))
            m_new = m_t[end:end + 1, :]
            b_end = b_col[end:end + 1, :]
            w_end = jnp.exp(b_end - b_col + i_col - m_new)
            decay = jnp.exp(b_end + m_prev - m_new)
            kw = (k.astype(jnp.float32) * w_end).astype(jnp.bfloat16)
            c_ref[st] = decay * c_ref[st] + lax.dot_general(kw, v_ext, tn, preferred_element_type=jnp.float32)
            m_ref[st] = jnp.broadcast_to(m_new, (8, LANES))


def _scan_specs(n_ctx_chunks, n_chunks, width, col0, rev):
    return pl.BlockSpec((SCAN_L, width),
                        lambda j: (_scan_chunk_index(j, n_ctx_chunks, n_chunks, rev), col0))


def _mlstm(qk, p, gates, gate_bias, seg):
    t = qk.shape[0]
    nck = t // SCAN_L
    ncc = seg // SCAN_L
    ins, specs = [], []
    for rev in (False, True):
        ins += [qk, qk, p, gates]
        specs += [_scan_specs(ncc, nck, A_WIDTH, 0, rev), _scan_specs(ncc, nck, A_WIDTH, 1, rev),
                  _scan_specs(ncc, nck, A_WIDTH, 2, rev), _scan_specs(ncc, nck, LANES, 0, rev)]
    return pl.pallas_call(
        _mlstm_kernel,
        grid=(nck,),
        in_specs=specs + [pl.BlockSpec((1, LANES), lambda j: (0, 0))],
        out_specs=[_scan_specs(ncc, nck, A_WIDTH, 0, False), _scan_specs(ncc, nck, A_WIDTH, 0, True)],
        out_shape=[jax.ShapeDtypeStruct((t, A_WIDTH), jnp.float32)] * 2,
        scratch_shapes=[pltpu.VMEM((N_SCAN_STATES, HEAD_DIM, 2 * HEAD_DIM), jnp.float32),
                        pltpu.VMEM((N_SCAN_STATES, 8, LANES), jnp.float32)],
        compiler_params=pltpu.CompilerParams(
            dimension_semantics=("arbitrary",), vmem_limit_bytes=VMEM_LIMIT_BYTES),
        name="mlstm_scan",
    )(*ins, gate_bias)


def _gla_kernel(qf_ref, kf_ref, vf_ref, gf_ref, qb_ref, kb_ref, vb_ref, gb_ref, w2_ref, ab_ref,
                of_ref, ob_ref, s_ref):
    @pl.when(pl.program_id(0) == 0)
    def _():
        s_ref[...] = jnp.zeros_like(s_ref)

    nsub = SCAN_L // GLA_SUB
    row = lax.broadcasted_iota(jnp.int32, (SCAN_L, LANES), 0)
    lane = lax.broadcasted_iota(jnp.int32, (SCAN_L, LANES), 1)
    head_lane = (lane < B_KEY_DIM, lane >= B_KEY_DIM)
    r_sq = lax.broadcasted_iota(jnp.int32, (SCAN_L, SCAN_L), 0)
    c_sq = lax.broadcasted_iota(jnp.int32, (SCAN_L, SCAN_L), 1)
    sub_of_row = jnp.right_shift(row, 4)
    in_sub = jnp.bitwise_and(row, GLA_SUB - 1)
    sel_r = lax.broadcasted_iota(jnp.int32, (LANES, 2 * LANES), 0)
    sel_c = lax.broadcasted_iota(jnp.int32, (LANES, 2 * LANES), 1)
    head_sum = ((sel_r < B_KEY_DIM) == (sel_c < LANES)).astype(jnp.bfloat16)
    nt = (((1,), (1,)), ((), ()))
    tn = (((0,), (0,)), ((), ()))
    for z, (q_ref, k_ref, v_ref, g_ref, o_ref) in enumerate(
            ((qf_ref, kf_ref, vf_ref, gf_ref, of_ref), (qb_ref, kb_ref, vb_ref, gb_ref, ob_ref))):
        rev = z == 1
        tri = _tri_mask(rev).astype(jnp.bfloat16)
        lr = g_ref[...].astype(jnp.bfloat16)
        end = 0 if rev else SCAN_L - 1
        for pr in range(B_HEADS // 2):
            ls = slice(pr * LANES, (pr + 1) * LANES)
            za = jnp.dot(lr, w2_ref[z, :, ls], preferred_element_type=jnp.float32) + ab_ref[z:z + 1, ls]
            b = _cumsum_rows(tri, _log_sigmoid(za) * (1.0 / GLA_TAU))
            q = q_ref[:, ls].astype(jnp.float32) * (B_KEY_DIM ** -0.5)
            k = k_ref[:, ls].astype(jnp.float32)
            v = v_ref[:, 2 * pr * HEAD_DIM:(2 * pr + 2) * HEAD_DIM]
            b_ref_rows = jnp.zeros_like(b)
            for j in range(nsub):
                e = j * GLA_SUB if rev else j * GLA_SUB + GLA_SUB - 1
                b_ref_rows = jnp.where(sub_of_row == j, b[e:e + 1, :], b_ref_rows)
            k_sub = (k * jnp.exp(b_ref_rows - b)).astype(jnp.bfloat16)
            attn = [jnp.zeros((SCAN_L, SCAN_L), jnp.float32), jnp.zeros((SCAN_L, SCAN_L), jnp.float32)]
            for j in range(nsub):
                e = j * GLA_SUB if rev else j * GLA_SUB + GLA_SUB - 1
                if j == (0 if rev else nsub - 1):
                    continue
                later = (sub_of_row < j) if rev else (sub_of_row > j)
                qj = q * jnp.exp(jnp.where(later, b - b[e:e + 1, :], NEG_INF))
                kj = jnp.where(sub_of_row == j, k_sub, jnp.zeros_like(k_sub))
                for hh in range(2):
                    qjh = jnp.where(head_lane[hh], qj, 0.0).astype(jnp.bfloat16)
                    attn[hh] = attn[hh] + lax.dot_general(qjh, kj, nt, preferred_element_type=jnp.float32)
            for d in range(GLA_SUB):
                shift = (SCAN_L - d) % SCAN_L if rev else d
                k_s = pltpu.roll(k, shift, 0) if d else k
                b_s = pltpu.roll(b, shift, 0) if d else b
                ok_row = (in_sub <= GLA_SUB - 1 - d) if rev else (in_sub >= d)
                f = (q * k_s * jnp.exp(jnp.where(ok_row, b - b_s, NEG_INF))).astype(jnp.bfloat16)
                red = jnp.dot(f, head_sum, preferred_element_type=jnp.float32)
                on_diag = (c_sq == r_sq + d) if rev else (c_sq == r_sq - d)
                for hh in range(2):
                    attn[hh] = attn[hh] + jnp.where(on_diag, red[:, hh * LANES:(hh + 1) * LANES], 0.0)
            q_in = q * jnp.exp(b)
            b_end = b[end:end + 1, :]
            k_out = (k * jnp.exp(b_end - b)).astype(jnp.bfloat16)
            st = z * (B_HEADS // 2) + pr
            s_t = s_ref[st]
            s_bf = s_t.astype(jnp.bfloat16)
            for hh in range(2):
                vs = slice(hh * HEAD_DIM, (hh + 1) * HEAD_DIM)
                qh = jnp.where(head_lane[hh], q_in, 0.0).astype(jnp.bfloat16)
                o = (jnp.dot(attn[hh].astype(jnp.bfloat16), v[:, vs], preferred_element_type=jnp.float32)
                     + lax.dot_general(qh, s_bf[vs, :], nt, preferred_element_type=jnp.float32))
                o_ref[:, (2 * pr + hh) * HEAD_DIM:(2 * pr + hh + 1) * HEAD_DIM] = o
            s_ref[st] = s_t * jnp.exp(b_end) + lax.dot_general(v, k_out, tn, preferred_element_type=jnp.float32)


def _gla(p, gates, w2, alpha_b, seg):
    t = p.shape[0]
    nck = t // SCAN_L
    ncc = seg // SCAN_L
    ins, specs = [], []
    for rev in (False, True):
        ins += [p, p, p, gates]
        specs += [_scan_specs(ncc, nck, B_KEY_WIDTH, P_BQ // B_KEY_WIDTH, rev),
                  _scan_specs(ncc, nck, B_KEY_WIDTH, P_BK // B_KEY_WIDTH, rev),
                  _scan_specs(ncc, nck, B_VAL_WIDTH, P_BV // B_VAL_WIDTH, rev),
                  _scan_specs(ncc, nck, LANES, 0, rev)]
    return pl.pallas_call(
        _gla_kernel,
        grid=(nck,),
        in_specs=specs + [pl.BlockSpec((2, LANES, B_KEY_WIDTH), lambda j: (0, 0, 0)),
                          pl.BlockSpec((2, B_KEY_WIDTH), lambda j: (0, 0))],
        out_specs=[_scan_specs(ncc, nck, B_VAL_WIDTH, 0, False), _scan_specs(ncc, nck, B_VAL_WIDTH, 0, True)],
        out_shape=[jax.ShapeDtypeStruct((t, B_VAL_WIDTH), jnp.float32)] * 2,
        scratch_shapes=[pltpu.VMEM((B_HEADS, 2 * HEAD_DIM, LANES), jnp.float32)],
        compiler_params=pltpu.CompilerParams(
            dimension_semantics=("arbitrary",), vmem_limit_bytes=VMEM_LIMIT_BYTES),
        name="gla_scan",
    )(*ins, w2, alpha_b)


def _head_out_kernel(af_ref, ab_ref, bf_ref, bb_ref, gate_ref, g_ref, o_ref):
    is_gla = pl.program_id(1) >= A_HEADS
    y = jnp.where(is_gla, bf_ref[...] + bb_ref[...], af_ref[...] + ab_ref[...])
    yn = y * lax.rsqrt(jnp.mean(y * y, axis=-1, keepdims=True) + EPS) * g_ref[0]
    gate = gate_ref[...].astype(jnp.float32)
    o_ref[...] = (yn * jax.nn.sigmoid(gate) * jnp.where(is_gla, gate, 1.0)).astype(o_ref.dtype)


def _head_out(ya, yb, p, head_g):
    t = p.shape[0]
    bt = _row_tile(t)
    nh = A_HEADS + B_HEADS
    a_spec = pl.BlockSpec((bt, HEAD_DIM), lambda i, j: (i, jnp.minimum(j, A_HEADS - 1)))
    b_spec = pl.BlockSpec((bt, HEAD_DIM), lambda i, j: (i, jnp.maximum(j - A_HEADS, 0)))

    def gate_col(i, j):
        return (i, jnp.where(j < A_HEADS, P_AO + j, P_BR + j - A_HEADS))

    return pl.pallas_call(
        _head_out_kernel,
        grid=(t // bt, nh),
        in_specs=[a_spec, a_spec, b_spec, b_spec,
                  pl.BlockSpec((bt, HEAD_DIM), gate_col),
                  pl.BlockSpec((1, 1, HEAD_DIM), lambda i, j: (j // A_HEADS, 0, 0))],
        out_specs=pl.BlockSpec((bt, HEAD_DIM), lambda i, j: (i, j)),
        out_shape=jax.ShapeDtypeStruct((t, nh * HEAD_DIM), jnp.bfloat16),
        compiler_params=pltpu.CompilerParams(
            dimension_semantics=("parallel", "arbitrary"), vmem_limit_bytes=VMEM_LIMIT_BYTES),
        name="head_out",
    )(ya[0], ya[1], yb[0], yb[1], p, head_g.astype(jnp.float32)[:, None, :])


P_AO = 24
P_BQ, P_BK, P_BV = 4096, 4608, 5120
P_BR = 48


def _mixer_ab(h, w_in, conv_w, conv_b, gate_b, alpha_w2, alpha_b, head_g, w_out, seg):
    bf = jnp.bfloat16
    cols = np.cumsum((0,) + AB_SPLITS)
    pick = lambda *ids: jnp.concatenate([w_in[:, cols[i]:cols[i + 1]] for i in ids], axis=1)
    w_main = pick(0, 1, 2, 3, 5, 6, 7, 8).astype(bf)
    w_gate = jnp.pad(pick(4, 9), ((0, 0), (0, LANES - 4 * A_HEADS - 2 * GLA_RANK))).astype(bf)
    hb = h.astype(bf)
    p = _matmul(hb, w_main, bf)
    gates = _matmul(hb, w_gate, jnp.float32)
    gate_bias = jnp.pad(gate_b.reshape(1, -1).astype(jnp.float32), ((0, 0), (0, LANES - 4 * A_HEADS)))
    w2 = jnp.zeros((2, LANES, B_KEY_WIDTH), jnp.float32)
    for z in range(2):
        w2 = w2.at[z, GATE_LR_LANE + z * GLA_RANK:GATE_LR_LANE + (z + 1) * GLA_RANK].set(alpha_w2[z])
    qk = _short_conv_silu(p, conv_w, conv_b, seg)
    ya = _mlstm(qk, p, gates, gate_bias, seg)
    yb = _gla(p, gates, w2.astype(bf), alpha_b.astype(jnp.float32), seg)
    return _matmul(_head_out(ya, yb, p, head_g), w_out)


def _rmsnorm(x, g):
    xf = x.astype(jnp.float32)
    y = xf * lax.rsqrt(jnp.mean(xf * xf, axis=-1, keepdims=True) + EPS)
    return (y * g.astype(jnp.float32)).astype(x.dtype)


def _modulate(x, g, shift, scale):
    return _rmsnorm(x, g) * (1 + scale) + shift


def kernel(x, c, ctx, c_ctx, mod_w, mod_b, norm_g, ab_w_in, ab_conv_w, ab_conv_b, ab_gate_b, ab_alpha_w2,
           ab_alpha_b, ab_head_g, ab_w_out, cd_w_in, cd_qk_g, cd_rpb, cd_sink, cd_w_out, peer_w_q,
           peer_sub_keys, peer_u, peer_v):
    assert DEPTH == 2
    bf = jnp.bfloat16
    m = ctx.shape[1]
    xs = jnp.concatenate([ctx[0], x[0]], axis=0)
    is_ctx = (jnp.arange(xs.shape[0]) < m)[:, None]
    for l in range(DEPTH):
        j = l // 2
        mod_l = jnp.split(jax.nn.silu(c) @ mod_w[l] + mod_b[l], 6, axis=-1)
        mod_c = jnp.split((jax.nn.silu(c_ctx) @ mod_w[l] + mod_b[l])[None, :], 6, axis=-1)
        wq = peer_w_q[l].astype(bf)
        keys = peer_sub_keys[l].astype(bf)
        ut = peer_u[l].T.astype(bf)
        vt = peer_v[l].astype(bf)
        if l % 2 == 0:
            mod = [jnp.where(is_ctx, mc, ml) for mc, ml in zip(mod_c, mod_l)]
            h = _modulate(xs, norm_g[l, 0], mod[0], mod[1])
            y = _mixer_ab(h, ab_w_in[j], ab_conv_w[j], ab_conv_b[j], ab_gate_b[j], ab_alpha_w2[j],
                          ab_alpha_b[j], ab_head_g[j], ab_w_out[j].astype(bf), m)
            xs = xs + mod[2] * y
            h = _modulate(xs, norm_g[l, 1], mod[3], mod[4])
            xs = xs + mod[5] * _peer(h, wq, keys, ut, vt)
        else:
            xc, xl = xs[:m], xs[m:]
            hc = _modulate(xc, norm_g[l, 0], mod_c[0], mod_c[1])
            hl = _modulate(xl, norm_g[l, 0], mod_l[0], mod_l[1])
            y = _mixer_cd(hc, hl, cd_w_in[j].astype(bf), cd_qk_g[j], cd_rpb[j], cd_sink[j], cd_w_out[j].astype(bf))
            xl = xl + mod_l[2] * y
            h = _modulate(xl, norm_g[l, 1], mod_l[3], mod_l[4])
            xs = xl + mod_l[5] * _peer(h, wq, keys, ut, vt)
    return xs[None]
```

```python
import functools

import numpy as np
import jax
import jax.numpy as jnp
from jax import lax
from jax.experimental import pallas as pl
from jax.experimental.pallas import tpu as pltpu

D_MODEL = 2048
SEQ = 8192
DEPTH = 2
GRID_W = 64
CTX_LEN = 256
HEAD_DIM = 128
N_GROUP_HEADS = 8
EPS = 1e-6
A_HEADS = 8
A_WIDTH = 1024
CONV_W = 3
B_HEADS = 8
B_KEY_DIM = 64
B_KEY_WIDTH = 512
B_VAL_WIDTH = 1024
GLA_RANK = 16
GLA_TAU = 16.0
SCAN_CHUNK = 64
C_HEADS = 8
C_WIDTH = 1024
WIN_H = 8
WIN_W = 16
D_HEADS = 8
D_KV_HEADS = 2
D_GROUP = 4
D_WIDTH = 1024
D_KV_WIDTH = 256
WINDOW = 128
ROPE_BASE = 10000.0
PEER_HEADS = 8
PEER_NKEYS = 128
PEER_EXPERTS = PEER_NKEYS * PEER_NKEYS
PEER_QDIM = 256
PEER_TOPK = 16
PEER_SLOTS = PEER_HEADS * PEER_TOPK
AB_SPLITS = (A_WIDTH, A_WIDTH, A_WIDTH, A_WIDTH, 4 * A_HEADS,
             B_KEY_WIDTH, B_KEY_WIDTH, B_VAL_WIDTH, B_VAL_WIDTH, 2 * GLA_RANK)
CD_SPLITS = (C_WIDTH, C_WIDTH, D_KV_WIDTH, D_KV_WIDTH, C_WIDTH, D_WIDTH)
CD_KV_COLS = sum(CD_SPLITS[:4])

VMEM_LIMIT_BYTES = 56 * 1024 * 1024
LANES = 128


def _row_tile(m):
    for t in (1024, 768, 512, 256, 128):
        if m % t == 0:
            return t
    raise ValueError(f"unsupported row count {m}")


def _mm_kernel(x_ref, w_ref, o_ref):
    o_ref[...] = jnp.dot(x_ref[...], w_ref[...], preferred_element_type=jnp.float32).astype(o_ref.dtype)


def _matmul(x, w, out_dtype=jnp.float32):
    m, k = x.shape
    n = w.shape[1]
    bn = min(512, n)
    n_pad = -(-n // bn) * bn
    if n_pad != n:
        w = jnp.pad(w, ((0, 0), (0, n_pad - n)))
    bm = _row_tile(m)
    out = pl.pallas_call(
        _mm_kernel,
        grid=(m // bm, n_pad // bn),
        in_specs=[pl.BlockSpec((bm, k), lambda i, j: (i, 0)),
                  pl.BlockSpec((k, bn), lambda i, j: (0, j))],
        out_specs=pl.BlockSpec((bm, bn), lambda i, j: (i, j)),
        out_shape=jax.ShapeDtypeStruct((m, n_pad), out_dtype),
        compiler_params=pltpu.CompilerParams(
            dimension_semantics=("parallel", "arbitrary"), vmem_limit_bytes=VMEM_LIMIT_BYTES),
        name="matmul",
    )(x.astype(jnp.bfloat16), w.astype(jnp.bfloat16))
    return out[:, :n] if n_pad != n else out


ROUTE_TOKENS = (768, 512, 256, 128)
GATE_TOKENS = 128
NEG_INF = float("-inf")


def _top16(s, key, big):
    slot = lax.broadcasted_iota(jnp.int32, (PEER_TOPK, s.shape[1]), 0)
    vals = jnp.zeros((PEER_TOPK, s.shape[1]), jnp.float32)
    keys = jnp.zeros((PEER_TOPK, s.shape[1]), jnp.float32)
    for it in range(PEER_TOPK):
        m = jnp.max(s, axis=0, keepdims=True)
        k = jnp.min(jnp.where(s == m, key, big), axis=0, keepdims=True)
        vals = jnp.where(slot == it, m, vals)
        keys = jnp.where(slot == it, k, keys)
        s = jnp.where(key == k, NEG_INF, s)
    return vals, keys


def _route_kernel(h_ref, wq_ref, k_ref, i1_ref, i2_ref, g_ref):
    bt = h_ref.shape[0]
    q = jnp.dot(h_ref[...], wq_ref[...], preferred_element_type=jnp.float32).astype(jnp.bfloat16)
    nt = (((1,), (1,)), ((), ()))
    s1 = lax.dot_general(k_ref[0, 0], q[:, :PEER_NKEYS], nt, preferred_element_type=jnp.float32)
    s2 = lax.dot_general(k_ref[0, 1], q[:, PEER_NKEYS:], nt, preferred_element_type=jnp.float32)
    row = lax.broadcasted_iota(jnp.int32, (PEER_NKEYS, bt), 0).astype(jnp.float32)
    v1, r1 = _top16(s1, row, float(PEER_NKEYS))
    v2, r2 = _top16(s2, row, float(PEER_NKEYS))
    i16 = lax.broadcasted_iota(jnp.int32, (16, bt), 0).astype(jnp.float32)
    i8 = lax.broadcasted_iota(jnp.int32, (8, bt), 0).astype(jnp.float32)
    cand = [v1[0:1] + v2]
    flat = [i16]
    for a in range(1, 8):
        cand.append(v1[a:a + 1] + v2[0:8])
        flat.append(i8 + float(a * PEER_TOPK))
    cand.append(v1[8:16] + v2[0:1])
    flat.append((i8 + 8.0) * float(PEER_TOPK))
    best, key = _top16(jnp.concatenate(cand, axis=0), jnp.concatenate(flat, axis=0), float(PEER_TOPK * PEER_TOPK))
    key = key.astype(jnp.int32)
    a_sel = jnp.right_shift(key, 4)
    b_sel = jnp.bitwise_and(key, PEER_TOPK - 1)
    e1 = jnp.zeros_like(r1)
    e2 = jnp.zeros_like(r2)
    for r in range(PEER_TOPK):
        e1 = jnp.where(a_sel == r, r1[r:r + 1], e1)
        e2 = jnp.where(b_sel == r, r2[r:r + 1], e2)
    e1 = e1.astype(jnp.int32)
    e2 = e2.astype(jnp.int32)
    ex = jnp.exp(best - best[0:1])
    i1_ref[...] = e1
    i2_ref[...] = e2
    g_ref[...] = ex / jnp.sum(ex, axis=0, keepdims=True)


def _route(hb, wq, keys):
    t, d = hb.shape
    bt = next(b for b in ROUTE_TOKENS if t % b == 0)
    slot_spec = pl.BlockSpec((PEER_TOPK, bt), lambda i, h: (h, i))
    return pl.pallas_call(
        _route_kernel,
        grid=(t // bt, PEER_HEADS),
        in_specs=[pl.BlockSpec((bt, d), lambda i, h: (i, 0)),
                  pl.BlockSpec((d, PEER_QDIM), lambda i, h: (0, h)),
                  pl.BlockSpec((1, 2, PEER_NKEYS, PEER_QDIM // 2), lambda i, h: (h, 0, 0, 0))],
        out_specs=[slot_spec, slot_spec, slot_spec],
        out_shape=[jax.ShapeDtypeStruct((PEER_SLOTS, t), jnp.int32),
                   jax.ShapeDtypeStruct((PEER_SLOTS, t), jnp.int32),
                   jax.ShapeDtypeStruct((PEER_SLOTS, t), jnp.float32)],
        compiler_params=pltpu.CompilerParams(
            dimension_semantics=("parallel", "arbitrary"), vmem_limit_bytes=VMEM_LIMIT_BYTES),
        name="peer_route",
    )(hb, wq, keys)


def _gate_matrix_kernel(i1_ref, i2_ref, g_ref, o_ref, i1_t, g_t):
    tb = o_ref.shape[0]
    i1_t[...] = i1_ref[...].T
    g_t[...] = g_ref[...].T
    row_id = lax.broadcasted_iota(jnp.int32, (PEER_NKEYS, PEER_SLOTS), 0)
    lane_id = lax.broadcasted_iota(jnp.int32, (PEER_SLOTS, PEER_NKEYS), 1)
    zero = jnp.zeros((PEER_NKEYS, PEER_SLOTS), jnp.bfloat16)

    for tp in range(tb // 2):
        ta = 2 * tp
        tb_ = ta + 1
        ca = jnp.where(row_id == i1_t[ta:ta + 1, :], g_t[ta:ta + 1, :], 0.0).astype(jnp.bfloat16)
        cb = jnp.where(row_id == i1_t[tb_:tb_ + 1, :], g_t[tb_:tb_ + 1, :], 0.0).astype(jnp.bfloat16)
        oa = jnp.where(i2_ref[:, ta:ta + 1] == lane_id, 1.0, 0.0).astype(jnp.bfloat16)
        ob = jnp.where(i2_ref[:, tb_:tb_ + 1] == lane_id, 1.0, 0.0).astype(jnp.bfloat16)
        lhs = jnp.concatenate([ca, cb], axis=1)
        rhs = jnp.concatenate([jnp.concatenate([oa, zero], axis=1),
                               jnp.concatenate([zero, ob], axis=1)], axis=0)
        out = jnp.dot(lhs, rhs, preferred_element_type=jnp.float32)
        o_ref[ta] = out[:, :PEER_NKEYS].astype(o_ref.dtype)
        o_ref[tb_] = out[:, PEER_NKEYS:].astype(o_ref.dtype)


def _gate_matrix(i1, i2, g):
    t = i1.shape[1]
    tb = GATE_TOKENS
    out = pl.pallas_call(
        _gate_matrix_kernel,
        grid=(t // tb,),
        in_specs=[pl.BlockSpec((PEER_SLOTS, tb), lambda i: (0, i))] * 3,
        out_specs=pl.BlockSpec((tb, PEER_NKEYS, PEER_NKEYS), lambda i: (i, 0, 0)),
        out_shape=jax.ShapeDtypeStruct((t, PEER_NKEYS, PEER_NKEYS), jnp.bfloat16),
        scratch_shapes=[pltpu.VMEM((tb, PEER_SLOTS), jnp.int32),
                        pltpu.VMEM((tb, PEER_SLOTS), jnp.float32)],
        compiler_params=pltpu.CompilerParams(
            dimension_semantics=("parallel",), vmem_limit_bytes=VMEM_LIMIT_BYTES),
        name="peer_gate_matrix",
    )(i1, i2, g)
    return out.reshape(t, PEER_EXPERTS)


def _gelu_tanh(x):
    return 0.5 * x * (1.0 + jnp.tanh(0.7978845608028654 * (x + 0.044715 * x * x * x)))


def _cast_kernel(x_ref, o_ref):
    o_ref[...] = x_ref[...].astype(o_ref.dtype)


def _cast_bf16(x):
    nl, r, c = x.shape
    br = 1024
    return pl.pallas_call(
        _cast_kernel,
        grid=(nl, r // br),
        in_specs=[pl.BlockSpec((1, br, c), lambda l, i: (l, i, 0))],
        out_specs=pl.BlockSpec((1, br, c), lambda l, i: (l, i, 0)),
        out_shape=jax.ShapeDtypeStruct(x.shape, jnp.bfloat16),
        compiler_params=pltpu.CompilerParams(
            dimension_semantics=("parallel", "parallel"), vmem_limit_bytes=VMEM_LIMIT_BYTES),
        name="cast_bf16",
    )(x)


def _peer_dense_kernel(h_ref, u_ref, v_ref, g_ref, o_ref):
    j = pl.program_id(1)
    nt = (((1,), (1,)), ((), ()))
    act = _gelu_tanh(lax.dot_general(h_ref[...], u_ref[0], nt, preferred_element_type=jnp.float32))
    w = (act * g_ref[...].astype(jnp.float32)).astype(jnp.bfloat16)
    contrib = jnp.dot(w, v_ref[0], preferred_element_type=jnp.float32)

    @pl.when(j == 0)
    def _():
        o_ref[...] = contrib

    @pl.when(j != 0)
    def _():
        o_ref[...] += contrib


def _peer_dense(h, u, v, gmat, layer):
    t, d = h.shape
    e = v.shape[1]
    bm = _row_tile(t)
    be = 1024
    return pl.pallas_call(
        _peer_dense_kernel,
        grid=(t // bm, e // be),
        in_specs=[pl.BlockSpec((bm, d), lambda i, j: (i, 0)),
                  pl.BlockSpec((1, be, d), lambda i, j: (layer, j, 0)),
                  pl.BlockSpec((1, be, d), lambda i, j: (layer, j, 0)),
                  pl.BlockSpec((bm, be), lambda i, j: (i, j))],
        out_specs=pl.BlockSpec((bm, d), lambda i, j: (i, 0)),
        out_shape=jax.ShapeDtypeStruct((t, d), jnp.float32),
        compiler_params=pltpu.CompilerParams(
            dimension_semantics=("parallel", "arbitrary"), vmem_limit_bytes=VMEM_LIMIT_BYTES),
        name="peer_dense",
    )(h, u, v, gmat)


def _peer(h, wq, keys, u, v, layer):
    hb = h.astype(jnp.bfloat16)
    i1, i2, g = _route(hb, wq, keys)
    return _peer_dense(hb, u, v, _gate_matrix(i1, i2, g), layer)


ROPE_HALF = 32
NA_QROWS = 8
NA_KROWS = 16
SWA_QB = 256
SWA_KB = SWA_QB + 2 * WINDOW

P_CK, P_CV, P_DK, P_DV, P_CQ, P_DQ = 0, 8, 16, 18, 20, 28
QK_CK, QK_DK, QK_CQ, QK_DQ = 0, 8, 10, 20


def _rope_tables(n):
    t = np.arange(n)
    freqs = (np.float32(ROPE_BASE) ** (-np.arange(ROPE_HALF, dtype=np.float32) / np.float32(ROPE_HALF))).astype(np.float32)
    ang_r = (t // GRID_W).astype(np.float32)[:, None] * freqs
    ang_c = (t % GRID_W).astype(np.float32)[:, None] * freqs
    cos = np.concatenate([np.cos(ang_r), np.cos(ang_r), np.cos(ang_c), np.cos(ang_c)], axis=-1)
    sin = np.concatenate([-np.sin(ang_r), np.sin(ang_r), -np.sin(ang_c), np.sin(ang_c)], axis=-1)
    return cos.astype(np.float32), sin.astype(np.float32)


def _qk_prep_kernel(tbl_ref, p_ref, g_ref, cos_ref, sin_ref, o_ref):
    del tbl_ref
    x = p_ref[...].astype(jnp.float32)
    y = x * lax.rsqrt(jnp.mean(x * x, axis=-1, keepdims=True) + EPS) * g_ref[0]
    lane = lax.broadcasted_iota(jnp.int32, y.shape, 1)
    first = jnp.bitwise_and(lane, 2 * ROPE_HALF - 1) < ROPE_HALF
    partner = jnp.where(first, pltpu.roll(y, LANES - ROPE_HALF, 1), pltpu.roll(y, ROPE_HALF, 1))
    o_ref[...] = (y * cos_ref[0] + partner * sin_ref[0]).astype(o_ref.dtype)


def _qk_prep(p, table, gains, cos, sin):
    n = p.shape[0]
    ncol = table.shape[1]
    bt = 2048 if n % 2048 == 0 else n
    grid_spec = pltpu.PrefetchScalarGridSpec(
        num_scalar_prefetch=1,
        grid=(n // bt, ncol),
        in_specs=[pl.BlockSpec((bt, HEAD_DIM), lambda i, j, tbl: (i, tbl[0, j])),
                  pl.BlockSpec((1, 1, HEAD_DIM), lambda i, j, tbl: (tbl[1, j], 0, 0)),
                  pl.BlockSpec((1, bt, HEAD_DIM), lambda i, j, tbl: (tbl[2, j], i, 0)),
                  pl.BlockSpec((1, bt, HEAD_DIM), lambda i, j, tbl: (tbl[2, j], i, 0))],
        out_specs=pl.BlockSpec((bt, HEAD_DIM), lambda i, j, tbl: (i, j)),
    )
    return pl.pallas_call(
        _qk_prep_kernel,
        grid_spec=grid_spec,
        out_shape=jax.ShapeDtypeStruct((n, ncol * HEAD_DIM), jnp.bfloat16),
        compiler_params=pltpu.CompilerParams(
            dimension_semantics=("parallel", "arbitrary"), vmem_limit_bytes=VMEM_LIMIT_BYTES),
        name="qk_prep",
    )(table, p, gains, cos, sin)


def _na_bias(rpb, n):
    rows = n // GRID_W
    nr, nc = 2 * WIN_H - 1, 2 * WIN_W - 1
    c = np.arange(GRID_W)[:, None]
    kc = np.arange(GRID_W)[None, :]
    cs = np.clip(c - WIN_W // 2, 0, GRID_W - WIN_W)
    col_ok = (kc >= cs) & (kc < cs + WIN_W)
    oh_col = (np.clip(kc - c + WIN_W - 1, 0, nc - 1)[None] == np.arange(nc)[:, None, None]).astype(np.float32)
    oh_row, ok = [], []
    for r0, rlo in ((NA_QROWS, NA_QROWS - 4), (0, 0), (rows - NA_QROWS, rows - NA_KROWS)):
        r = r0 + np.arange(NA_QROWS)[:, None]
        kr = rlo + np.arange(NA_KROWS)[None, :]
        rs = np.clip(r - WIN_H // 2, 0, rows - WIN_H)
        row_ok = (kr >= rs) & (kr < rs + WIN_H)
        oh_row.append((np.clip(kr - r + WIN_H - 1, 0, nr - 1)[..., None] == np.arange(nr)).astype(np.float32))
        ok.append((row_ok[:, None, :, None] & col_ok[None, :, None, :]).reshape(NA_QROWS * GRID_W, NA_KROWS * GRID_W))
    hp = lax.Precision.HIGHEST
    colb = jnp.einsum('hab,bcd->hacd', rpb.astype(jnp.float32), jnp.asarray(oh_col), precision=hp)
    bias = jnp.einsum('tqka,hacd->htqckd', jnp.asarray(np.stack(oh_row)), colb, precision=hp)
    bias = bias.reshape(rpb.shape[0], 3, NA_QROWS * GRID_W, NA_KROWS * GRID_W)
    return jnp.where(jnp.asarray(np.stack(ok))[None], bias, NEG_INF)


def _softmax_pv(parts, extra_logit=None):
    m = parts[0][0].max(axis=-1, keepdims=True)
    for s, _ in parts[1:]:
        m = jnp.maximum(m, s.max(axis=-1, keepdims=True))
    if extra_logit is not None:
        m = jnp.maximum(m, extra_logit)
    den = jnp.exp(extra_logit - m) if extra_logit is not None else 0.0
    acc = None
    for s, v in parts:
        p = jnp.exp(s - m)
        den = den + p.sum(axis=-1, keepdims=True)
        pv = jnp.dot(p.astype(jnp.bfloat16), v, preferred_element_type=jnp.float32)
        acc = pv if acc is None else acc + pv
    return acc / den


def _na_kernel(q_ref, k_ref, v_ref, kc_ref, vc_ref, bias_ref, o_ref):
    n = q_ref.shape[0]
    qb = NA_QROWS * GRID_W
    kb = NA_KROWS * GRID_W
    nblk = n // qb
    nt = (((1,), (1,)), ((), ()))
    kc = kc_ref[...]
    vc = vc_ref[...]

    def body(b, carry):
        q0 = pl.multiple_of(b * qb, qb)
        ks = pl.multiple_of(jnp.clip(b * qb - (NA_KROWS - NA_QROWS) // 2 * GRID_W, 0, n - kb), 2 * LANES)
        kind = jnp.where(b == 0, 1, jnp.where(b == nblk - 1, 2, 0))
        q = q_ref[pl.ds(q0, qb), :]
        s_w = lax.dot_general(q, k_ref[pl.ds(ks, kb), :], nt, preferred_element_type=jnp.float32) + bias_ref[0, kind]
        s_c = lax.dot_general(q, kc, nt, preferred_element_type=jnp.float32)
        o = _softmax_pv([(s_w, v_ref[pl.ds(ks, kb), :]), (s_c, vc)])
        o_ref[pl.ds(q0, qb), :] = o.astype(o_ref.dtype)
        return carry

    lax.fori_loop(0, nblk, body, 0)


def _neighbourhood_attn(qk, p, qk_c, p_c, bias):
    n = qk.shape[0]
    m = qk_c.shape[0]

    def full(col0):
        return pl.BlockSpec((n, HEAD_DIM), lambda h: (0, col0 + h))

    def ctx(col0):
        return pl.BlockSpec((m, HEAD_DIM), lambda h: (0, col0 + h))

    return pl.pallas_call(
        _na_kernel,
        grid=(C_HEADS,),
        in_specs=[full(QK_CQ), full(QK_CK), full(P_CV), ctx(QK_CK), ctx(P_CV),
                  pl.BlockSpec((1, 3, NA_QROWS * GRID_W, NA_KROWS * GRID_W), lambda h: (h, 0, 0, 0))],
        out_specs=pl.BlockSpec((n, HEAD_DIM), lambda h: (0, h)),
        out_shape=jax.ShapeDtypeStruct((n, C_WIDTH), jnp.bfloat16),
        compiler_params=pltpu.CompilerParams(
            dimension_semantics=("parallel",), vmem_limit_bytes=VMEM_LIMIT_BYTES),
        name="neighbourhood_attn",
    )(qk, qk, p, qk_c, p_c, bias)


def _swa_kernel(q_ref, k_ref, v_ref, kc_ref, vc_ref, sink_ref, o_ref):
    n = q_ref.shape[0]
    nblk = n // SWA_QB
    nt = (((1,), (1,)), ((), ()))
    kc = kc_ref[...]
    vc = vc_ref[...]
    kv = pl.program_id(0)
    sink = jnp.concatenate(
        [jnp.broadcast_to(sink_ref[pl.ds(kv * D_GROUP + g, 1), 0:1], (SWA_QB, 1)) for g in range(D_GROUP)], axis=0)
    q_off = jnp.bitwise_and(lax.broadcasted_iota(jnp.int32, (D_GROUP * SWA_QB, SWA_KB), 0), SWA_QB - 1)
    k_off = lax.broadcasted_iota(jnp.int32, (D_GROUP * SWA_QB, SWA_KB), 1)
    rel = q_off - k_off

    def body(b, carry):
        q0 = pl.multiple_of(b * SWA_QB, SWA_QB)
        ks = pl.multiple_of(jnp.clip(q0 - WINDOW, 0, n - SWA_KB), LANES)
        q = jnp.concatenate([q_ref[pl.ds(q0, SWA_QB), g * HEAD_DIM:(g + 1) * HEAD_DIM] for g in range(D_GROUP)], axis=0)
        s_w = lax.dot_general(q, k_ref[pl.ds(ks, SWA_KB), :], nt, preferred_element_type=jnp.float32)
        s_w = jnp.where(jnp.abs(rel + (q0 - ks)) <= WINDOW, s_w, NEG_INF)
        s_c = lax.dot_general(q, kc, nt, preferred_element_type=jnp.float32)
        o = _softmax_pv([(s_w, v_ref[pl.ds(ks, SWA_KB), :]), (s_c, vc)], extra_logit=sink)
        for g in range(D_GROUP):
            o_ref[pl.ds(q0, SWA_QB), g * HEAD_DIM:(g + 1) * HEAD_DIM] = o[g * SWA_QB:(g + 1) * SWA_QB].astype(o_ref.dtype)
        return carry

    lax.fori_loop(0, nblk, body, 0)


def _window_attn(qk, p, qk_c, p_c, sink_rows):
    n = qk.shape[0]
    m = qk_c.shape[0]
    gw = D_GROUP * HEAD_DIM
    return pl.pallas_call(
        _swa_kernel,
        grid=(D_KV_HEADS,),
        in_specs=[pl.BlockSpec((n, gw), lambda kv: (0, QK_DQ // D_GROUP + kv)),
                  pl.BlockSpec((n, HEAD_DIM), lambda kv: (0, QK_DK + kv)),
                  pl.BlockSpec((n, HEAD_DIM), lambda kv: (0, P_DV + kv)),
                  pl.BlockSpec((m, HEAD_DIM), lambda kv: (0, QK_DK + kv)),
                  pl.BlockSpec((m, HEAD_DIM), lambda kv: (0, P_DV + kv)),
                  pl.BlockSpec((D_HEADS, LANES), lambda kv: (0, 0))],
        out_specs=pl.BlockSpec((n, gw), lambda kv: (0, kv)),
        out_shape=jax.ShapeDtypeStruct((n, D_WIDTH), jnp.bfloat16),
        compiler_params=pltpu.CompilerParams(
            dimension_semantics=("parallel",), vmem_limit_bytes=VMEM_LIMIT_BYTES),
        name="window_attn",
    )(qk, qk, p, qk_c, p_c, sink_rows)


def _mixer_cd(h_c, h_l, w_in, qk_g, rpb, sink, w_out):
    n = h_l.shape[0]
    m = h_c.shape[0]
    scale = HEAD_DIM ** -0.5
    p_l = _matmul(h_l, w_in, jnp.bfloat16)
    p_c = _matmul(h_c, w_in[:, :CD_KV_COLS], jnp.bfloat16)
    gains = jnp.stack([qk_g[0] * scale, qk_g[1], qk_g[2] * scale, qk_g[3]]).astype(jnp.float32)[:, None, :]
    cos, sin = _rope_tables(n)
    cs_l = (np.stack([np.ones_like(cos), cos]), np.stack([np.zeros_like(sin), sin]))
    cs_c = (np.ones((1, m, HEAD_DIM), np.float32), np.zeros((1, m, HEAD_DIM), np.float32))
    src = list(range(P_CK, P_CK + 8)) + [P_DK, P_DK + 1] + list(range(P_CQ, P_CQ + 8)) + [P_CQ, P_CQ] + list(range(P_DQ, P_DQ + 8))
    gain = [1] * 8 + [3] * 2 + [0] * 8 + [0, 0] + [2] * 8
    rope = [0] * 8 + [1] * 2 + [0] * 8 + [0, 0] + [1] * 8
    tbl_l = jnp.asarray(np.array([src, gain, rope], np.int32))
    tbl_c = jnp.asarray(np.array([src[:10], gain[:10], [0] * 10], np.int32))
    qk_l = _qk_prep(p_l, tbl_l, gains, jnp.asarray(cs_l[0]), jnp.asarray(cs_l[1]))
    qk_c = _qk_prep(p_c, tbl_c, gains, jnp.asarray(cs_c[0]), jnp.asarray(cs_c[1]))
    c_out = _neighbourhood_attn(qk_l, p_l, qk_c, p_c, _na_bias(rpb, n))
    sink_rows = jnp.broadcast_to(sink.astype(jnp.float32)[:, None], (D_HEADS, LANES))
    d_out = _window_attn(qk_l, p_l, qk_c, p_c, sink_rows)
    return _matmul(jnp.concatenate([c_out, d_out], axis=-1), w_out)


SCAN_L = 128
GLA_SUB = 16
N_SCAN_STATES = 2 * A_HEADS
GATE_LR_LANE = 4 * A_HEADS


def _log_sigmoid(x):
    return jnp.minimum(x, 0.0) - jnp.log(1.0 + jnp.exp(-jnp.abs(x)))


def _tri_mask(rev):
    r = lax.broadcasted_iota(jnp.int32, (SCAN_L, SCAN_L), 0)
    c = lax.broadcasted_iota(jnp.int32, (SCAN_L, SCAN_L), 1)
    return (c >= r) if rev else (c <= r)


def _cumsum_rows(tri, x):
    hi = x.astype(jnp.bfloat16)
    r1 = x - hi.astype(jnp.float32)
    mid = r1.astype(jnp.bfloat16)
    lo = (r1 - mid.astype(jnp.float32)).astype(jnp.bfloat16)
    dot = lambda p: jnp.dot(tri, p, preferred_element_type=jnp.float32)
    return dot(hi) + dot(mid) + dot(lo)


def _conv_kernel(x_ref, w_ref, b_ref, o_ref, *, seg, scale):
    x = x_ref[...].astype(jnp.float32)
    n = x.shape[0]
    t = lax.broadcasted_iota(jnp.int32, x.shape, 0)
    first = (t == 0) | (t == seg)
    last = (t == seg - 1) | (t == n - 1)
    prev = jnp.where(first, 0.0, pltpu.roll(x, 1, 0))
    nxt = jnp.where(last, 0.0, pltpu.roll(x, n - 1, 0))
    y = w_ref[0:1, :] * prev + w_ref[1:2, :] * x + w_ref[2:3, :] * nxt + b_ref[...]
    y = y * jax.nn.sigmoid(y)
    j = pl.program_id(0)
    o_ref[...] = (y * jnp.where(j >= A_HEADS, scale, 1.0)).astype(o_ref.dtype)


def _short_conv_silu(p, conv_w, conv_b, seg):
    t = p.shape[0]
    nblk = 2 * A_WIDTH // LANES
    return pl.pallas_call(
        functools.partial(_conv_kernel, seg=seg, scale=HEAD_DIM ** -0.5),
        grid=(nblk,),
        in_specs=[pl.BlockSpec((t, LANES), lambda j: (0, j)),
                  pl.BlockSpec((8, LANES), lambda j: (0, j)),
                  pl.BlockSpec((1, LANES), lambda j: (0, j))],
        out_specs=pl.BlockSpec((t, LANES), lambda j: (0, j)),
        out_shape=jax.ShapeDtypeStruct((t, 2 * A_WIDTH), jnp.bfloat16),
        compiler_params=pltpu.CompilerParams(
            dimension_semantics=("parallel",), vmem_limit_bytes=VMEM_LIMIT_BYTES),
        name="short_conv_silu",
    )(p, jnp.pad(conv_w, ((0, 8 - CONV_W), (0, 0))), conv_b[None, :])


def _scan_chunk_index(j, n_ctx_chunks, n_chunks, rev):
    if not rev:
        return j
    return jnp.where(j < n_ctx_chunks, n_ctx_chunks - 1 - j, n_chunks + n_ctx_chunks - 1 - j)


def _mlstm_kernel(qf_ref, kf_ref, vf_ref, gf_ref, qb_ref, kb_ref, vb_ref, gb_ref, gbias_ref,
                  yf_ref, yb_ref, c_ref, m_ref):
    @pl.when(pl.program_id(0) == 0)
    def _():
        c_ref[...] = jnp.zeros_like(c_ref)
        m_ref[...] = jnp.zeros_like(m_ref)

    lane = lax.broadcasted_iota(jnp.int32, (SCAN_L, LANES), 1)
    is_forget = jnp.bitwise_and(lane, 2 * A_HEADS - 1) >= A_HEADS
    ones = jnp.ones((SCAN_L, HEAD_DIM), jnp.bfloat16)
    nt = (((1,), (1,)), ((), ()))
    tn = (((0,), (0,)), ((), ()))
    for z, (q_ref, k_ref, v_ref, g_ref, y_ref) in enumerate(
            ((qf_ref, kf_ref, vf_ref, gf_ref, yf_ref), (qb_ref, kb_ref, vb_ref, gb_ref, yb_ref))):
        rev = z == 1
        mask = _tri_mask(rev)
        gates = g_ref[...] + gbias_ref[...]
        gates = jnp.where(is_forget, _log_sigmoid(gates), gates)
        csum = _cumsum_rows(mask.astype(jnp.bfloat16), jnp.where(is_forget, gates, 0.0))
        gates_t = gates.T
        csum_t = csum.T
        end = 0 if rev else SCAN_L - 1
        for h in range(A_HEADS):
            li, lf = z * 2 * A_HEADS + h, z * 2 * A_HEADS + A_HEADS + h
            sl = slice(h * HEAD_DIM, (h + 1) * HEAD_DIM)
            q, k, v = q_ref[:, sl], k_ref[:, sl], v_ref[:, sl]
            b_col, b_row = csum[:, lf:lf + 1], csum_t[lf:lf + 1, :]
            i_col, i_row = gates[:, li:li + 1], gates_t[li:li + 1, :]
            st = z * A_HEADS + h
            m_prev = m_ref[st][0:1, 0:1]
            dlog = jnp.where(mask, b_col - b_row + i_row, NEG_INF)
            m_inter = b_col + m_prev
            m_t = jnp.maximum(m_inter, jnp.max(dlog, axis=-1, keepdims=True))
            s = lax.dot_general(q, k, nt, preferred_element_type=jnp.float32)
            sc = (s * jnp.exp(dlog - m_t)).astype(jnp.bfloat16)
            w_inter = jnp.exp(m_inter - m_t)
            v_ext = jnp.concatenate([v, ones], axis=1)
            r = (jnp.dot(sc, v_ext, preferred_element_type=jnp.float32)
                 + w_inter * jnp.dot(q, c_ref[st].astype(jnp.bfloat16), preferred_element_type=jnp.float32))
            num, den = r[:, :HEAD_DIM], r[:, HEAD_DIM:]
            y_ref[:, sl] = num / jnp.maximum(jnp.abs(den), jnp.exp(-m_t))
            m_new = m_t[end:end + 1, :]
            b_end = b_col[end:end + 1, :]
            w_end = jnp.exp(b_end - b_col + i_col - m_new)
            decay = jnp.exp(b_end + m_prev - m_new)
            kw = (k.astype(jnp.float32) * w_end).astype(jnp.bfloat16)
            c_ref[st] = decay * c_ref[st] + lax.dot_general(kw, v_ext, tn, preferred_element_type=jnp.float32)
            m_ref[st] = jnp.broadcast_to(m_new, (8, LANES))


def _scan_specs(n_ctx_chunks, n_chunks, width, col0, rev):
    return pl.BlockSpec((SCAN_L, width),
                        lambda j: (_scan_chunk_index(j, n_ctx_chunks, n_chunks, rev), col0))


def _mlstm(qk, p, gates, gate_bias, seg):
    t = qk.shape[0]
    nck = t // SCAN_L
    ncc = seg // SCAN_L
    ins, specs = [], []
    for rev in (False, True):
        ins += [qk, qk, p, gates]
        specs += [_scan_specs(ncc, nck, A_WIDTH, 0, rev), _scan_specs(ncc, nck, A_WIDTH, 1, rev),
                  _scan_specs(ncc, nck, A_WIDTH, 2, rev), _scan_specs(ncc, nck, LANES, 0, rev)]
    return pl.pallas_call(
        _mlstm_kernel,
        grid=(nck,),
        in_specs=specs + [pl.BlockSpec((1, LANES), lambda j: (0, 0))],
        out_specs=[_scan_specs(ncc, nck, A_WIDTH, 0, False), _scan_specs(ncc, nck, A_WIDTH, 0, True)],
        out_shape=[jax.ShapeDtypeStruct((t, A_WIDTH), jnp.float32)] * 2,
        scratch_shapes=[pltpu.VMEM((N_SCAN_STATES, HEAD_DIM, 2 * HEAD_DIM), jnp.float32),
                        pltpu.VMEM((N_SCAN_STATES, 8, LANES), jnp.float32)],
        compiler_params=pltpu.CompilerParams(
            dimension_semantics=("arbitrary",), vmem_limit_bytes=VMEM_LIMIT_BYTES),
        name="mlstm_scan",
    )(*ins, gate_bias)


def _gla_kernel(qf_ref, kf_ref, vf_ref, gf_ref, qb_ref, kb_ref, vb_ref, gb_ref, w2_ref, ab_ref,
                of_ref, ob_ref, s_ref):
    @pl.when(pl.program_id(0) == 0)
    def _():
        s_ref[...] = jnp.zeros_like(s_ref)

    nsub = SCAN_L // GLA_SUB
    row = lax.broadcasted_iota(jnp.int32, (SCAN_L, LANES), 0)
    lane = lax.broadcasted_iota(jnp.int32, (SCAN_L, LANES), 1)
    head_lane = (lane < B_KEY_DIM, lane >= B_KEY_DIM)
    r_sq = lax.broadcasted_iota(jnp.int32, (SCAN_L, SCAN_L), 0)
    c_sq = lax.broadcasted_iota(jnp.int32, (SCAN_L, SCAN_L), 1)
    sub_of_row = jnp.right_shift(row, 4)
    in_sub = jnp.bitwise_and(row, GLA_SUB - 1)
    sel_r = lax.broadcasted_iota(jnp.int32, (LANES, 2 * LANES), 0)
    sel_c = lax.broadcasted_iota(jnp.int32, (LANES, 2 * LANES), 1)
    head_sum = ((sel_r < B_KEY_DIM) == (sel_c < LANES)).astype(jnp.bfloat16)
    nt = (((1,), (1,)), ((), ()))
    tn = (((0,), (0,)), ((), ()))
    for z, (q_ref, k_ref, v_ref, g_ref, o_ref) in enumerate(
            ((qf_ref, kf_ref, vf_ref, gf_ref, of_ref), (qb_ref, kb_ref, vb_ref, gb_ref, ob_ref))):
        rev = z == 1
        tri = _tri_mask(rev).astype(jnp.bfloat16)
        lr = g_ref[...].astype(jnp.bfloat16)
        end = 0 if rev else SCAN_L - 1
        for pr in range(B_HEADS // 2):
            ls = slice(pr * LANES, (pr + 1) * LANES)
            za = jnp.dot(lr, w2_ref[z, :, ls], preferred_element_type=jnp.float32) + ab_ref[z:z + 1, ls]
            b = _cumsum_rows(tri, _log_sigmoid(za) * (1.0 / GLA_TAU))
            q = q_ref[:, ls].astype(jnp.float32) * (B_KEY_DIM ** -0.5)
            k = k_ref[:, ls].astype(jnp.float32)
            v = v_ref[:, 2 * pr * HEAD_DIM:(2 * pr + 2) * HEAD_DIM]
            b_ref_rows = jnp.zeros_like(b)
            for j in range(nsub):
                e = j * GLA_SUB if rev else j * GLA_SUB + GLA_SUB - 1
                b_ref_rows = jnp.where(sub_of_row == j, b[e:e + 1, :], b_ref_rows)
            k_sub = (k * jnp.exp(b_ref_rows - b)).astype(jnp.bfloat16)
            k_sub_h = [jnp.where(head_lane[hh], k_sub, jnp.zeros_like(k_sub)) for hh in range(2)]
            attn =[jnp.zeros((SCAN_L, SCAN_L), jnp.float32), jnp.zeros((SCAN_L, SCAN_L), jnp.float32)]
            for j in range(nsub):
                e = j * GLA_SUB if rev else j * GLA_SUB + GLA_SUB - 1
                if j == (0 if rev else nsub - 1):
                    continue
                later = (sub_of_row < j) if rev else (sub_of_row > j)
                qj = (q * jnp.exp(jnp.where(later, b - b[e:e + 1, :], NEG_INF))).astype(jnp.bfloat16)
                for hh in range(2):
                    kj = jnp.where(sub_of_row == j, k_sub_h[hh], jnp.zeros_like(k_sub))
                    attn[hh] = attn[hh] + lax.dot_general(qj, kj, nt, preferred_element_type=jnp.float32)
            for d in range(GLA_SUB):
                shift = (SCAN_L - d) % SCAN_L if rev else d
                k_s = pltpu.roll(k, shift, 0) if d else k
                b_s = pltpu.roll(b, shift, 0) if d else b
                ok_row = (in_sub <= GLA_SUB - 1 - d) if rev else (in_sub >= d)
                f = (q * k_s * jnp.exp(jnp.where(ok_row, b - b_s, NEG_INF))).astype(jnp.bfloat16)
                red = jnp.dot(f, head_sum, preferred_element_type=jnp.float32)
                on_diag = (c_sq == r_sq + d) if rev else (c_sq == r_sq - d)
                for hh in range(2):
                    attn[hh] = attn[hh] + jnp.where(on_diag, red[:, hh * LANES:(hh + 1) * LANES], 0.0)
            q_in = q * jnp.exp(b)
            b_end = b[end:end + 1, :]
            k_out = (k * jnp.exp(b_end - b)).astype(jnp.bfloat16)
            st = z * (B_HEADS // 2) + pr
            s_t = s_ref[st]
            s_bf = s_t.astype(jnp.bfloat16)
            for hh in range(2):
                vs = slice(hh * HEAD_DIM, (hh + 1) * HEAD_DIM)
                qh = jnp.where(head_lane[hh], q_in, 0.0).astype(jnp.bfloat16)
                o = (jnp.dot(attn[hh].astype(jnp.bfloat16), v[:, vs], preferred_element_type=jnp.float32)
                     + lax.dot_general(qh, s_bf[vs, :], nt, preferred_element_type=jnp.float32))
                o_ref[:, (2 * pr + hh) * HEAD_DIM:(2 * pr + hh + 1) * HEAD_DIM] = o
            s_ref[st] = s_t * jnp.exp(b_end) + lax.dot_general(v, k_out, tn, preferred_element_type=jnp.float32)


def _gla(p, gates, w2, alpha_b, seg):
    t = p.shape[0]
    nck = t // SCAN_L
    ncc = seg // SCAN_L
    ins, specs = [], []
    for rev in (False, True):
        ins += [p, p, p, gates]
        specs += [_scan_specs(ncc, nck, B_KEY_WIDTH, P_BQ // B_KEY_WIDTH, rev),
                  _scan_specs(ncc, nck, B_KEY_WIDTH, P_BK // B_KEY_WIDTH, rev),
                  _scan_specs(ncc, nck, B_VAL_WIDTH, P_BV // B_VAL_WIDTH, rev),
                  _scan_specs(ncc, nck, LANES, 0, rev)]
    return pl.pallas_call(
        _gla_kernel,
        grid=(nck,),
        in_specs=specs + [pl.BlockSpec((2, LANES, B_KEY_WIDTH), lambda j: (0, 0, 0)),
                          pl.BlockSpec((2, B_KEY_WIDTH), lambda j: (0, 0))],
        out_specs=[_scan_specs(ncc, nck, B_VAL_WIDTH, 0, False), _scan_specs(ncc, nck, B_VAL_WIDTH, 0, True)],
        out_shape=[jax.ShapeDtypeStruct((t, B_VAL_WIDTH), jnp.float32)] * 2,
        scratch_shapes=[pltpu.VMEM((B_HEADS, 2 * HEAD_DIM, LANES), jnp.float32)],
        compiler_params=pltpu.CompilerParams(
            dimension_semantics=("arbitrary",), vmem_limit_bytes=VMEM_LIMIT_BYTES),
        name="gla_scan",
    )(*ins, w2, alpha_b)


def _head_out_kernel(af_ref, ab_ref, bf_ref, bb_ref, gate_ref, g_ref, o_ref):
    is_gla = pl.program_id(1) >= A_HEADS
    y = jnp.where(is_gla, bf_ref[...] + bb_ref[...], af_ref[...] + ab_ref[...])
    yn = y * lax.rsqrt(jnp.mean(y * y, axis=-1, keepdims=True) + EPS) * g_ref[0]
    gate = gate_ref[...].astype(jnp.float32)
    o_ref[...] = (yn * jax.nn.sigmoid(gate) * jnp.where(is_gla, gate, 1.0)).astype(o_ref.dtype)


def _head_out(ya, yb, p, head_g):
    t = p.shape[0]
    bt = _row_tile(t)
    nh = A_HEADS + B_HEADS
    a_spec = pl.BlockSpec((bt, HEAD_DIM), lambda i, j: (i, jnp.minimum(j, A_HEADS - 1)))
    b_spec = pl.BlockSpec((bt, HEAD_DIM), lambda i, j: (i, jnp.maximum(j - A_HEADS, 0)))

    def gate_col(i, j):
        return (i, jnp.where(j < A_HEADS, P_AO + j, P_BR + j - A_HEADS))

    return pl.pallas_call(
        _head_out_kernel,
        grid=(t // bt, nh),
        in_specs=[a_spec, a_spec, b_spec, b_spec,
                  pl.BlockSpec((bt, HEAD_DIM), gate_col),
                  pl.BlockSpec((1, 1, HEAD_DIM), lambda i, j: (j // A_HEADS, 0, 0))],
        out_specs=pl.BlockSpec((bt, HEAD_DIM), lambda i, j: (i, j)),
        out_shape=jax.ShapeDtypeStruct((t, nh * HEAD_DIM), jnp.bfloat16),
        compiler_params=pltpu.CompilerParams(
            dimension_semantics=("parallel", "arbitrary"), vmem_limit_bytes=VMEM_LIMIT_BYTES),
        name="head_out",
    )(ya[0], ya[1], yb[0], yb[1], p, head_g.astype(jnp.float32)[:, None, :])


P_AO = 24
P_BQ, P_BK, P_BV = 4096, 4608, 5120
P_BR = 48


def _mixer_ab(h, w_in, conv_w, conv_b, gate_b, alpha_w2, alpha_b, head_g, w_out, seg):
    bf = jnp.bfloat16
    cols = np.cumsum((0,) + AB_SPLITS)
    pick = lambda *ids: jnp.concatenate([w_in[:, cols[i]:cols[i + 1]] for i in ids], axis=1)
    w_main = pick(0, 1, 2, 3, 5, 6, 7, 8).astype(bf)
    w_gate = jnp.pad(pick(4, 9), ((0, 0), (0, LANES - 4 * A_HEADS - 2 * GLA_RANK))).astype(bf)
    hb = h.astype(bf)
    p = _matmul(hb, w_main, bf)
    gates = _matmul(hb, w_gate, jnp.float32)
    gate_bias = jnp.pad(gate_b.reshape(1, -1).astype(jnp.float32), ((0, 0), (0, LANES - 4 * A_HEADS)))
    w2 = jnp.zeros((2, LANES, B_KEY_WIDTH), jnp.float32)
    for z in range(2):
        w2 = w2.at[z, GATE_LR_LANE + z * GLA_RANK:GATE_LR_LANE + (z + 1) * GLA_RANK].set(alpha_w2[z])
    qk = _short_conv_silu(p, conv_w, conv_b, seg)
    ya = _mlstm(qk, p, gates, gate_bias, seg)
    yb = _gla(p, gates, w2.astype(bf), alpha_b.astype(jnp.float32), seg)
    return _matmul(_head_out(ya, yb, p, head_g), w_out)


def _rmsnorm(x, g):
    xf = x.astype(jnp.float32)
    y = xf * lax.rsqrt(jnp.mean(xf * xf, axis=-1, keepdims=True) + EPS)
    return (y * g.astype(jnp.float32)).astype(x.dtype)


def _modulate(x, g, shift, scale):
    return _rmsnorm(x, g) * (1 + scale) + shift


def kernel(x, c, ctx, c_ctx, mod_w, mod_b, norm_g, ab_w_in, ab_conv_w, ab_conv_b, ab_gate_b, ab_alpha_w2,
           ab_alpha_b, ab_head_g, ab_w_out, cd_w_in, cd_qk_g, cd_rpb, cd_sink, cd_w_out, peer_w_q,
           peer_sub_keys, peer_u, peer_v):
    assert DEPTH == 2
    bf = jnp.bfloat16
    m = ctx.shape[1]
    xs = jnp.concatenate([ctx[0], x[0]], axis=0)
    is_ctx = (jnp.arange(xs.shape[0]) < m)[:, None]
    ub = _cast_bf16(peer_u)
    vb = _cast_bf16(peer_v)
    for l in range(DEPTH):
        j = l // 2
        mod_l = jnp.split(jax.nn.silu(c) @ mod_w[l] + mod_b[l], 6, axis=-1)
        mod_c = jnp.split((jax.nn.silu(c_ctx) @ mod_w[l] + mod_b[l])[None, :], 6, axis=-1)
        wq = peer_w_q[l].astype(bf)
        keys = peer_sub_keys[l].astype(bf)
        if l % 2 == 0:
            mod = [jnp.where(is_ctx, mc, ml) for mc, ml in zip(mod_c, mod_l)]
            h = _modulate(xs, norm_g[l, 0], mod[0], mod[1])
            y = _mixer_ab(h, ab_w_in[j], ab_conv_w[j], ab_conv_b[j], ab_gate_b[j], ab_alpha_w2[j],
                          ab_alpha_b[j], ab_head_g[j], ab_w_out[j].astype(bf), m)
            xs = xs + mod[2] * y
            h = _modulate(xs, norm_g[l, 1], mod[3], mod[4])
            xs = xs + mod[5] * _peer(h, wq, keys, ub, vb, l)
        else:
            xc, xl = xs[:m], xs[m:]
            hc = _modulate(xc, norm_g[l, 0], mod_c[0], mod_c[1])
            hl = _modulate(xl, norm_g[l, 0], mod_l[0], mod_l[1])
            y = _mixer_cd(hc, hl, cd_w_in[j].astype(bf), cd_qk_g[j], cd_rpb[j], cd_sink[j], cd_w_out[j].astype(bf))
            xl = xl + mod_l[2] * y
            h = _modulate(xl, norm_g[l, 1], mod_l[3], mod_l[4])
            xs = xl + mod_l[5] * _peer(h, wq, keys, ub, vb, l)
    return xs[None]
```

```python
import functools

import numpy as np
import jax
import jax.numpy as jnp
from jax import lax
from jax.experimental import pallas as pl
from jax.experimental.pallas import tpu as pltpu

D_MODEL = 2048
SEQ = 8192
DEPTH = 2
GRID_W = 64
CTX_LEN = 256
HEAD_DIM = 128
N_GROUP_HEADS = 8
EPS = 1e-6
A_HEADS = 8
A_WIDTH = 1024
CONV_W = 3
B_HEADS = 8
B_KEY_DIM = 64
B_KEY_WIDTH = 512
B_VAL_WIDTH = 1024
GLA_RANK = 16
GLA_TAU = 16.0
SCAN_CHUNK = 64
C_HEADS = 8
C_WIDTH = 1024
WIN_H = 8
WIN_W = 16
D_HEADS = 8
D_KV_HEADS = 2
D_GROUP = 4
D_WIDTH = 1024
D_KV_WIDTH = 256
WINDOW = 128
ROPE_BASE = 10000.0
PEER_HEADS = 8
PEER_NKEYS = 128
PEER_EXPERTS = PEER_NKEYS * PEER_NKEYS
PEER_QDIM = 256
PEER_TOPK = 16
PEER_SLOTS = PEER_HEADS * PEER_TOPK
AB_SPLITS = (A_WIDTH, A_WIDTH, A_WIDTH, A_WIDTH, 4 * A_HEADS,
             B_KEY_WIDTH, B_KEY_WIDTH, B_VAL_WIDTH, B_VAL_WIDTH, 2 * GLA_RANK)
CD_SPLITS = (C_WIDTH, C_WIDTH, D_KV_WIDTH, D_KV_WIDTH, C_WIDTH, D_WIDTH)
CD_KV_COLS = sum(CD_SPLITS[:4])

VMEM_LIMIT_BYTES = 56 * 1024 * 1024
LANES = 128


def _row_tile(m):
    for t in (1024, 768, 512, 256, 128):
        if m % t == 0:
            return t
    raise ValueError(f"unsupported row count {m}")


def _mm_kernel(x_ref, w_ref, o_ref):
    o_ref[...] = jnp.dot(x_ref[...], w_ref[...], preferred_element_type=jnp.float32).astype(o_ref.dtype)


def _matmul(x, w, out_dtype=jnp.float32):
    m, k = x.shape
    n = w.shape[1]
    bn = min(512, n)
    n_pad = -(-n // bn) * bn
    if n_pad != n:
        w = jnp.pad(w, ((0, 0), (0, n_pad - n)))
    bm = _row_tile(m)
    out = pl.pallas_call(
        _mm_kernel,
        grid=(m // bm, n_pad // bn),
        in_specs=[pl.BlockSpec((bm, k), lambda i, j: (i, 0)),
                  pl.BlockSpec((k, bn), lambda i, j: (0, j))],
        out_specs=pl.BlockSpec((bm, bn), lambda i, j: (i, j)),
        out_shape=jax.ShapeDtypeStruct((m, n_pad), out_dtype),
        compiler_params=pltpu.CompilerParams(
            dimension_semantics=("parallel", "arbitrary"), vmem_limit_bytes=VMEM_LIMIT_BYTES),
        name="matmul",
    )(x.astype(jnp.bfloat16), w.astype(jnp.bfloat16))
    return out[:, :n] if n_pad != n else out


ROUTE_TOKENS = (768, 512, 256, 128)
GATE_TOKENS = 128
NEG_INF = float("-inf")


def _top16(s, key, big):
    slot = lax.broadcasted_iota(jnp.int32, (PEER_TOPK, s.shape[1]), 0)
    vals = jnp.zeros((PEER_TOPK, s.shape[1]), jnp.float32)
    keys = jnp.zeros((PEER_TOPK, s.shape[1]), jnp.float32)
    for it in range(PEER_TOPK):
        m = jnp.max(s, axis=0, keepdims=True)
        k = jnp.min(jnp.where(s == m, key, big), axis=0, keepdims=True)
        vals = jnp.where(slot == it, m, vals)
        keys = jnp.where(slot == it, k, keys)
        s = jnp.where(key == k, NEG_INF, s)
    return vals, keys


def _route_kernel(h_ref, wq_ref, k_ref, i1_ref, i2_ref, g_ref):
    bt = h_ref.shape[0]
    q = jnp.dot(h_ref[...], wq_ref[...], preferred_element_type=jnp.float32).astype(jnp.bfloat16)
    nt = (((1,), (1,)), ((), ()))
    s1 = lax.dot_general(k_ref[0, 0], q[:, :PEER_NKEYS], nt, preferred_element_type=jnp.float32)
    s2 = lax.dot_general(k_ref[0, 1], q[:, PEER_NKEYS:], nt, preferred_element_type=jnp.float32)
    row = lax.broadcasted_iota(jnp.int32, (PEER_NKEYS, bt), 0).astype(jnp.float32)
    v1, r1 = _top16(s1, row, float(PEER_NKEYS))
    v2, r2 = _top16(s2, row, float(PEER_NKEYS))
    i16 = lax.broadcasted_iota(jnp.int32, (16, bt), 0).astype(jnp.float32)
    i8 = lax.broadcasted_iota(jnp.int32, (8, bt), 0).astype(jnp.float32)
    cand = [v1[0:1] + v2]
    flat = [i16]
    for a in range(1, 8):
        cand.append(v1[a:a + 1] + v2[0:8])
        flat.append(i8 + float(a * PEER_TOPK))
    cand.append(v1[8:16] + v2[0:1])
    flat.append((i8 + 8.0) * float(PEER_TOPK))
    best, key = _top16(jnp.concatenate(cand, axis=0), jnp.concatenate(flat, axis=0), float(PEER_TOPK * PEER_TOPK))
    key = key.astype(jnp.int32)
    a_sel = jnp.right_shift(key, 4)
    b_sel = jnp.bitwise_and(key, PEER_TOPK - 1)
    e1 = jnp.zeros_like(r1)
    e2 = jnp.zeros_like(r2)
    for r in range(PEER_TOPK):
        e1 = jnp.where(a_sel == r, r1[r:r + 1], e1)
        e2 = jnp.where(b_sel == r, r2[r:r + 1], e2)
    e1 = e1.astype(jnp.int32)
    e2 = e2.astype(jnp.int32)
    ex = jnp.exp(best - best[0:1])
    i1_ref[...] = e1
    i2_ref[...] = e2
    g_ref[...] = ex / jnp.sum(ex, axis=0, keepdims=True)


def _route(hb, wq, keys):
    t, d = hb.shape
    bt = next(b for b in ROUTE_TOKENS if t % b == 0)
    slot_spec = pl.BlockSpec((PEER_TOPK, bt), lambda i, h: (h, i))
    return pl.pallas_call(
        _route_kernel,
        grid=(t // bt, PEER_HEADS),
        in_specs=[pl.BlockSpec((bt, d), lambda i, h: (i, 0)),
                  pl.BlockSpec((d, PEER_QDIM), lambda i, h: (0, h)),
                  pl.BlockSpec((1, 2, PEER_NKEYS, PEER_QDIM // 2), lambda i, h: (h, 0, 0, 0))],
        out_specs=[slot_spec, slot_spec, slot_spec],
        out_shape=[jax.ShapeDtypeStruct((PEER_SLOTS, t), jnp.int32),
                   jax.ShapeDtypeStruct((PEER_SLOTS, t), jnp.int32),
                   jax.ShapeDtypeStruct((PEER_SLOTS, t), jnp.float32)],
        compiler_params=pltpu.CompilerParams(
            dimension_semantics=("parallel", "arbitrary"), vmem_limit_bytes=VMEM_LIMIT_BYTES),
        name="peer_route",
    )(hb, wq, keys)


GATE_GROUP = 16


def _gate_matrix_kernel(i1_ref, i2_ref, g_ref, o_ref, i1_t, g_t, stage):
    tb = o_ref.shape[1]
    i1_t[...] = i1_ref[...].T
    g_t[...] = g_ref[...].T
    row_id = lax.broadcasted_iota(jnp.int32, (PEER_NKEYS, PEER_SLOTS), 0)
    lane_id = lax.broadcasted_iota(jnp.int32, (PEER_SLOTS, PEER_NKEYS), 1)
    zero = jnp.zeros((PEER_NKEYS, PEER_SLOTS), jnp.bfloat16)

    for tp in range(tb // 2):
        ta = 2 * tp
        tb_ = ta + 1
        ca = jnp.where(row_id == i1_t[ta:ta + 1, :], g_t[ta:ta + 1, :], 0.0).astype(jnp.bfloat16)
        cb = jnp.where(row_id == i1_t[tb_:tb_ + 1, :], g_t[tb_:tb_ + 1, :], 0.0).astype(jnp.bfloat16)
        oa = jnp.where(i2_ref[:, ta:ta + 1] == lane_id, 1.0, 0.0).astype(jnp.bfloat16)
        ob = jnp.where(i2_ref[:, tb_:tb_ + 1] == lane_id, 1.0, 0.0).astype(jnp.bfloat16)
        lhs = jnp.concatenate([ca, cb], axis=1)
        rhs = jnp.concatenate([jnp.concatenate([oa, zero], axis=1),
                               jnp.concatenate([zero, ob], axis=1)], axis=0)
        out = jnp.dot(lhs, rhs, preferred_element_type=jnp.float32)
        stage[ta % GATE_GROUP] = out[:, :PEER_NKEYS]
        stage[tb_ % GATE_GROUP] = out[:, PEER_NKEYS:]
        if tb_ % GATE_GROUP == GATE_GROUP - 1:
            t0 = tb_ + 1 - GATE_GROUP
            o_ref[:, t0:t0 + GATE_GROUP, :] = jnp.swapaxes(stage[...], 0, 1).astype(o_ref.dtype)


def _gate_matrix(i1, i2, g):
    t = i1.shape[1]
    tb = GATE_TOKENS
    return pl.pallas_call(
        _gate_matrix_kernel,
        grid=(t // tb,),
        in_specs=[pl.BlockSpec((PEER_SLOTS, tb), lambda i: (0, i))] * 3,
        out_specs=pl.BlockSpec((PEER_NKEYS, tb, PEER_NKEYS), lambda i: (0, i, 0)),
        out_shape=jax.ShapeDtypeStruct((PEER_NKEYS, t, PEER_NKEYS), jnp.bfloat16),
        scratch_shapes=[pltpu.VMEM((tb, PEER_SLOTS), jnp.int32),
                        pltpu.VMEM((tb, PEER_SLOTS), jnp.float32),
                        pltpu.VMEM((GATE_GROUP, PEER_NKEYS, PEER_NKEYS), jnp.float32)],
        compiler_params=pltpu.CompilerParams(
            dimension_semantics=("parallel",), vmem_limit_bytes=VMEM_LIMIT_BYTES),
        name="peer_gate_matrix",
    )(i1, i2, g)


def _gelu_tanh(x):
    return 0.5 * x * (1.0 + jnp.tanh(0.7978845608028654 * (x + 0.044715 * x * x * x)))


def _cast_kernel(x_ref, o_ref):
    o_ref[...] = x_ref[...].astype(o_ref.dtype)


def _cast_bf16(x):
    nl, r, c = x.shape
    br = 1024
    return pl.pallas_call(
        _cast_kernel,
        grid=(nl, r // br),
        in_specs=[pl.BlockSpec((1, br, c), lambda l, i: (l, i, 0))],
        out_specs=pl.BlockSpec((1, br, c), lambda l, i: (l, i, 0)),
        out_shape=jax.ShapeDtypeStruct(x.shape, jnp.bfloat16),
        compiler_params=pltpu.CompilerParams(
            dimension_semantics=("parallel", "parallel"), vmem_limit_bytes=VMEM_LIMIT_BYTES),
        name="cast_bf16",
    )(x)


def _peer_dense_kernel(h_ref, u_ref, v_ref, g_ref, o_ref):
    j = pl.program_id(1)
    nt = (((1,), (1,)), ((), ()))
    act = _gelu_tanh(lax.dot_general(h_ref[...], u_ref[0], nt, preferred_element_type=jnp.float32))
    gate = jnp.concatenate([g_ref[r] for r in range(g_ref.shape[0])], axis=1)
    w = (act * gate.astype(jnp.float32)).astype(jnp.bfloat16)
    contrib = jnp.dot(w, v_ref[0], preferred_element_type=jnp.float32)

    @pl.when(j == 0)
    def _():
        o_ref[...] = contrib

    @pl.when(j != 0)
    def _():
        o_ref[...] += contrib


def _peer_dense(h, u, v, gmat, layer):
    t, d = h.shape
    e = v.shape[1]
    bm = _row_tile(t)
    be = 1024
    return pl.pallas_call(
        _peer_dense_kernel,
        grid=(t // bm, e // be),
        in_specs=[pl.BlockSpec((bm, d), lambda i, j: (i, 0)),
                  pl.BlockSpec((1, be, d), lambda i, j: (layer, j, 0)),
                  pl.BlockSpec((1, be, d), lambda i, j: (layer, j, 0)),
                  pl.BlockSpec((be // PEER_NKEYS, bm, PEER_NKEYS), lambda i, j: (j, i, 0))],
        out_specs=pl.BlockSpec((bm, d), lambda i, j: (i, 0)),
        out_shape=jax.ShapeDtypeStruct((t, d), jnp.float32),
        compiler_params=pltpu.CompilerParams(
            dimension_semantics=("parallel", "arbitrary"), vmem_limit_bytes=VMEM_LIMIT_BYTES),
        name="peer_dense",
    )(h, u, v, gmat)


def _peer(h, wq, keys, u, v, layer):
    hb = h
    i1, i2, g = _route(hb, wq, keys)
    return _peer_dense(hb, u, v, _gate_matrix(i1, i2, g), layer)


ROPE_HALF = 32
NA_QROWS = 8
NA_KROWS = 16
SWA_QB = 256
SWA_KB = SWA_QB + 2 * WINDOW

P_CK, P_CV, P_DK, P_DV, P_CQ, P_DQ = 0, 8, 16, 18, 20, 28
QK_CK, QK_DK, QK_CQ, QK_DQ = 0, 8, 10, 20


def _rope_tables(n):
    t = np.arange(n)
    freqs = (np.float32(ROPE_BASE) ** (-np.arange(ROPE_HALF, dtype=np.float32) / np.float32(ROPE_HALF))).astype(np.float32)
    ang_r = (t // GRID_W).astype(np.float32)[:, None] * freqs
    ang_c = (t % GRID_W).astype(np.float32)[:, None] * freqs
    cos = np.concatenate([np.cos(ang_r), np.cos(ang_r), np.cos(ang_c), np.cos(ang_c)], axis=-1)
    sin = np.concatenate([-np.sin(ang_r), np.sin(ang_r), -np.sin(ang_c), np.sin(ang_c)], axis=-1)
    return cos.astype(np.float32), sin.astype(np.float32)


def _qk_prep_kernel(tbl_ref, p_ref, g_ref, cos_ref, sin_ref, o_ref):
    del tbl_ref
    x = p_ref[...].astype(jnp.float32)
    y = x * lax.rsqrt(jnp.mean(x * x, axis=-1, keepdims=True) + EPS) * g_ref[0]
    lane = lax.broadcasted_iota(jnp.int32, y.shape, 1)
    first = jnp.bitwise_and(lane, 2 * ROPE_HALF - 1) < ROPE_HALF
    partner = jnp.where(first, pltpu.roll(y, LANES - ROPE_HALF, 1), pltpu.roll(y, ROPE_HALF, 1))
    o_ref[...] = (y * cos_ref[0] + partner * sin_ref[0]).astype(o_ref.dtype)


def _qk_prep(p, table, gains, cos, sin):
    n = p.shape[0]
    ncol = table.shape[1]
    bt = 2048 if n % 2048 == 0 else n
    grid_spec = pltpu.PrefetchScalarGridSpec(
        num_scalar_prefetch=1,
        grid=(n // bt, ncol),
        in_specs=[pl.BlockSpec((bt, HEAD_DIM), lambda i, j, tbl: (i, tbl[0, j])),
                  pl.BlockSpec((1, 1, HEAD_DIM), lambda i, j, tbl: (tbl[1, j], 0, 0)),
                  pl.BlockSpec((1, bt, HEAD_DIM), lambda i, j, tbl: (tbl[2, j], i, 0)),
                  pl.BlockSpec((1, bt, HEAD_DIM), lambda i, j, tbl: (tbl[2, j], i, 0))],
        out_specs=pl.BlockSpec((bt, HEAD_DIM), lambda i, j, tbl: (i, j)),
    )
    return pl.pallas_call(
        _qk_prep_kernel,
        grid_spec=grid_spec,
        out_shape=jax.ShapeDtypeStruct((n, ncol * HEAD_DIM), jnp.bfloat16),
        compiler_params=pltpu.CompilerParams(
            dimension_semantics=("parallel", "arbitrary"), vmem_limit_bytes=VMEM_LIMIT_BYTES),
        name="qk_prep",
    )(table, p, gains, cos, sin)


def _na_bias(rpb, n):
    rows = n // GRID_W
    nr, nc = 2 * WIN_H - 1, 2 * WIN_W - 1
    c = np.arange(GRID_W)[:, None]
    kc = np.arange(GRID_W)[None, :]
    cs = np.clip(c - WIN_W // 2, 0, GRID_W - WIN_W)
    col_ok = (kc >= cs) & (kc < cs + WIN_W)
    oh_col = (np.clip(kc - c + WIN_W - 1, 0, nc - 1)[None] == np.arange(nc)[:, None, None]).astype(np.float32)
    oh_row, ok = [], []
    for r0, rlo in ((NA_QROWS, NA_QROWS - 4), (0, 0), (rows - NA_QROWS, rows - NA_KROWS)):
        r = r0 + np.arange(NA_QROWS)[:, None]
        kr = rlo + np.arange(NA_KROWS)[None, :]
        rs = np.clip(r - WIN_H // 2, 0, rows - WIN_H)
        row_ok = (kr >= rs) & (kr < rs + WIN_H)
        oh_row.append((np.clip(kr - r + WIN_H - 1, 0, nr - 1)[..., None] == np.arange(nr)).astype(np.float32))
        ok.append((row_ok[:, None, :, None] & col_ok[None, :, None, :]).reshape(NA_QROWS * GRID_W, NA_KROWS * GRID_W))
    hp = lax.Precision.HIGHEST
    colb = jnp.einsum('hab,bcd->hacd', rpb.astype(jnp.float32), jnp.asarray(oh_col), precision=hp)
    bias = jnp.einsum('tqka,hacd->htqckd', jnp.asarray(np.stack(oh_row)), colb, precision=hp)
    bias = bias.reshape(rpb.shape[0], 3, NA_QROWS * GRID_W, NA_KROWS * GRID_W)
    return jnp.where(jnp.asarray(np.stack(ok))[None], bias, NEG_INF)


def _softmax_pv(parts, extra_logit=None):
    m = parts[0][0].max(axis=-1, keepdims=True)
    for s, _ in parts[1:]:
        m = jnp.maximum(m, s.max(axis=-1, keepdims=True))
    if extra_logit is not None:
        m = jnp.maximum(m, extra_logit)
    den = jnp.exp(extra_logit - m) if extra_logit is not None else 0.0
    acc = None
    for s, v in parts:
        p = jnp.exp(s - m)
        den = den + p.sum(axis=-1, keepdims=True)
        pv = jnp.dot(p.astype(jnp.bfloat16), v, preferred_element_type=jnp.float32)
        acc = pv if acc is None else acc + pv
    return acc / den


def _na_kernel(q_ref, k_ref, v_ref, kc_ref, vc_ref, bias_ref, o_ref):
    n = q_ref.shape[0]
    qb = NA_QROWS * GRID_W
    kb = NA_KROWS * GRID_W
    nblk = n // qb
    nt = (((1,), (1,)), ((), ()))
    kc = kc_ref[...]
    vc = vc_ref[...]

    def body(b, carry):
        q0 = pl.multiple_of(b * qb, qb)
        ks = pl.multiple_of(jnp.clip(b * qb - (NA_KROWS - NA_QROWS) // 2 * GRID_W, 0, n - kb), 2 * LANES)
        kind = jnp.where(b == 0, 1, jnp.where(b == nblk - 1, 2, 0))
        q = q_ref[pl.ds(q0, qb), :]
        s_w = lax.dot_general(q, k_ref[pl.ds(ks, kb), :], nt, preferred_element_type=jnp.float32) + bias_ref[0, kind]
        s_c = lax.dot_general(q, kc, nt, preferred_element_type=jnp.float32)
        o = _softmax_pv([(s_w, v_ref[pl.ds(ks, kb), :]), (s_c, vc)])
        o_ref[pl.ds(q0, qb), :] = o.astype(o_ref.dtype)
        return carry

    lax.fori_loop(0, nblk, body, 0)


def _neighbourhood_attn(qk, p, qk_c, p_c, bias):
    n = qk.shape[0]
    m = qk_c.shape[0]

    def full(col0):
        return pl.BlockSpec((n, HEAD_DIM), lambda h: (0, col0 + h))

    def ctx(col0):
        return pl.BlockSpec((m, HEAD_DIM), lambda h: (0, col0 + h))

    return pl.pallas_call(
        _na_kernel,
        grid=(C_HEADS,),
        in_specs=[full(QK_CQ), full(QK_CK), full(P_CV), ctx(QK_CK), ctx(P_CV),
                  pl.BlockSpec((1, 3, NA_QROWS * GRID_W, NA_KROWS * GRID_W), lambda h: (h, 0, 0, 0))],
        out_specs=pl.BlockSpec((n, HEAD_DIM), lambda h: (0, h)),
        out_shape=jax.ShapeDtypeStruct((n, C_WIDTH), jnp.bfloat16),
        compiler_params=pltpu.CompilerParams(
            dimension_semantics=("parallel",), vmem_limit_bytes=VMEM_LIMIT_BYTES),
        name="neighbourhood_attn",
    )(qk, qk, p, qk_c, p_c, bias)


def _swa_kernel(q_ref, k_ref, v_ref, kc_ref, vc_ref, sink_ref, o_ref):
    n = q_ref.shape[0]
    nblk = n // SWA_QB
    nt = (((1,), (1,)), ((), ()))
    kc = kc_ref[...]
    vc = vc_ref[...]
    kv = pl.program_id(0)
    sink = jnp.concatenate(
        [jnp.broadcast_to(sink_ref[pl.ds(kv * D_GROUP + g, 1), 0:1], (SWA_QB, 1)) for g in range(D_GROUP)], axis=0)
    q_off = jnp.bitwise_and(lax.broadcasted_iota(jnp.int32, (D_GROUP * SWA_QB, SWA_KB), 0), SWA_QB - 1)
    k_off = lax.broadcasted_iota(jnp.int32, (D_GROUP * SWA_QB, SWA_KB), 1)
    rel = q_off - k_off

    def body(b, carry):
        q0 = pl.multiple_of(b * SWA_QB, SWA_QB)
        ks = pl.multiple_of(jnp.clip(q0 - WINDOW, 0, n - SWA_KB), LANES)
        q = jnp.concatenate([q_ref[pl.ds(q0, SWA_QB), g * HEAD_DIM:(g + 1) * HEAD_DIM] for g in range(D_GROUP)], axis=0)
        s_w = lax.dot_general(q, k_ref[pl.ds(ks, SWA_KB), :], nt, preferred_element_type=jnp.float32)
        s_w = jnp.where(jnp.abs(rel + (q0 - ks)) <= WINDOW, s_w, NEG_INF)
        s_c = lax.dot_general(q, kc, nt, preferred_element_type=jnp.float32)
        o = _softmax_pv([(s_w, v_ref[pl.ds(ks, SWA_KB), :]), (s_c, vc)], extra_logit=sink)
        for g in range(D_GROUP):
            o_ref[pl.ds(q0, SWA_QB), g * HEAD_DIM:(g + 1) * HEAD_DIM] = o[g * SWA_QB:(g + 1) * SWA_QB].astype(o_ref.dtype)
        return carry

    lax.fori_loop(0, nblk, body, 0)


def _window_attn(qk, p, qk_c, p_c, sink_rows):
    n = qk.shape[0]
    m = qk_c.shape[0]
    gw = D_GROUP * HEAD_DIM
    return pl.pallas_call(
        _swa_kernel,
        grid=(D_KV_HEADS,),
        in_specs=[pl.BlockSpec((n, gw), lambda kv: (0, QK_DQ // D_GROUP + kv)),
                  pl.BlockSpec((n, HEAD_DIM), lambda kv: (0, QK_DK + kv)),
                  pl.BlockSpec((n, HEAD_DIM), lambda kv: (0, P_DV + kv)),
                  pl.BlockSpec((m, HEAD_DIM), lambda kv: (0, QK_DK + kv)),
                  pl.BlockSpec((m, HEAD_DIM), lambda kv: (0, P_DV + kv)),
                  pl.BlockSpec((D_HEADS, LANES), lambda kv: (0, 0))],
        out_specs=pl.BlockSpec((n, gw), lambda kv: (0, kv)),
        out_shape=jax.ShapeDtypeStruct((n, D_WIDTH), jnp.bfloat16),
        compiler_params=pltpu.CompilerParams(
            dimension_semantics=("parallel",), vmem_limit_bytes=VMEM_LIMIT_BYTES),
        name="window_attn",
    )(qk, qk, p, qk_c, p_c, sink_rows)


def _mixer_cd(h_c, h_l, w_in, qk_g, rpb, sink, w_out):
    n = h_l.shape[0]
    m = h_c.shape[0]
    scale = HEAD_DIM ** -0.5
    p_l = _matmul(h_l, w_in, jnp.bfloat16)
    p_c = _matmul(h_c, w_in[:, :CD_KV_COLS], jnp.bfloat16)
    gains = jnp.stack([qk_g[0] * scale, qk_g[1], qk_g[2] * scale, qk_g[3]]).astype(jnp.float32)[:, None, :]
    cos, sin = _rope_tables(n)
    cs_l = (np.stack([np.ones_like(cos), cos]), np.stack([np.zeros_like(sin), sin]))
    cs_c = (np.ones((1, m, HEAD_DIM), np.float32), np.zeros((1, m, HEAD_DIM), np.float32))
    src = list(range(P_CK, P_CK + 8)) + [P_DK, P_DK + 1] + list(range(P_CQ, P_CQ + 8)) + [P_CQ, P_CQ] + list(range(P_DQ, P_DQ + 8))
    gain = [1] * 8 + [3] * 2 + [0] * 8 + [0, 0] + [2] * 8
    rope = [0] * 8 + [1] * 2 + [0] * 8 + [0, 0] + [1] * 8
    tbl_l = jnp.asarray(np.array([src, gain, rope], np.int32))
    tbl_c = jnp.asarray(np.array([src[:10], gain[:10], [0] * 10], np.int32))
    qk_l = _qk_prep(p_l, tbl_l, gains, jnp.asarray(cs_l[0]), jnp.asarray(cs_l[1]))
    qk_c = _qk_prep(p_c, tbl_c, gains, jnp.asarray(cs_c[0]), jnp.asarray(cs_c[1]))
    c_out = _neighbourhood_attn(qk_l, p_l, qk_c, p_c, _na_bias(rpb, n))
    sink_rows = jnp.broadcast_to(sink.astype(jnp.float32)[:, None], (D_HEADS, LANES))
    d_out = _window_attn(qk_l, p_l, qk_c, p_c, sink_rows)
    return _matmul(jnp.concatenate([c_out, d_out], axis=-1), w_out)


SCAN_L = 128
GLA_SUB = 16
N_SCAN_STATES = 2 * A_HEADS
GATE_LR_LANE = 4 * A_HEADS


def _log_sigmoid(x):
    return jnp.minimum(x, 0.0) - jnp.log(1.0 + jnp.exp(-jnp.abs(x)))


def _tri_mask(rev):
    r = lax.broadcasted_iota(jnp.int32, (SCAN_L, SCAN_L), 0)
    c = lax.broadcasted_iota(jnp.int32, (SCAN_L, SCAN_L), 1)
    return (c >= r) if rev else (c <= r)


def _cumsum_rows(tri, x):
    hi = x.astype(jnp.bfloat16)
    r1 = x - hi.astype(jnp.float32)
    mid = r1.astype(jnp.bfloat16)
    lo = (r1 - mid.astype(jnp.float32)).astype(jnp.bfloat16)
    dot = lambda p: jnp.dot(tri, p, preferred_element_type=jnp.float32)
    return dot(hi) + dot(mid) + dot(lo)


def _conv_kernel(x_ref, w_ref, b_ref, o_ref, *, seg, scale):
    x = x_ref[...].astype(jnp.float32)
    n = x.shape[0]
    t = lax.broadcasted_iota(jnp.int32, x.shape, 0)
    first = (t == 0) | (t == seg)
    last = (t == seg - 1) | (t == n - 1)
    prev = jnp.where(first, 0.0, pltpu.roll(x, 1, 0))
    nxt = jnp.where(last, 0.0, pltpu.roll(x, n - 1, 0))
    y = w_ref[0:1, :] * prev + w_ref[1:2, :] * x + w_ref[2:3, :] * nxt + b_ref[...]
    y = y * jax.nn.sigmoid(y)
    j = pl.program_id(0)
    o_ref[...] = (y * jnp.where(j >= A_HEADS, scale, 1.0)).astype(o_ref.dtype)


def _short_conv_silu(p, conv_w, conv_b, seg):
    t = p.shape[0]
    nblk = 2 * A_WIDTH // LANES
    return pl.pallas_call(
        functools.partial(_conv_kernel, seg=seg, scale=HEAD_DIM ** -0.5),
        grid=(nblk,),
        in_specs=[pl.BlockSpec((t, LANES), lambda j: (0, j)),
                  pl.BlockSpec((8, LANES), lambda j: (0, j)),
                  pl.BlockSpec((1, LANES), lambda j: (0, j))],
        out_specs=pl.BlockSpec((t, LANES), lambda j: (0, j)),
        out_shape=jax.ShapeDtypeStruct((t, 2 * A_WIDTH), jnp.bfloat16),
        compiler_params=pltpu.CompilerParams(
            dimension_semantics=("parallel",), vmem_limit_bytes=VMEM_LIMIT_BYTES),
        name="short_conv_silu",
    )(p, jnp.pad(conv_w, ((0, 8 - CONV_W), (0, 0))), conv_b[None, :])


def _scan_chunk_index(j, n_ctx_chunks, n_chunks, rev):
    if not rev:
        return j
    return jnp.where(j < n_ctx_chunks, n_ctx_chunks - 1 - j, n_chunks + n_ctx_chunks - 1 - j)


def _mlstm_kernel(qf_ref, kf_ref, vf_ref, gf_ref, qb_ref, kb_ref, vb_ref, gb_ref, gbias_ref,
                  yf_ref, yb_ref, c_ref, m_ref):
    @pl.when(pl.program_id(0) == 0)
    def _():
        c_ref[...] = jnp.zeros_like(c_ref)
        m_ref[...] = jnp.zeros_like(m_ref)

    lane = lax.broadcasted_iota(jnp.int32, (SCAN_L, LANES), 1)
    is_forget = jnp.bitwise_and(lane, 2 * A_HEADS - 1) >= A_HEADS
    ones = jnp.ones((SCAN_L, HEAD_DIM), jnp.bfloat16)
    nt = (((1,), (1,)), ((), ()))
    tn = (((0,), (0,)), ((), ()))
    for z, (q_ref, k_ref, v_ref, g_ref, y_ref) in enumerate(
            ((qf_ref, kf_ref, vf_ref, gf_ref, yf_ref), (qb_ref, kb_ref, vb_ref, gb_ref, yb_ref))):
        rev = z == 1
        mask = _tri_mask(rev)
        gates = g_ref[...] + gbias_ref[...]
        gates = jnp.where(is_forget, _log_sigmoid(gates), gates)
        csum = _cumsum_rows(mask.astype(jnp.bfloat16), jnp.where(is_forget, gates, 0.0))
        gates_t = gates.T
        csum_t = csum.T
        end = 0 if rev else SCAN_L - 1
        for h in range(A_HEADS):
            li, lf = z * 2 * A_HEADS + h, z * 2 * A_HEADS + A_HEADS + h
            sl = slice(h * HEAD_DIM, (h + 1) * HEAD_DIM)
            q, k, v = q_ref[:, sl], k_ref[:, sl], v_ref[:, sl]
            b_col, b_row = csum[:, lf:lf + 1], csum_t[lf:lf + 1, :]
            i_col, i_row = gates[:, li:li + 1], gates_t[li:li + 1, :]
            st = z * A_HEADS + h
            m_prev = m_ref[st][0:1, 0:1]
            dlog = jnp.where(mask, b_col - b_row + i_row, NEG_INF)
            m_inter = b_col + m_prev
            m_t = jnp.maximum(m_inter, jnp.max(dlog, axis=-1, keepdims=True))
            s = lax.dot_general(q, k, nt, preferred_element_type=jnp.float32)
            sc = (s * jnp.exp(dlog - m_t)).astype(jnp.bfloat16)
            w_inter = jnp.exp(m_inter - m_t)
            v_ext = jnp.concatenate([v, ones], axis=1)
            r = (jnp.dot(sc, v_ext, preferred_element_type=jnp.float32)
                 + w_inter * jnp.dot(q, c_ref[st].astype(jnp.bfloat16), preferred_element_type=jnp.float32))
            num, den = r[:, :HEAD_DIM], r[:, HEAD_DIM:]
            y_ref[:, sl] = num / jnp.maximum(jnp.abs(den), jnp.exp(-m_t))
            m_new = m_t[end:end + 1, :]
            b_end = b_col[end:end + 1, :]
            w_end = jnp.exp(b_end - b_col + i_col - m_new)
            decay = jnp.exp(b_end + m_prev - m_new)
            kw = (k.astype(jnp.float32) * w_end).astype(jnp.bfloat16)
            c_ref[st] = decay * c_ref[st] + lax.dot_general(kw, v_ext, tn, preferred_element_type=jnp.float32)
            m_ref[st] = jnp.broadcast_to(m_new, (8, LANES))


def _scan_specs(n_ctx_chunks, n_chunks, width, col0, rev):
    return pl.BlockSpec((SCAN_L, width),
                        lambda j: (_scan_chunk_index(j, n_ctx_chunks, n_chunks, rev), col0))


def _mlstm(qk, p, gates, gate_bias, seg):
    t = qk.shape[0]
    nck = t // SCAN_L
    ncc = seg // SCAN_L
    ins, specs = [], []
    for rev in (False, True):
        ins += [qk, qk, p, gates]
        specs += [_scan_specs(ncc, nck, A_WIDTH, 0, rev), _scan_specs(ncc, nck, A_WIDTH, 1, rev),
                  _scan_specs(ncc, nck, A_WIDTH, 2, rev), _scan_specs(ncc, nck, LANES, 0, rev)]
    return pl.pallas_call(
        _mlstm_kernel,
        grid=(nck,),
        in_specs=specs + [pl.BlockSpec((1, LANES), lambda j: (0, 0))],
        out_specs=[_scan_specs(ncc, nck, A_WIDTH, 0, False), _scan_specs(ncc, nck, A_WIDTH, 0, True)],
        out_shape=[jax.ShapeDtypeStruct((t, A_WIDTH), jnp.float32)] * 2,
        scratch_shapes=[pltpu.VMEM((N_SCAN_STATES, HEAD_DIM, 2 * HEAD_DIM), jnp.float32),
                        pltpu.VMEM((N_SCAN_STATES, 8, LANES), jnp.float32)],
        compiler_params=pltpu.CompilerParams(
            dimension_semantics=("arbitrary",), vmem_limit_bytes=VMEM_LIMIT_BYTES),
        name="mlstm_scan",
    )(*ins, gate_bias)


def _gla_kernel(qf_ref, kf_ref, vf_ref, gf_ref, qb_ref, kb_ref, vb_ref, gb_ref, w2_ref, ab_ref,
                of_ref, ob_ref, s_ref):
    @pl.when(pl.program_id(0) == 0)
    def _():
        s_ref[...] = jnp.zeros_like(s_ref)

    nsub = SCAN_L // GLA_SUB
    row = lax.broadcasted_iota(jnp.int32, (SCAN_L, LANES), 0)
    lane = lax.broadcasted_iota(jnp.int32, (SCAN_L, LANES), 1)
    head_lane = (lane < B_KEY_DIM, lane >= B_KEY_DIM)
    r_sq = lax.broadcasted_iota(jnp.int32, (SCAN_L, SCAN_L), 0)
    c_sq = lax.broadcasted_iota(jnp.int32, (SCAN_L, SCAN_L), 1)
    sub_of_row = jnp.right_shift(row, 4)
    in_sub = jnp.bitwise_and(row, GLA_SUB - 1)
    sel_r = lax.broadcasted_iota(jnp.int32, (LANES, 2 * LANES), 0)
    sel_c = lax.broadcasted_iota(jnp.int32, (LANES, 2 * LANES), 1)
    head_sum = ((sel_r < B_KEY_DIM) == (sel_c < LANES)).astype(jnp.bfloat16)
    nt = (((1,), (1,)), ((), ()))
    tn = (((0,), (0,)), ((), ()))
    for z, (q_ref, k_ref, v_ref, g_ref, o_ref) in enumerate(
            ((qf_ref, kf_ref, vf_ref, gf_ref, of_ref), (qb_ref, kb_ref, vb_ref, gb_ref, ob_ref))):
        rev = z == 1
        tri = _tri_mask(rev).astype(jnp.bfloat16)
        lr = g_ref[...].astype(jnp.bfloat16)
        end = 0 if rev else SCAN_L - 1
        for pr in range(B_HEADS // 2):
            ls = slice(pr * LANES, (pr + 1) * LANES)
            za = jnp.dot(lr, w2_ref[z, :, ls], preferred_element_type=jnp.float32) + ab_ref[z:z + 1, ls]
            b = _cumsum_rows(tri, _log_sigmoid(za) * (1.0 / GLA_TAU))
            q = q_ref[:, ls].astype(jnp.float32) * (B_KEY_DIM ** -0.5)
            k = k_ref[:, ls].astype(jnp.float32)
            v = v_ref[:, 2 * pr * HEAD_DIM:(2 * pr + 2) * HEAD_DIM]
            b_ref_rows = jnp.zeros_like(b)
            for j in range(nsub):
                e = j * GLA_SUB if rev else j * GLA_SUB + GLA_SUB - 1
                b_ref_rows = jnp.where(sub_of_row == j, b[e:e + 1, :], b_ref_rows)
            k_sub = (k * jnp.exp(b_ref_rows - b)).astype(jnp.bfloat16)
            k_sub_h = [jnp.where(head_lane[hh], k_sub, jnp.zeros_like(k_sub)) for hh in range(2)]
            attn =[jnp.zeros((SCAN_L, SCAN_L), jnp.float32), jnp.zeros((SCAN_L, SCAN_L), jnp.float32)]
            for j in range(nsub):
                e = j * GLA_SUB if rev else j * GLA_SUB + GLA_SUB - 1
                if j == (0 if rev else nsub - 1):
                    continue
                later = (sub_of_row < j) if rev else (sub_of_row > j)
                qj = (q * jnp.exp(jnp.where(later, b - b[e:e + 1, :], NEG_INF))).astype(jnp.bfloat16)
                for hh in range(2):
                    kj = jnp.where(sub_of_row == j, k_sub_h[hh], jnp.zeros_like(k_sub))
                    attn[hh] = attn[hh] + lax.dot_general(qj, kj, nt, preferred_element_type=jnp.float32)
            for d in range(GLA_SUB):
                shift = (SCAN_L - d) % SCAN_L if rev else d
                k_s = pltpu.roll(k, shift, 0) if d else k
                b_s = pltpu.roll(b, shift, 0) if d else b
                ok_row = (in_sub <= GLA_SUB - 1 - d) if rev else (in_sub >= d)
                f = (q * k_s * jnp.exp(jnp.where(ok_row, b - b_s, NEG_INF))).astype(jnp.bfloat16)
                red = jnp.dot(f, head_sum, preferred_element_type=jnp.float32)
                on_diag = (c_sq == r_sq + d) if rev else (c_sq == r_sq - d)
                for hh in range(2):
                    attn[hh] = attn[hh] + jnp.where(on_diag, red[:, hh * LANES:(hh + 1) * LANES], 0.0)
            q_in = q * jnp.exp(b)
            b_end = b[end:end + 1, :]
            k_out = (k * jnp.exp(b_end - b)).astype(jnp.bfloat16)
            st = z * (B_HEADS // 2) + pr
            s_t = s_ref[st]
            s_bf = s_t.astype(jnp.bfloat16)
            for hh in range(2):
                vs = slice(hh * HEAD_DIM, (hh + 1) * HEAD_DIM)
                qh = jnp.where(head_lane[hh], q_in, 0.0).astype(jnp.bfloat16)
                o = (jnp.dot(attn[hh].astype(jnp.bfloat16), v[:, vs], preferred_element_type=jnp.float32)
                     + lax.dot_general(qh, s_bf[vs, :], nt, preferred_element_type=jnp.float32))
                o_ref[:, (2 * pr + hh) * HEAD_DIM:(2 * pr + hh + 1) * HEAD_DIM] = o
            s_ref[st] = s_t * jnp.exp(b_end) + lax.dot_general(v, k_out, tn, preferred_element_type=jnp.float32)


def _gla(p, gates, w2, alpha_b, seg):
    t = p.shape[0]
    nck = t // SCAN_L
    ncc = seg // SCAN_L
    ins, specs = [], []
    for rev in (False, True):
        ins += [p, p, p, gates]
        specs += [_scan_specs(ncc, nck, B_KEY_WIDTH, P_BQ // B_KEY_WIDTH, rev),
                  _scan_specs(ncc, nck, B_KEY_WIDTH, P_BK // B_KEY_WIDTH, rev),
                  _scan_specs(ncc, nck, B_VAL_WIDTH, P_BV // B_VAL_WIDTH, rev),
                  _scan_specs(ncc, nck, LANES, 0, rev)]
    return pl.pallas_call(
        _gla_kernel,
        grid=(nck,),
        in_specs=specs + [pl.BlockSpec((2, LANES, B_KEY_WIDTH), lambda j: (0, 0, 0)),
                          pl.BlockSpec((2, B_KEY_WIDTH), lambda j: (0, 0))],
        out_specs=[_scan_specs(ncc, nck, B_VAL_WIDTH, 0, False), _scan_specs(ncc, nck, B_VAL_WIDTH, 0, True)],
        out_shape=[jax.ShapeDtypeStruct((t, B_VAL_WIDTH), jnp.float32)] * 2,
        scratch_shapes=[pltpu.VMEM((B_HEADS, 2 * HEAD_DIM, LANES), jnp.float32)],
        compiler_params=pltpu.CompilerParams(
            dimension_semantics=("arbitrary",), vmem_limit_bytes=VMEM_LIMIT_BYTES),
        name="gla_scan",
    )(*ins, w2, alpha_b)


def _head_out_kernel(af_ref, ab_ref, bf_ref, bb_ref, gate_ref, g_ref, o_ref):
    is_gla = pl.program_id(1) >= A_HEADS
    y = jnp.where(is_gla, bf_ref[...] + bb_ref[...], af_ref[...] + ab_ref[...])
    yn = y * lax.rsqrt(jnp.mean(y * y, axis=-1, keepdims=True) + EPS) * g_ref[0]
    gate = gate_ref[...].astype(jnp.float32)
    o_ref[...] = (yn * jax.nn.sigmoid(gate) * jnp.where(is_gla, gate, 1.0)).astype(o_ref.dtype)


def _head_out(ya, yb, p, head_g):
    t = p.shape[0]
    bt = _row_tile(t)
    nh = A_HEADS + B_HEADS
    a_spec = pl.BlockSpec((bt, HEAD_DIM), lambda i, j: (i, jnp.minimum(j, A_HEADS - 1)))
    b_spec = pl.BlockSpec((bt, HEAD_DIM), lambda i, j: (i, jnp.maximum(j - A_HEADS, 0)))

    def gate_col(i, j):
        return (i, jnp.where(j < A_HEADS, P_AO + j, P_BR + j - A_HEADS))

    return pl.pallas_call(
        _head_out_kernel,
        grid=(t // bt, nh),
        in_specs=[a_spec, a_spec, b_spec, b_spec,
                  pl.BlockSpec((bt, HEAD_DIM), gate_col),
                  pl.BlockSpec((1, 1, HEAD_DIM), lambda i, j: (j // A_HEADS, 0, 0))],
        out_specs=pl.BlockSpec((bt, HEAD_DIM), lambda i, j: (i, j)),
        out_shape=jax.ShapeDtypeStruct((t, nh * HEAD_DIM), jnp.bfloat16),
        compiler_params=pltpu.CompilerParams(
            dimension_semantics=("parallel", "arbitrary"), vmem_limit_bytes=VMEM_LIMIT_BYTES),
        name="head_out",
    )(ya[0], ya[1], yb[0], yb[1], p, head_g.astype(jnp.float32)[:, None, :])


P_AO = 24
P_BQ, P_BK, P_BV = 4096, 4608, 5120
P_BR = 48


def _mixer_ab(h, w_in, conv_w, conv_b, gate_b, alpha_w2, alpha_b, head_g, w_out, seg):
    bf = jnp.bfloat16
    cols = np.cumsum((0,) + AB_SPLITS)
    pick = lambda *ids: jnp.concatenate([w_in[:, cols[i]:cols[i + 1]] for i in ids], axis=1)
    w_main = pick(0, 1, 2, 3, 5, 6, 7, 8).astype(bf)
    w_gate = jnp.pad(pick(4, 9), ((0, 0), (0, LANES - 4 * A_HEADS - 2 * GLA_RANK))).astype(bf)
    hb = h
    p = _matmul(hb, w_main, bf)
    gates = _matmul(hb, w_gate, jnp.float32)
    gate_bias = jnp.pad(gate_b.reshape(1, -1).astype(jnp.float32), ((0, 0), (0, LANES - 4 * A_HEADS)))
    w2 = jnp.zeros((2, LANES, B_KEY_WIDTH), jnp.float32)
    for z in range(2):
        w2 = w2.at[z, GATE_LR_LANE + z * GLA_RANK:GATE_LR_LANE + (z + 1) * GLA_RANK].set(alpha_w2[z])
    qk = _short_conv_silu(p, conv_w, conv_b, seg)
    ya = _mlstm(qk, p, gates, gate_bias, seg)
    yb = _gla(p, gates, w2.astype(bf), alpha_b.astype(jnp.float32), seg)
    return _matmul(_head_out(ya, yb, p, head_g), w_out)


MOD_ROWS = 256


def _res_mod_kernel(x_ref, y_ref, p_ref, xo_ref, h_ref):
    p = p_ref[0]
    x = x_ref[...] + p[0:1, :] * y_ref[...]
    xo_ref[...] = x
    r = x * lax.rsqrt(jnp.mean(x * x, axis=-1, keepdims=True) + EPS) * p[1:2, :]
    h_ref[...] = (r * (1.0 + p[3:4, :]) + p[2:3, :]).astype(h_ref.dtype)


def _mod_kernel(x_ref, p_ref, h_ref):
    p = p_ref[0]
    x = x_ref[...]
    r = x * lax.rsqrt(jnp.mean(x * x, axis=-1, keepdims=True) + EPS) * p[1:2, :]
    h_ref[...] = (r * (1.0 + p[3:4, :]) + p[2:3, :]).astype(h_ref.dtype)


def _mod_params(gate_c, gate_l, gain, shift_c, shift_l, scale_c, scale_l):
    rows = lambda g, sh, sc: jnp.concatenate(
        [g.reshape(1, -1), gain.reshape(1, -1), sh.reshape(1, -1), sc.reshape(1, -1),
         jnp.zeros((4, gain.shape[-1]), jnp.float32)], axis=0)
    return jnp.stack([rows(gate_c, shift_c, scale_c), rows(gate_l, shift_l, scale_l)]).astype(jnp.float32)


def _res_mod(x, y, params, seg, row0=0, n_rows=None):
    d = x.shape[1]
    n_rows = x.shape[0] - row0 if n_rows is None else n_rows
    bt = MOD_ROWS
    off, seg_blocks = row0 // bt, seg // bt
    row_spec = pl.BlockSpec((bt, d), lambda i: (i + off, 0))
    out_spec = pl.BlockSpec((bt, d), lambda i: (i, 0))
    p_spec = pl.BlockSpec((1, 8, d), lambda i: (jnp.where(i + off < seg_blocks, 0, 1), 0, 0))
    cp = pltpu.CompilerParams(dimension_semantics=("parallel",), vmem_limit_bytes=VMEM_LIMIT_BYTES)
    h_shape = jax.ShapeDtypeStruct((n_rows, d), jnp.bfloat16)
    if y is None:
        return pl.pallas_call(_mod_kernel, grid=(n_rows // bt,), in_specs=[row_spec, p_spec], out_specs=out_spec,
                              out_shape=h_shape, compiler_params=cp, name="modulate")(x, params)
    return pl.pallas_call(
        _res_mod_kernel, grid=(n_rows // bt,), in_specs=[row_spec, row_spec, p_spec],
        out_specs=[out_spec, out_spec],
        out_shape=[jax.ShapeDtypeStruct((n_rows, d), jnp.float32), h_shape],
        compiler_params=cp, name="residual_modulate")(x, y, params)


def _modulation_kernel(c_ref, w_ref, b_ref, o_ref):
    act = c_ref[...]
    act = (act * jax.nn.sigmoid(act)).astype(jnp.bfloat16)
    o_ref[0] = jnp.dot(act, w_ref[0].astype(jnp.bfloat16), preferred_element_type=jnp.float32) + b_ref[0]


def _modulation(c, c_ctx, mod_w, mod_b):
    nl, d, n = mod_w.shape
    cond = jnp.concatenate([c_ctx.reshape(1, d), c.reshape(1, d), jnp.zeros((6, d), jnp.float32)], axis=0)
    bn = n // 8
    return pl.pallas_call(
        _modulation_kernel,
        grid=(nl, n // bn),
        in_specs=[pl.BlockSpec((8, d), lambda l, j: (0, 0)),
                  pl.BlockSpec((1, d, bn), lambda l, j: (l, 0, j)),
                  pl.BlockSpec((1, 1, bn), lambda l, j: (l, 0, j))],
        out_specs=pl.BlockSpec((1, 8, bn), lambda l, j: (l, 0, j)),
        out_shape=jax.ShapeDtypeStruct((nl, 8, n), jnp.float32),
        compiler_params=pltpu.CompilerParams(
            dimension_semantics=("parallel", "parallel"), vmem_limit_bytes=VMEM_LIMIT_BYTES),
        name="modulation",
    )(cond, mod_w, mod_b[:, None, :])


def kernel(x, c, ctx, c_ctx, mod_w, mod_b, norm_g, ab_w_in, ab_conv_w, ab_conv_b, ab_gate_b, ab_alpha_w2,
           ab_alpha_b, ab_head_g, ab_w_out, cd_w_in, cd_qk_g, cd_rpb, cd_sink, cd_w_out, peer_w_q,
           peer_sub_keys, peer_u, peer_v):
    assert DEPTH == 2
    bf = jnp.bfloat16
    m = ctx.shape[1]
    xs = jnp.concatenate([ctx[0], x[0]], axis=0)
    ub = _cast_bf16(peer_u)
    vb = _cast_bf16(peer_v)
    mod_all = _modulation(c, c_ctx, mod_w, mod_b)
    mods = [(jnp.split(mod_all[l, 0], 6), jnp.split(mod_all[l, 1], 6)) for l in range(DEPTH)]
    one = jnp.ones_like(mods[0][0][0])

    def params(l, sub, gate_c, gate_l):
        mc, ml = mods[l]
        return _mod_params(gate_c, gate_l, norm_g[l, sub], mc[3 * sub], ml[3 * sub], mc[3 * sub + 1], ml[3 * sub + 1])

    (mc0, ml0), (mc1, ml1) = mods
    h = _res_mod(xs, None, params(0, 0, one, one), m)
    y = _mixer_ab(h, ab_w_in[0], ab_conv_w[0], ab_conv_b[0], ab_gate_b[0], ab_alpha_w2[0], ab_alpha_b[0],
                  ab_head_g[0], ab_w_out[0].astype(bf), m)
    xs, h = _res_mod(xs, y, params(0, 1, mc0[2], ml0[2]), m)
    y = _peer(h, peer_w_q[0].astype(bf), peer_sub_keys[0].astype(bf), ub, vb, 0)
    p10 = params(1, 0, mc0[5], ml0[5])
    _, hc = _res_mod(xs, y, p10, m, 0, m)
    xl, hl = _res_mod(xs, y, p10, m, m)
    y = _mixer_cd(hc, hl, cd_w_in[0].astype(bf), cd_qk_g[0], cd_rpb[0], cd_sink[0], cd_w_out[0].astype(bf))
    xl, h = _res_mod(xl, y, params(1, 1, ml1[2], ml1[2]), 0)
    y = _peer(h, peer_w_q[1].astype(bf), peer_sub_keys[1].astype(bf), ub, vb, 1)
    return (xl + ml1[5] * y)[None]
```

```python
import functools

import numpy as np
import jax
import jax.numpy as jnp
from jax import lax
from jax.experimental import pallas as pl
from jax.experimental.pallas import tpu as pltpu

D_MODEL = 2048
SEQ = 8192
DEPTH = 2
GRID_W = 64
CTX_LEN = 256
HEAD_DIM = 128
N_GROUP_HEADS = 8
EPS = 1e-6
A_HEADS = 8
A_WIDTH = 1024
CONV_W = 3
B_HEADS = 8
B_KEY_DIM = 64
B_KEY_WIDTH = 512
B_VAL_WIDTH = 1024
GLA_RANK = 16
GLA_TAU = 16.0
SCAN_CHUNK = 64
C_HEADS = 8
C_WIDTH = 1024
WIN_H = 8
WIN_W = 16
D_HEADS = 8
D_KV_HEADS = 2
D_GROUP = 4
D_WIDTH = 1024
D_KV_WIDTH = 256
WINDOW = 128
ROPE_BASE = 10000.0
PEER_HEADS = 8
PEER_NKEYS = 128
PEER_EXPERTS = PEER_NKEYS * PEER_NKEYS
PEER_QDIM = 256
PEER_TOPK = 16
PEER_SLOTS = PEER_HEADS * PEER_TOPK
AB_SPLITS = (A_WIDTH, A_WIDTH, A_WIDTH, A_WIDTH, 4 * A_HEADS,
             B_KEY_WIDTH, B_KEY_WIDTH, B_VAL_WIDTH, B_VAL_WIDTH, 2 * GLA_RANK)
CD_SPLITS = (C_WIDTH, C_WIDTH, D_KV_WIDTH, D_KV_WIDTH, C_WIDTH, D_WIDTH)
CD_KV_COLS = sum(CD_SPLITS[:4])

VMEM_LIMIT_BYTES = 56 * 1024 * 1024
LANES = 128


def _row_tile(m):
    for t in (1024, 768, 512, 256, 128):
        if m % t == 0:
            return t
    raise ValueError(f"unsupported row count {m}")


def _mm_kernel(x_ref, w_ref, o_ref):
    o_ref[...] = jnp.dot(x_ref[...], w_ref[...], preferred_element_type=jnp.float32).astype(o_ref.dtype)


def _matmul(x, w, out_dtype=jnp.float32):
    m, k = x.shape
    n = w.shape[1]
    bn = min(512, n)
    n_pad = -(-n // bn) * bn
    if n_pad != n:
        w = jnp.pad(w, ((0, 0), (0, n_pad - n)))
    bm = _row_tile(m)
    out = pl.pallas_call(
        _mm_kernel,
        grid=(m // bm, n_pad // bn),
        in_specs=[pl.BlockSpec((bm, k), lambda i, j: (i, 0)),
                  pl.BlockSpec((k, bn), lambda i, j: (0, j))],
        out_specs=pl.BlockSpec((bm, bn), lambda i, j: (i, j)),
        out_shape=jax.ShapeDtypeStruct((m, n_pad), out_dtype),
        compiler_params=pltpu.CompilerParams(
            dimension_semantics=("parallel", "arbitrary"), vmem_limit_bytes=VMEM_LIMIT_BYTES),
        name="matmul",
    )(x.astype(jnp.bfloat16), w.astype(jnp.bfloat16))
    return out[:, :n] if n_pad != n else out


ROUTE_TOKENS = (768, 512, 256, 128)
GATE_TOKENS = 128
NEG_INF = float("-inf")


def _top16(s, key, big):
    slot = lax.broadcasted_iota(jnp.int32, (PEER_TOPK, s.shape[1]), 0)
    vals = jnp.zeros((PEER_TOPK, s.shape[1]), jnp.float32)
    keys = jnp.zeros((PEER_TOPK, s.shape[1]), jnp.float32)
    for it in range(PEER_TOPK):
        m = jnp.max(s, axis=0, keepdims=True)
        k = jnp.min(jnp.where(s == m, key, big), axis=0, keepdims=True)
        vals = jnp.where(slot == it, m, vals)
        keys = jnp.where(slot == it, k, keys)
        s = jnp.where(key == k, NEG_INF, s)
    return vals, keys


def _route_kernel(h_ref, wq_ref, k_ref, i1_ref, i2_ref, g_ref):
    bt = h_ref.shape[0]
    q = jnp.dot(h_ref[...], wq_ref[...], preferred_element_type=jnp.float32).astype(jnp.bfloat16)
    nt = (((1,), (1,)), ((), ()))
    s1 = lax.dot_general(k_ref[0, 0], q[:, :PEER_NKEYS], nt, preferred_element_type=jnp.float32)
    s2 = lax.dot_general(k_ref[0, 1], q[:, PEER_NKEYS:], nt, preferred_element_type=jnp.float32)
    row = lax.broadcasted_iota(jnp.int32, (PEER_NKEYS, bt), 0).astype(jnp.float32)
    v1, r1 = _top16(s1, row, float(PEER_NKEYS))
    v2, r2 = _top16(s2, row, float(PEER_NKEYS))
    i16 = lax.broadcasted_iota(jnp.int32, (16, bt), 0).astype(jnp.float32)
    i8 = lax.broadcasted_iota(jnp.int32, (8, bt), 0).astype(jnp.float32)
    cand = [v1[0:1] + v2]
    flat = [i16]
    for a in range(1, 8):
        cand.append(v1[a:a + 1] + v2[0:8])
        flat.append(i8 + float(a * PEER_TOPK))
    cand.append(v1[8:16] + v2[0:1])
    flat.append((i8 + 8.0) * float(PEER_TOPK))
    best, key = _top16(jnp.concatenate(cand, axis=0), jnp.concatenate(flat, axis=0), float(PEER_TOPK * PEER_TOPK))
    key = key.astype(jnp.int32)
    a_sel = jnp.right_shift(key, 4)
    b_sel = jnp.bitwise_and(key, PEER_TOPK - 1)
    e1 = jnp.zeros_like(r1)
    e2 = jnp.zeros_like(r2)
    for r in range(PEER_TOPK):
        e1 = jnp.where(a_sel == r, r1[r:r + 1], e1)
        e2 = jnp.where(b_sel == r, r2[r:r + 1], e2)
    e1 = e1.astype(jnp.int32)
    e2 = e2.astype(jnp.int32)
    ex = jnp.exp(best - best[0:1])
    i1_ref[...] = e1
    i2_ref[...] = e2
    g_ref[...] = ex / jnp.sum(ex, axis=0, keepdims=True)


def _route(hb, wq, keys):
    t, d = hb.shape
    bt = next(b for b in ROUTE_TOKENS if t % b == 0)
    slot_spec = pl.BlockSpec((PEER_TOPK, bt), lambda i, h: (h, i))
    return pl.pallas_call(
        _route_kernel,
        grid=(t // bt, PEER_HEADS),
        in_specs=[pl.BlockSpec((bt, d), lambda i, h: (i, 0)),
                  pl.BlockSpec((d, PEER_QDIM), lambda i, h: (0, h)),
                  pl.BlockSpec((1, 2, PEER_NKEYS, PEER_QDIM // 2), lambda i, h: (h, 0, 0, 0))],
        out_specs=[slot_spec, slot_spec, slot_spec],
        out_shape=[jax.ShapeDtypeStruct((PEER_SLOTS, t), jnp.int32),
                   jax.ShapeDtypeStruct((PEER_SLOTS, t), jnp.int32),
                   jax.ShapeDtypeStruct((PEER_SLOTS, t), jnp.float32)],
        compiler_params=pltpu.CompilerParams(
            dimension_semantics=("parallel", "arbitrary"), vmem_limit_bytes=VMEM_LIMIT_BYTES),
        name="peer_route",
    )(hb, wq, keys)


GATE_GROUP = 16


def _gate_matrix_kernel(i1_ref, i2_ref, g_ref, o_ref, i1_t, g_t, stage):
    tb = o_ref.shape[1]
    i1_t[...] = i1_ref[...].T
    g_t[...] = g_ref[...].T
    row_id = lax.broadcasted_iota(jnp.int32, (PEER_NKEYS, PEER_SLOTS), 0)
    lane_id = lax.broadcasted_iota(jnp.int32, (PEER_SLOTS, PEER_NKEYS), 1)
    zero = jnp.zeros((PEER_NKEYS, PEER_SLOTS), jnp.bfloat16)

    for tp in range(tb // 2):
        ta = 2 * tp
        tb_ = ta + 1
        ca = jnp.where(row_id == i1_t[ta:ta + 1, :], g_t[ta:ta + 1, :], 0.0).astype(jnp.bfloat16)
        cb = jnp.where(row_id == i1_t[tb_:tb_ + 1, :], g_t[tb_:tb_ + 1, :], 0.0).astype(jnp.bfloat16)
        oa = jnp.where(i2_ref[:, ta:ta + 1] == lane_id, 1.0, 0.0).astype(jnp.bfloat16)
        ob = jnp.where(i2_ref[:, tb_:tb_ + 1] == lane_id, 1.0, 0.0).astype(jnp.bfloat16)
        lhs = jnp.concatenate([ca, cb], axis=1)
        rhs = jnp.concatenate([jnp.concatenate([oa, zero], axis=1),
                               jnp.concatenate([zero, ob], axis=1)], axis=0)
        out = jnp.dot(lhs, rhs, preferred_element_type=jnp.float32)
        stage[ta % GATE_GROUP] = out[:, :PEER_NKEYS]
        stage[tb_ % GATE_GROUP] = out[:, PEER_NKEYS:]
        if tb_ % GATE_GROUP == GATE_GROUP - 1:
            t0 = tb_ + 1 - GATE_GROUP
            o_ref[:, t0:t0 + GATE_GROUP, :] = jnp.swapaxes(stage[...], 0, 1).astype(o_ref.dtype)


def _gate_matrix(i1, i2, g):
    t = i1.shape[1]
    tb = GATE_TOKENS
    return pl.pallas_call(
        _gate_matrix_kernel,
        grid=(t // tb,),
        in_specs=[pl.BlockSpec((PEER_SLOTS, tb), lambda i: (0, i))] * 3,
        out_specs=pl.BlockSpec((PEER_NKEYS, tb, PEER_NKEYS), lambda i: (0, i, 0)),
        out_shape=jax.ShapeDtypeStruct((PEER_NKEYS, t, PEER_NKEYS), jnp.bfloat16),
        scratch_shapes=[pltpu.VMEM((tb, PEER_SLOTS), jnp.int32),
                        pltpu.VMEM((tb, PEER_SLOTS), jnp.float32),
                        pltpu.VMEM((GATE_GROUP, PEER_NKEYS, PEER_NKEYS), jnp.float32)],
        compiler_params=pltpu.CompilerParams(
            dimension_semantics=("parallel",), vmem_limit_bytes=VMEM_LIMIT_BYTES),
        name="peer_gate_matrix",
    )(i1, i2, g)


def _gelu_tanh(x):
    return 0.5 * x * (1.0 + jnp.tanh(0.7978845608028654 * (x + 0.044715 * x * x * x)))


FP8 = jnp.float8_e4m3fn
FP8_TARGET = 240.0


def _row_scale(x):
    amax = jnp.max(jnp.abs(x), axis=-1, keepdims=True)
    pos = amax > 0.0
    return jnp.where(pos, FP8_TARGET / amax, 1.0), jnp.where(pos, amax * (1.0 / FP8_TARGET), 1.0)


def _quant_kernel(x_ref, o_ref, s_ref):
    x = x_ref[0]
    scale, inv = _row_scale(x)
    o_ref[0] = (x * scale).astype(o_ref.dtype)
    s_ref[0] = jnp.broadcast_to(inv, s_ref.shape[1:])


def _quant_rows(x):
    nl, r, c = x.shape
    br = 1024
    q, s = pl.pallas_call(
        _quant_kernel,
        grid=(nl, r // br),
        in_specs=[pl.BlockSpec((1, br, c), lambda l, i: (l, i, 0))],
        out_specs=[pl.BlockSpec((1, br, c), lambda l, i: (l, i, 0)),
                   pl.BlockSpec((1, br, LANES), lambda l, i: (l, i, 0))],
        out_shape=[jax.ShapeDtypeStruct(x.shape, FP8), jax.ShapeDtypeStruct((nl, r, LANES), jnp.float32)],
        compiler_params=pltpu.CompilerParams(
            dimension_semantics=("parallel", "parallel"), vmem_limit_bytes=VMEM_LIMIT_BYTES),
        name="quant_rows",
    )(x)
    return q, s[:, :, 0][:, None, :]


def _peer_dense_kernel(h_ref, u_ref, su_ref, v_ref, sv_ref, g_ref, o_ref, h8_ref, sh_ref):
    j = pl.program_id(1)

    @pl.when(j == 0)
    def _():
        h = h_ref[...].astype(jnp.float32)
        scale, inv = _row_scale(h)
        h8_ref[...] = (h * scale).astype(h8_ref.dtype)
        sh_ref[...] = jnp.broadcast_to(inv, sh_ref.shape)

    nt = (((1,), (1,)), ((), ()))
    acc = lax.dot_general(h8_ref[...], u_ref[0], nt, preferred_element_type=jnp.float32)
    act = _gelu_tanh(acc * sh_ref[:, 0:1] * su_ref[0])
    gate = jnp.concatenate([g_ref[r] for r in range(g_ref.shape[0])], axis=1)
    w = act * gate.astype(jnp.float32) * sv_ref[0]
    scale, inv = _row_scale(w)
    contrib = jnp.dot((w * scale).astype(FP8), v_ref[0], preferred_element_type=jnp.float32) * inv

    @pl.when(j == 0)
    def _():
        o_ref[...] = contrib

    @pl.when(j != 0)
    def _():
        o_ref[...] += contrib


def _peer_dense(h, u, v, gmat, layer):
    t, d = h.shape
    (u8, su), (v8, sv) = u, v
    e = v8.shape[1]
    bm = _row_tile(t)
    be = 1024
    tab = pl.BlockSpec((1, be, d), lambda i, j: (layer, j, 0))
    sc = pl.BlockSpec((1, 1, be), lambda i, j: (layer, 0, j))
    return pl.pallas_call(
        _peer_dense_kernel,
        grid=(t // bm, e // be),
        in_specs=[pl.BlockSpec((bm, d), lambda i, j: (i, 0)), tab, sc, tab, sc,
                  pl.BlockSpec((be // PEER_NKEYS, bm, PEER_NKEYS), lambda i, j: (j, i, 0))],
        out_specs=pl.BlockSpec((bm, d), lambda i, j: (i, 0)),
        out_shape=jax.ShapeDtypeStruct((t, d), jnp.float32),
        scratch_shapes=[pltpu.VMEM((bm, d), FP8), pltpu.VMEM((bm, LANES), jnp.float32)],
        compiler_params=pltpu.CompilerParams(
            dimension_semantics=("parallel", "arbitrary"), vmem_limit_bytes=VMEM_LIMIT_BYTES),
        name="peer_dense",
    )(h, u8, su, v8, sv, gmat)


def _peer(h, wq, keys, u, v, layer):
    hb = h
    i1, i2, g = _route(hb, wq, keys)
    return _peer_dense(hb, u, v, _gate_matrix(i1, i2, g), layer)


ROPE_HALF = 32
NA_QROWS = 8
NA_KROWS = 16
SWA_QB = 256
SWA_KB = SWA_QB + 2 * WINDOW

P_CK, P_CV, P_DK, P_DV, P_CQ, P_DQ = 0, 8, 16, 18, 20, 28
QK_CK, QK_DK, QK_CQ, QK_DQ = 0, 8, 10, 20


def _rope_tables(n):
    t = np.arange(n)
    freqs = (np.float32(ROPE_BASE) ** (-np.arange(ROPE_HALF, dtype=np.float32) / np.float32(ROPE_HALF))).astype(np.float32)
    ang_r = (t // GRID_W).astype(np.float32)[:, None] * freqs
    ang_c = (t % GRID_W).astype(np.float32)[:, None] * freqs
    cos = np.concatenate([np.cos(ang_r), np.cos(ang_r), np.cos(ang_c), np.cos(ang_c)], axis=-1)
    sin = np.concatenate([-np.sin(ang_r), np.sin(ang_r), -np.sin(ang_c), np.sin(ang_c)], axis=-1)
    return cos.astype(np.float32), sin.astype(np.float32)


def _qk_prep_kernel(tbl_ref, p_ref, g_ref, cos_ref, sin_ref, o_ref):
    del tbl_ref
    x = p_ref[...].astype(jnp.float32)
    y = x * lax.rsqrt(jnp.mean(x * x, axis=-1, keepdims=True) + EPS) * g_ref[0]
    lane = lax.broadcasted_iota(jnp.int32, y.shape, 1)
    first = jnp.bitwise_and(lane, 2 * ROPE_HALF - 1) < ROPE_HALF
    partner = jnp.where(first, pltpu.roll(y, LANES - ROPE_HALF, 1), pltpu.roll(y, ROPE_HALF, 1))
    o_ref[...] = (y * cos_ref[0] + partner * sin_ref[0]).astype(o_ref.dtype)


def _qk_prep(p, table, gains, cos, sin):
    n = p.shape[0]
    ncol = table.shape[1]
    bt = 2048 if n % 2048 == 0 else n
    grid_spec = pltpu.PrefetchScalarGridSpec(
        num_scalar_prefetch=1,
        grid=(n // bt, ncol),
        in_specs=[pl.BlockSpec((bt, HEAD_DIM), lambda i, j, tbl: (i, tbl[0, j])),
                  pl.BlockSpec((1, 1, HEAD_DIM), lambda i, j, tbl: (tbl[1, j], 0, 0)),
                  pl.BlockSpec((1, bt, HEAD_DIM), lambda i, j, tbl: (tbl[2, j], i, 0)),
                  pl.BlockSpec((1, bt, HEAD_DIM), lambda i, j, tbl: (tbl[2, j], i, 0))],
        out_specs=pl.BlockSpec((bt, HEAD_DIM), lambda i, j, tbl: (i, j)),
    )
    return pl.pallas_call(
        _qk_prep_kernel,
        grid_spec=grid_spec,
        out_shape=jax.ShapeDtypeStruct((n, ncol * HEAD_DIM), jnp.bfloat16),
        compiler_params=pltpu.CompilerParams(
            dimension_semantics=("parallel", "arbitrary"), vmem_limit_bytes=VMEM_LIMIT_BYTES),
        name="qk_prep",
    )(table, p, gains, cos, sin)


def _na_bias(rpb, n):
    rows = n // GRID_W
    nr, nc = 2 * WIN_H - 1, 2 * WIN_W - 1
    c = np.arange(GRID_W)[:, None]
    kc = np.arange(GRID_W)[None, :]
    cs = np.clip(c - WIN_W // 2, 0, GRID_W - WIN_W)
    col_ok = (kc >= cs) & (kc < cs + WIN_W)
    oh_col = (np.clip(kc - c + WIN_W - 1, 0, nc - 1)[None] == np.arange(nc)[:, None, None]).astype(np.float32)
    oh_row, ok = [], []
    for r0, rlo in ((NA_QROWS, NA_QROWS - 4), (0, 0), (rows - NA_QROWS, rows - NA_KROWS)):
        r = r0 + np.arange(NA_QROWS)[:, None]
        kr = rlo + np.arange(NA_KROWS)[None, :]
        rs = np.clip(r - WIN_H // 2, 0, rows - WIN_H)
        row_ok = (kr >= rs) & (kr < rs + WIN_H)
        oh_row.append((np.clip(kr - r + WIN_H - 1, 0, nr - 1)[..., None] == np.arange(nr)).astype(np.float32))
        ok.append((row_ok[:, None, :, None] & col_ok[None, :, None, :]).reshape(NA_QROWS * GRID_W, NA_KROWS * GRID_W))
    hp = lax.Precision.HIGHEST
    colb = jnp.einsum('hab,bcd->hacd', rpb.astype(jnp.float32), jnp.asarray(oh_col), precision=hp)
    bias = jnp.einsum('tqka,hacd->htqckd', jnp.asarray(np.stack(oh_row)), colb, precision=hp)
    bias = bias.reshape(rpb.shape[0], 3, NA_QROWS * GRID_W, NA_KROWS * GRID_W)
    return jnp.where(jnp.asarray(np.stack(ok))[None], bias, NEG_INF)


def _softmax_pv(parts, extra_logit=None):
    m = parts[0][0].max(axis=-1, keepdims=True)
    for s, _ in parts[1:]:
        m = jnp.maximum(m, s.max(axis=-1, keepdims=True))
    if extra_logit is not None:
        m = jnp.maximum(m, extra_logit)
    den = jnp.exp(extra_logit - m) if extra_logit is not None else 0.0
    acc = None
    for s, v in parts:
        p = jnp.exp(s - m)
        den = den + p.sum(axis=-1, keepdims=True)
        pv = jnp.dot(p.astype(jnp.bfloat16), v, preferred_element_type=jnp.float32)
        acc = pv if acc is None else acc + pv
    return acc / den


def _na_kernel(q_ref, k_ref, v_ref, kc_ref, vc_ref, bias_ref, o_ref):
    n = q_ref.shape[0]
    qb = NA_QROWS * GRID_W
    kb = NA_KROWS * GRID_W
    nblk = n // qb
    nt = (((1,), (1,)), ((), ()))
    kc = kc_ref[...]
    vc = vc_ref[...]

    def body(b, carry):
        q0 = pl.multiple_of(b * qb, qb)
        ks = pl.multiple_of(jnp.clip(b * qb - (NA_KROWS - NA_QROWS) // 2 * GRID_W, 0, n - kb), 2 * LANES)
        kind = jnp.where(b == 0, 1, jnp.where(b == nblk - 1, 2, 0))
        q = q_ref[pl.ds(q0, qb), :]
        s_w = lax.dot_general(q, k_ref[pl.ds(ks, kb), :], nt, preferred_element_type=jnp.float32) + bias_ref[0, kind]
        s_c = lax.dot_general(q, kc, nt, preferred_element_type=jnp.float32)
        o = _softmax_pv([(s_w, v_ref[pl.ds(ks, kb), :]), (s_c, vc)])
        o_ref[pl.ds(q0, qb), :] = o.astype(o_ref.dtype)
        return carry

    lax.fori_loop(0, nblk, body, 0)


def _neighbourhood_attn(qk, p, qk_c, p_c, bias):
    n = qk.shape[0]
    m = qk_c.shape[0]

    def full(col0):
        return pl.BlockSpec((n, HEAD_DIM), lambda h: (0, col0 + h))

    def ctx(col0):
        return pl.BlockSpec((m, HEAD_DIM), lambda h: (0, col0 + h))

    return pl.pallas_call(
        _na_kernel,
        grid=(C_HEADS,),
        in_specs=[full(QK_CQ), full(QK_CK), full(P_CV), ctx(QK_CK), ctx(P_CV),
                  pl.BlockSpec((1, 3, NA_QROWS * GRID_W, NA_KROWS * GRID_W), lambda h: (h, 0, 0, 0))],
        out_specs=pl.BlockSpec((n, HEAD_DIM), lambda h: (0, h)),
        out_shape=jax.ShapeDtypeStruct((n, C_WIDTH), jnp.bfloat16),
        compiler_params=pltpu.CompilerParams(
            dimension_semantics=("parallel",), vmem_limit_bytes=VMEM_LIMIT_BYTES),
        name="neighbourhood_attn",
    )(qk, qk, p, qk_c, p_c, bias)


def _swa_kernel(q_ref, k_ref, v_ref, kc_ref, vc_ref, sink_ref, o_ref):
    n = q_ref.shape[0]
    nblk = n // SWA_QB
    nt = (((1,), (1,)), ((), ()))
    kc = kc_ref[...]
    vc = vc_ref[...]
    kv = pl.program_id(0)
    sink = jnp.concatenate(
        [jnp.broadcast_to(sink_ref[pl.ds(kv * D_GROUP + g, 1), 0:1], (SWA_QB, 1)) for g in range(D_GROUP)], axis=0)
    q_off = jnp.bitwise_and(lax.broadcasted_iota(jnp.int32, (D_GROUP * SWA_QB, SWA_KB), 0), SWA_QB - 1)
    k_off = lax.broadcasted_iota(jnp.int32, (D_GROUP * SWA_QB, SWA_KB), 1)
    rel = q_off - k_off

    def body(b, carry):
        q0 = pl.multiple_of(b * SWA_QB, SWA_QB)
        ks = pl.multiple_of(jnp.clip(q0 - WINDOW, 0, n - SWA_KB), LANES)
        q = jnp.concatenate([q_ref[pl.ds(q0, SWA_QB), g * HEAD_DIM:(g + 1) * HEAD_DIM] for g in range(D_GROUP)], axis=0)
        s_w = lax.dot_general(q, k_ref[pl.ds(ks, SWA_KB), :], nt, preferred_element_type=jnp.float32)
        s_w = jnp.where(jnp.abs(rel + (q0 - ks)) <= WINDOW, s_w, NEG_INF)
        s_c = lax.dot_general(q, kc, nt, preferred_element_type=jnp.float32)
        o = _softmax_pv([(s_w, v_ref[pl.ds(ks, SWA_KB), :]), (s_c, vc)], extra_logit=sink)
        for g in range(D_GROUP):
            o_ref[pl.ds(q0, SWA_QB), g * HEAD_DIM:(g + 1) * HEAD_DIM] = o[g * SWA_QB:(g + 1) * SWA_QB].astype(o_ref.dtype)
        return carry

    lax.fori_loop(0, nblk, body, 0)


def _window_attn(qk, p, qk_c, p_c, sink_rows):
    n = qk.shape[0]
    m = qk_c.shape[0]
    gw = D_GROUP * HEAD_DIM
    return pl.pallas_call(
        _swa_kernel,
        grid=(D_KV_HEADS,),
        in_specs=[pl.BlockSpec((n, gw), lambda kv: (0, QK_DQ // D_GROUP + kv)),
                  pl.BlockSpec((n, HEAD_DIM), lambda kv: (0, QK_DK + kv)),
                  pl.BlockSpec((n, HEAD_DIM), lambda kv: (0, P_DV + kv)),
                  pl.BlockSpec((m, HEAD_DIM), lambda kv: (0, QK_DK + kv)),
                  pl.BlockSpec((m, HEAD_DIM), lambda kv: (0, P_DV + kv)),
                  pl.BlockSpec((D_HEADS, LANES), lambda kv: (0, 0))],
        out_specs=pl.BlockSpec((n, gw), lambda kv: (0, kv)),
        out_shape=jax.ShapeDtypeStruct((n, D_WIDTH), jnp.bfloat16),
        compiler_params=pltpu.CompilerParams(
            dimension_semantics=("parallel",), vmem_limit_bytes=VMEM_LIMIT_BYTES),
        name="window_attn",
    )(qk, qk, p, qk_c, p_c, sink_rows)


def _mixer_cd(h_c, h_l, w_in, qk_g, rpb, sink, w_out):
    n = h_l.shape[0]
    m = h_c.shape[0]
    scale = HEAD_DIM ** -0.5
    p_l = _matmul(h_l, w_in, jnp.bfloat16)
    p_c = _matmul(h_c, w_in[:, :CD_KV_COLS], jnp.bfloat16)
    gains = jnp.stack([qk_g[0] * scale, qk_g[1], qk_g[2] * scale, qk_g[3]]).astype(jnp.float32)[:, None, :]
    cos, sin = _rope_tables(n)
    cs_l = (np.stack([np.ones_like(cos), cos]), np.stack([np.zeros_like(sin), sin]))
    cs_c = (np.ones((1, m, HEAD_DIM), np.float32), np.zeros((1, m, HEAD_DIM), np.float32))
    src = list(range(P_CK, P_CK + 8)) + [P_DK, P_DK + 1] + list(range(P_CQ, P_CQ + 8)) + [P_CQ, P_CQ] + list(range(P_DQ, P_DQ + 8))
    gain = [1] * 8 + [3] * 2 + [0] * 8 + [0, 0] + [2] * 8
    rope = [0] * 8 + [1] * 2 + [0] * 8 + [0, 0] + [1] * 8
    tbl_l = jnp.asarray(np.array([src, gain, rope], np.int32))
    tbl_c = jnp.asarray(np.array([src[:10], gain[:10], [0] * 10], np.int32))
    qk_l = _qk_prep(p_l, tbl_l, gains, jnp.asarray(cs_l[0]), jnp.asarray(cs_l[1]))
    qk_c = _qk_prep(p_c, tbl_c, gains, jnp.asarray(cs_c[0]), jnp.asarray(cs_c[1]))
    c_out = _neighbourhood_attn(qk_l, p_l, qk_c, p_c, _na_bias(rpb, n))
    sink_rows = jnp.broadcast_to(sink.astype(jnp.float32)[:, None], (D_HEADS, LANES))
    d_out = _window_attn(qk_l, p_l, qk_c, p_c, sink_rows)
    return _matmul(jnp.concatenate([c_out, d_out], axis=-1), w_out)


SCAN_L = 128
GLA_SUB = 16
N_SCAN_STATES = 2 * A_HEADS
GATE_LR_LANE = 4 * A_HEADS


def _log_sigmoid(x):
    return jnp.minimum(x, 0.0) - jnp.log(1.0 + jnp.exp(-jnp.abs(x)))


def _tri_mask(rev):
    r = lax.broadcasted_iota(jnp.int32, (SCAN_L, SCAN_L), 0)
    c = lax.broadcasted_iota(jnp.int32, (SCAN_L, SCAN_L), 1)
    return (c >= r) if rev else (c <= r)


def _cumsum_rows(tri, x):
    hi = x.astype(jnp.bfloat16)
    r1 = x - hi.astype(jnp.float32)
    mid = r1.astype(jnp.bfloat16)
    lo = (r1 - mid.astype(jnp.float32)).astype(jnp.bfloat16)
    dot = lambda p: jnp.dot(tri, p, preferred_element_type=jnp.float32)
    return dot(hi) + dot(mid) + dot(lo)


def _conv_kernel(x_ref, w_ref, b_ref, o_ref, *, seg, scale):
    x = x_ref[...].astype(jnp.float32)
    n = x.shape[0]
    t = lax.broadcasted_iota(jnp.int32, x.shape, 0)
    first = (t == 0) | (t == seg)
    last = (t == seg - 1) | (t == n - 1)
    prev = jnp.where(first, 0.0, pltpu.roll(x, 1, 0))
    nxt = jnp.where(last, 0.0, pltpu.roll(x, n - 1, 0))
    y = w_ref[0:1, :] * prev + w_ref[1:2, :] * x + w_ref[2:3, :] * nxt + b_ref[...]
    y = y * jax.nn.sigmoid(y)
    j = pl.program_id(0)
    o_ref[...] = (y * jnp.where(j >= A_HEADS, scale, 1.0)).astype(o_ref.dtype)


def _short_conv_silu(p, conv_w, conv_b, seg):
    t = p.shape[0]
    nblk = 2 * A_WIDTH // LANES
    return pl.pallas_call(
        functools.partial(_conv_kernel, seg=seg, scale=HEAD_DIM ** -0.5),
        grid=(nblk,),
        in_specs=[pl.BlockSpec((t, LANES), lambda j: (0, j)),
                  pl.BlockSpec((8, LANES), lambda j: (0, j)),
                  pl.BlockSpec((1, LANES), lambda j: (0, j))],
        out_specs=pl.BlockSpec((t, LANES), lambda j: (0, j)),
        out_shape=jax.ShapeDtypeStruct((t, 2 * A_WIDTH), jnp.bfloat16),
        compiler_params=pltpu.CompilerParams(
            dimension_semantics=("parallel",), vmem_limit_bytes=VMEM_LIMIT_BYTES),
        name="short_conv_silu",
    )(p, jnp.pad(conv_w, ((0, 8 - CONV_W), (0, 0))), conv_b[None, :])


def _scan_chunk_index(j, n_ctx_chunks, n_chunks, rev):
    if not rev:
        return j
    return jnp.where(j < n_ctx_chunks, n_ctx_chunks - 1 - j, n_chunks + n_ctx_chunks - 1 - j)


def _mlstm_kernel(qf_ref, kf_ref, vf_ref, gf_ref, qb_ref, kb_ref, vb_ref, gb_ref, gbias_ref,
                  yf_ref, yb_ref, c_ref, m_ref):
    @pl.when(pl.program_id(0) == 0)
    def _():
        c_ref[...] = jnp.zeros_like(c_ref)
        m_ref[...] = jnp.zeros_like(m_ref)

    lane = lax.broadcasted_iota(jnp.int32, (SCAN_L, LANES), 1)
    is_forget = jnp.bitwise_and(lane, 2 * A_HEADS - 1) >= A_HEADS
    ones = jnp.ones((SCAN_L, HEAD_DIM), jnp.bfloat16)
    nt = (((1,), (1,)), ((), ()))
    tn = (((0,), (0,)), ((), ()))
    for z, (q_ref, k_ref, v_ref, g_ref, y_ref) in enumerate(
            ((qf_ref, kf_ref, vf_ref, gf_ref, yf_ref), (qb_ref, kb_ref, vb_ref, gb_ref, yb_ref))):
        rev = z == 1
        mask = _tri_mask(rev)
        gates = g_ref[...] + gbias_ref[...]
        gates = jnp.where(is_forget, _log_sigmoid(gates), gates)
        csum = _cumsum_rows(mask.astype(jnp.bfloat16), jnp.where(is_forget, gates, 0.0))
        gates_t = gates.T
        csum_t = csum.T
        end = 0 if rev else SCAN_L - 1
        for h in range(A_HEADS):
            li, lf = z * 2 * A_HEADS + h, z * 2 * A_HEADS + A_HEADS + h
            sl = slice(h * HEAD_DIM, (h + 1) * HEAD_DIM)
            q, k, v = q_ref[:, sl], k_ref[:, sl], v_ref[:, sl]
            b_col, b_row = csum[:, lf:lf + 1], csum_t[lf:lf + 1, :]
            i_col, i_row = gates[:, li:li + 1], gates_t[li:li + 1, :]
            st = z * A_HEADS + h
            m_prev = m_ref[st][0:1, 0:1]
            dlog = jnp.where(mask, b_col - b_row + i_row, NEG_INF)
            m_inter = b_col + m_prev
            m_t = jnp.maximum(m_inter, jnp.max(dlog, axis=-1, keepdims=True))
            s = lax.dot_general(q, k, nt, preferred_element_type=jnp.float32)
            sc = (s * jnp.exp(dlog - m_t)).astype(jnp.bfloat16)
            w_inter = jnp.exp(m_inter - m_t)
            v_ext = jnp.concatenate([v, ones], axis=1)
            r = (jnp.dot(sc, v_ext, preferred_element_type=jnp.float32)
                 + w_inter * jnp.dot(q, c_ref[st].astype(jnp.bfloat16), preferred_element_type=jnp.float32))
            num, den = r[:, :HEAD_DIM], r[:, HEAD_DIM:]
            y_ref[:, sl] = num / jnp.maximum(jnp.abs(den), jnp.exp(-m_t))
            m_new = m_t[end:end + 1, :]
            b_end = b_col[end:end + 1, :]
            w_end = jnp.exp(b_end - b_col + i_col - m_new)
            decay = jnp.exp(b_end + m_prev - m_new)
            kw = (k.astype(jnp.float32) * w_end).astype(jnp.bfloat16)
            c_ref[st] = decay * c_ref[st] + lax.dot_general(kw, v_ext, tn, preferred_element_type=jnp.float32)
            m_ref[st] = jnp.broadcast_to(m_new, (8, LANES))


def _scan_specs(n_ctx_chunks, n_chunks, width, col0, rev):
    return pl.BlockSpec((SCAN_L, width),
                        lambda j: (_scan_chunk_index(j, n_ctx_chunks, n_chunks, rev), col0))


def _mlstm(qk, p, gates, gate_bias, seg):
    t = qk.shape[0]
    nck = t // SCAN_L
    ncc = seg // SCAN_L
    ins, specs = [], []
    for rev in (False, True):
        ins += [qk, qk, p, gates]
        specs += [_scan_specs(ncc, nck, A_WIDTH, 0, rev), _scan_specs(ncc, nck, A_WIDTH, 1, rev),
                  _scan_specs(ncc, nck, A_WIDTH, 2, rev), _scan_specs(ncc, nck, LANES, 0, rev)]
    return pl.pallas_call(
        _mlstm_kernel,
        grid=(nck,),
        in_specs=specs + [pl.BlockSpec((1, LANES), lambda j: (0, 0))],
        out_specs=[_scan_specs(ncc, nck, A_WIDTH, 0, False), _scan_specs(ncc, nck, A_WIDTH, 0, True)],
        out_shape=[jax.ShapeDtypeStruct((t, A_WIDTH), jnp.float32)] * 2,
        scratch_shapes=[pltpu.VMEM((N_SCAN_STATES, HEAD_DIM, 2 * HEAD_DIM), jnp.float32),
                        pltpu.VMEM((N_SCAN_STATES, 8, LANES), jnp.float32)],
        compiler_params=pltpu.CompilerParams(
            dimension_semantics=("arbitrary",), vmem_limit_bytes=VMEM_LIMIT_BYTES),
        name="mlstm_scan",
    )(*ins, gate_bias)


def _gla_kernel(qf_ref, kf_ref, vf_ref, gf_ref, qb_ref, kb_ref, vb_ref, gb_ref, w2_ref, ab_ref,
                of_ref, ob_ref, s_ref):
    @pl.when(pl.program_id(0) == 0)
    def _():
        s_ref[...] = jnp.zeros_like(s_ref)

    nsub = SCAN_L // GLA_SUB
    row = lax.broadcasted_iota(jnp.int32, (SCAN_L, LANES), 0)
    lane = lax.broadcasted_iota(jnp.int32, (SCAN_L, LANES), 1)
    head_lane = (lane < B_KEY_DIM, lane >= B_KEY_DIM)
    r_sq = lax.broadcasted_iota(jnp.int32, (SCAN_L, SCAN_L), 0)
    c_sq = lax.broadcasted_iota(jnp.int32, (SCAN_L, SCAN_L), 1)
    sub_of_row = jnp.right_shift(row, 4)
    in_sub = jnp.bitwise_and(row, GLA_SUB - 1)
    sel_r = lax.broadcasted_iota(jnp.int32, (LANES, 2 * LANES), 0)
    sel_c = lax.broadcasted_iota(jnp.int32, (LANES, 2 * LANES), 1)
    head_sum = ((sel_r < B_KEY_DIM) == (sel_c < LANES)).astype(jnp.bfloat16)
    nt = (((1,), (1,)), ((), ()))
    tn = (((0,), (0,)), ((), ()))
    for z, (q_ref, k_ref, v_ref, g_ref, o_ref) in enumerate(
            ((qf_ref, kf_ref, vf_ref, gf_ref, of_ref), (qb_ref, kb_ref, vb_ref, gb_ref, ob_ref))):
        rev = z == 1
        tri = _tri_mask(rev).astype(jnp.bfloat16)
        lr = g_ref[...].astype(jnp.bfloat16)
        end = 0 if rev else SCAN_L - 1
        for pr in range(B_HEADS // 2):
            ls = slice(pr * LANES, (pr + 1) * LANES)
            za = jnp.dot(lr, w2_ref[z, :, ls], preferred_element_type=jnp.float32) + ab_ref[z:z + 1, ls]
            b = _cumsum_rows(tri, _log_sigmoid(za) * (1.0 / GLA_TAU))
            q = q_ref[:, ls].astype(jnp.float32) * (B_KEY_DIM ** -0.5)
            k = k_ref[:, ls].astype(jnp.float32)
            v = v_ref[:, 2 * pr * HEAD_DIM:(2 * pr + 2) * HEAD_DIM]
            b_ref_rows = jnp.zeros_like(b)
            for j in range(nsub):
                e = j * GLA_SUB if rev else j * GLA_SUB + GLA_SUB - 1
                b_ref_rows = jnp.where(sub_of_row == j, b[e:e + 1, :], b_ref_rows)
            k_sub = (k * jnp.exp(b_ref_rows - b)).astype(jnp.bfloat16)
            k_sub_h = [jnp.where(head_lane[hh], k_sub, jnp.zeros_like(k_sub)) for hh in range(2)]
            attn =[jnp.zeros((SCAN_L, SCAN_L), jnp.float32), jnp.zeros((SCAN_L, SCAN_L), jnp.float32)]
            for j in range(nsub):
                e = j * GLA_SUB if rev else j * GLA_SUB + GLA_SUB - 1
                if j == (0 if rev else nsub - 1):
                    continue
                later = (sub_of_row < j) if rev else (sub_of_row > j)
                qj = (q * jnp.exp(jnp.where(later, b - b[e:e + 1, :], NEG_INF))).astype(jnp.bfloat16)
                for hh in range(2):
                    kj = jnp.where(sub_of_row == j, k_sub_h[hh], jnp.zeros_like(k_sub))
                    attn[hh] = attn[hh] + lax.dot_general(qj, kj, nt, preferred_element_type=jnp.float32)
            for d in range(GLA_SUB):
                shift = (SCAN_L - d) % SCAN_L if rev else d
                k_s = pltpu.roll(k, shift, 0) if d else k
                b_s = pltpu.roll(b, shift, 0) if d else b
                ok_row = (in_sub <= GLA_SUB - 1 - d) if rev else (in_sub >= d)
                f = (q * k_s * jnp.exp(jnp.where(ok_row, b - b_s, NEG_INF))).astype(jnp.bfloat16)
                red = jnp.dot(f, head_sum, preferred_element_type=jnp.float32)
                on_diag = (c_sq == r_sq + d) if rev else (c_sq == r_sq - d)
                for hh in range(2):
                    attn[hh] = attn[hh] + jnp.where(on_diag, red[:, hh * LANES:(hh + 1) * LANES], 0.0)
            q_in = q * jnp.exp(b)
            b_end = b[end:end + 1, :]
            k_out = (k * jnp.exp(b_end - b)).astype(jnp.bfloat16)
            st = z * (B_HEADS // 2) + pr
            s_t = s_ref[st]
            s_bf = s_t.astype(jnp.bfloat16)
            for hh in range(2):
                vs = slice(hh * HEAD_DIM, (hh + 1) * HEAD_DIM)
                qh = jnp.where(head_lane[hh], q_in, 0.0).astype(jnp.bfloat16)
                o = (jnp.dot(attn[hh].astype(jnp.bfloat16), v[:, vs], preferred_element_type=jnp.float32)
                     + lax.dot_general(qh, s_bf[vs, :], nt, preferred_element_type=jnp.float32))
                o_ref[:, (2 * pr + hh) * HEAD_DIM:(2 * pr + hh + 1) * HEAD_DIM] = o
            s_ref[st] = s_t * jnp.exp(b_end) + lax.dot_general(v, k_out, tn, preferred_element_type=jnp.float32)


def _gla(p, gates, w2, alpha_b, seg):
    t = p.shape[0]
    nck = t // SCAN_L
    ncc = seg // SCAN_L
    ins, specs = [], []
    for rev in (False, True):
        ins += [p, p, p, gates]
        specs += [_scan_specs(ncc, nck, B_KEY_WIDTH, P_BQ // B_KEY_WIDTH, rev),
                  _scan_specs(ncc, nck, B_KEY_WIDTH, P_BK // B_KEY_WIDTH, rev),
                  _scan_specs(ncc, nck, B_VAL_WIDTH, P_BV // B_VAL_WIDTH, rev),
                  _scan_specs(ncc, nck, LANES, 0, rev)]
    return pl.pallas_call(
        _gla_kernel,
        grid=(nck,),
        in_specs=specs + [pl.BlockSpec((2, LANES, B_KEY_WIDTH), lambda j: (0, 0, 0)),
                          pl.BlockSpec((2, B_KEY_WIDTH), lambda j: (0, 0))],
        out_specs=[_scan_specs(ncc, nck, B_VAL_WIDTH, 0, False), _scan_specs(ncc, nck, B_VAL_WIDTH, 0, True)],
        out_shape=[jax.ShapeDtypeStruct((t, B_VAL_WIDTH), jnp.float32)] * 2,
        scratch_shapes=[pltpu.VMEM((B_HEADS, 2 * HEAD_DIM, LANES), jnp.float32)],
        compiler_params=pltpu.CompilerParams(
            dimension_semantics=("arbitrary",), vmem_limit_bytes=VMEM_LIMIT_BYTES),
        name="gla_scan",
    )(*ins, w2, alpha_b)


def _head_out_kernel(af_ref, ab_ref, bf_ref, bb_ref, gate_ref, g_ref, o_ref):
    is_gla = pl.program_id(1) >= A_HEADS
    y = jnp.where(is_gla, bf_ref[...] + bb_ref[...], af_ref[...] + ab_ref[...])
    yn = y * lax.rsqrt(jnp.mean(y * y, axis=-1, keepdims=True) + EPS) * g_ref[0]
    gate = gate_ref[...].astype(jnp.float32)
    o_ref[...] = (yn * jax.nn.sigmoid(gate) * jnp.where(is_gla, gate, 1.0)).astype(o_ref.dtype)


def _head_out(ya, yb, p, head_g):
    t = p.shape[0]
    bt = _row_tile(t)
    nh = A_HEADS + B_HEADS
    a_spec = pl.BlockSpec((bt, HEAD_DIM), lambda i, j: (i, jnp.minimum(j, A_HEADS - 1)))
    b_spec = pl.BlockSpec((bt, HEAD_DIM), lambda i, j: (i, jnp.maximum(j - A_HEADS, 0)))

    def gate_col(i, j):
        return (i, jnp.where(j < A_HEADS, P_AO + j, P_BR + j - A_HEADS))

    return pl.pallas_call(
        _head_out_kernel,
        grid=(t // bt, nh),
        in_specs=[a_spec, a_spec, b_spec, b_spec,
                  pl.BlockSpec((bt, HEAD_DIM), gate_col),
                  pl.BlockSpec((1, 1, HEAD_DIM), lambda i, j: (j // A_HEADS, 0, 0))],
        out_specs=pl.BlockSpec((bt, HEAD_DIM), lambda i, j: (i, j)),
        out_shape=jax.ShapeDtypeStruct((t, nh * HEAD_DIM), jnp.bfloat16),
        compiler_params=pltpu.CompilerParams(
            dimension_semantics=("parallel", "arbitrary"), vmem_limit_bytes=VMEM_LIMIT_BYTES),
        name="head_out",
    )(ya[0], ya[1], yb[0], yb[1], p, head_g.astype(jnp.float32)[:, None, :])


P_AO = 24
P_BQ, P_BK, P_BV = 4096, 4608, 5120
P_BR = 48


def _mixer_ab(h, w_in, conv_w, conv_b, gate_b, alpha_w2, alpha_b, head_g, w_out, seg):
    bf = jnp.bfloat16
    cols = np.cumsum((0,) + AB_SPLITS)
    pick = lambda *ids: jnp.concatenate([w_in[:, cols[i]:cols[i + 1]] for i in ids], axis=1)
    w_main = pick(0, 1, 2, 3, 5, 6, 7, 8).astype(bf)
    w_gate = jnp.pad(pick(4, 9), ((0, 0), (0, LANES - 4 * A_HEADS - 2 * GLA_RANK))).astype(bf)
    hb = h
    p = _matmul(hb, w_main, bf)
    gates = _matmul(hb, w_gate, jnp.float32)
    gate_bias = jnp.pad(gate_b.reshape(1, -1).astype(jnp.float32), ((0, 0), (0, LANES - 4 * A_HEADS)))
    w2 = jnp.zeros((2, LANES, B_KEY_WIDTH), jnp.float32)
    for z in range(2):
        w2 = w2.at[z, GATE_LR_LANE + z * GLA_RANK:GATE_LR_LANE + (z + 1) * GLA_RANK].set(alpha_w2[z])
    qk = _short_conv_silu(p, conv_w, conv_b, seg)
    ya = _mlstm(qk, p, gates, gate_bias, seg)
    yb = _gla(p, gates, w2.astype(bf), alpha_b.astype(jnp.float32), seg)
    return _matmul(_head_out(ya, yb, p, head_g), w_out)


MOD_ROWS = 256


def _res_mod_kernel(x_ref, y_ref, p_ref, xo_ref, h_ref):
    p = p_ref[0]
    x = x_ref[...] + p[0:1, :] * y_ref[...]
    xo_ref[...] = x
    r = x * lax.rsqrt(jnp.mean(x * x, axis=-1, keepdims=True) + EPS) * p[1:2, :]
    h_ref[...] = (r * (1.0 + p[3:4, :]) + p[2:3, :]).astype(h_ref.dtype)


def _mod_kernel(x_ref, p_ref, h_ref):
    p = p_ref[0]
    x = x_ref[...]
    r = x * lax.rsqrt(jnp.mean(x * x, axis=-1, keepdims=True) + EPS) * p[1:2, :]
    h_ref[...] = (r * (1.0 + p[3:4, :]) + p[2:3, :]).astype(h_ref.dtype)


def _mod_params(gate_c, gate_l, gain, shift_c, shift_l, scale_c, scale_l):
    rows = lambda g, sh, sc: jnp.concatenate(
        [g.reshape(1, -1), gain.reshape(1, -1), sh.reshape(1, -1), sc.reshape(1, -1),
         jnp.zeros((4, gain.shape[-1]), jnp.float32)], axis=0)
    return jnp.stack([rows(gate_c, shift_c, scale_c), rows(gate_l, shift_l, scale_l)]).astype(jnp.float32)


def _res_mod(x, y, params, seg, row0=0, n_rows=None):
    d = x.shape[1]
    n_rows = x.shape[0] - row0 if n_rows is None else n_rows
    bt = MOD_ROWS
    off, seg_blocks = row0 // bt, seg // bt
    row_spec = pl.BlockSpec((bt, d), lambda i: (i + off, 0))
    out_spec = pl.BlockSpec((bt, d), lambda i: (i, 0))
    p_spec = pl.BlockSpec((1, 8, d), lambda i: (jnp.where(i + off < seg_blocks, 0, 1), 0, 0))
    cp = pltpu.CompilerParams(dimension_semantics=("parallel",), vmem_limit_bytes=VMEM_LIMIT_BYTES)
    h_shape = jax.ShapeDtypeStruct((n_rows, d), jnp.bfloat16)
    if y is None:
        return pl.pallas_call(_mod_kernel, grid=(n_rows // bt,), in_specs=[row_spec, p_spec], out_specs=out_spec,
                              out_shape=h_shape, compiler_params=cp, name="modulate")(x, params)
    return pl.pallas_call(
        _res_mod_kernel, grid=(n_rows // bt,), in_specs=[row_spec, row_spec, p_spec],
        out_specs=[out_spec, out_spec],
        out_shape=[jax.ShapeDtypeStruct((n_rows, d), jnp.float32), h_shape],
        compiler_params=cp, name="residual_modulate")(x, y, params)


def _modulation_kernel(c_ref, w_ref, b_ref, o_ref):
    act = c_ref[...]
    act = (act * jax.nn.sigmoid(act)).astype(jnp.bfloat16)
    o_ref[0] = jnp.dot(act, w_ref[0].astype(jnp.bfloat16), preferred_element_type=jnp.float32) + b_ref[0]


def _modulation(c, c_ctx, mod_w, mod_b):
    nl, d, n = mod_w.shape
    cond = jnp.concatenate([c_ctx.reshape(1, d), c.reshape(1, d), jnp.zeros((6, d), jnp.float32)], axis=0)
    bn = n // 8
    return pl.pallas_call(
        _modulation_kernel,
        grid=(nl, n // bn),
        in_specs=[pl.BlockSpec((8, d), lambda l, j: (0, 0)),
                  pl.BlockSpec((1, d, bn), lambda l, j: (l, 0, j)),
                  pl.BlockSpec((1, 1, bn), lambda l, j: (l, 0, j))],
        out_specs=pl.BlockSpec((1, 8, bn), lambda l, j: (l, 0, j)),
        out_shape=jax.ShapeDtypeStruct((nl, 8, n), jnp.float32),
        compiler_params=pltpu.CompilerParams(
            dimension_semantics=("parallel", "parallel"), vmem_limit_bytes=VMEM_LIMIT_BYTES),
        name="modulation",
    )(cond, mod_w, mod_b[:, None, :])


def kernel(x, c, ctx, c_ctx, mod_w, mod_b, norm_g, ab_w_in, ab_conv_w, ab_conv_b, ab_gate_b, ab_alpha_w2,
           ab_alpha_b, ab_head_g, ab_w_out, cd_w_in, cd_qk_g, cd_rpb, cd_sink, cd_w_out, peer_w_q,
           peer_sub_keys, peer_u, peer_v):
    assert DEPTH == 2
    bf = jnp.bfloat16
    m = ctx.shape[1]
    xs = jnp.concatenate([ctx[0], x[0]], axis=0)
    ub = _quant_rows(peer_u)
    vb = _quant_rows(peer_v)
    mod_all = _modulation(c, c_ctx, mod_w, mod_b)
    mods = [(jnp.split(mod_all[l, 0], 6), jnp.split(mod_all[l, 1], 6)) for l in range(DEPTH)]
    one = jnp.ones_like(mods[0][0][0])

    def params(l, sub, gate_c, gate_l):
        mc, ml = mods[l]
        return _mod_params(gate_c, gate_l, norm_g[l, sub], mc[3 * sub], ml[3 * sub], mc[3 * sub + 1], ml[3 * sub + 1])

    (mc0, ml0), (mc1, ml1) = mods
    h = _res_mod(xs, None, params(0, 0, one, one), m)
    y = _mixer_ab(h, ab_w_in[0], ab_conv_w[0], ab_conv_b[0], ab_gate_b[0], ab_alpha_w2[0], ab_alpha_b[0],
                  ab_head_g[0], ab_w_out[0].astype(bf), m)
    xs, h = _res_mod(xs, y, params(0, 1, mc0[2], ml0[2]), m)
    y = _peer(h, peer_w_q[0].astype(bf), peer_sub_keys[0].astype(bf), ub, vb, 0)
    p10 = params(1, 0, mc0[5], ml0[5])
    _, hc = _res_mod(xs, y, p10, m, 0, m)
    xl, hl = _res_mod(xs, y, p10, m, m)
    y = _mixer_cd(hc, hl, cd_w_in[0].astype(bf), cd_qk_g[0], cd_rpb[0], cd_sink[0], cd_w_out[0].astype(bf))
    xl, h = _res_mod(xl, y, params(1, 1, ml1[2], ml1[2]), 0)
    y = _peer(h, peer_w_q[1].astype(bf), peer_sub_keys[1].astype(bf), ub, vb, 1)
    return (xl + ml1[5] * y)[None]
```

```python
import functools

import numpy as np
import jax
import jax.numpy as jnp
from jax import lax
from jax.experimental import pallas as pl
from jax.experimental.pallas import tpu as pltpu

D_MODEL = 2048
SEQ = 8192
DEPTH = 2
GRID_W = 64
CTX_LEN = 256
HEAD_DIM = 128
N_GROUP_HEADS = 8
EPS = 1e-6
A_HEADS = 8
A_WIDTH = 1024
CONV_W = 3
B_HEADS = 8
B_KEY_DIM = 64
B_KEY_WIDTH = 512
B_VAL_WIDTH = 1024
GLA_RANK = 16
GLA_TAU = 16.0
SCAN_CHUNK = 64
C_HEADS = 8
C_WIDTH = 1024
WIN_H = 8
WIN_W = 16
D_HEADS = 8
D_KV_HEADS = 2
D_GROUP = 4
D_WIDTH = 1024
D_KV_WIDTH = 256
WINDOW = 128
ROPE_BASE = 10000.0
PEER_HEADS = 8
PEER_NKEYS = 128
PEER_EXPERTS = PEER_NKEYS * PEER_NKEYS
PEER_QDIM = 256
PEER_TOPK = 16
PEER_SLOTS = PEER_HEADS * PEER_TOPK
AB_SPLITS = (A_WIDTH, A_WIDTH, A_WIDTH, A_WIDTH, 4 * A_HEADS,
             B_KEY_WIDTH, B_KEY_WIDTH, B_VAL_WIDTH, B_VAL_WIDTH, 2 * GLA_RANK)
CD_SPLITS = (C_WIDTH, C_WIDTH, D_KV_WIDTH, D_KV_WIDTH, C_WIDTH, D_WIDTH)
CD_KV_COLS = sum(CD_SPLITS[:4])

VMEM_LIMIT_BYTES = 56 * 1024 * 1024
LANES = 128


def _row_tile(m):
    for t in (1024, 768, 512, 256, 128):
        if m % t == 0:
            return t
    raise ValueError(f"unsupported row count {m}")


def _mm_kernel(x_ref, w_ref, o_ref):
    o_ref[...] = jnp.dot(x_ref[...], w_ref[...], preferred_element_type=jnp.float32).astype(o_ref.dtype)


def _matmul(x, w, out_dtype=jnp.float32):
    m, k = x.shape
    n = w.shape[1]
    bn = min(512, n)
    n_pad = -(-n // bn) * bn
    if n_pad != n:
        w = jnp.pad(w, ((0, 0), (0, n_pad - n)))
    bm = _row_tile(m)
    out = pl.pallas_call(
        _mm_kernel,
        grid=(m // bm, n_pad // bn),
        in_specs=[pl.BlockSpec((bm, k), lambda i, j: (i, 0)),
                  pl.BlockSpec((k, bn), lambda i, j: (0, j))],
        out_specs=pl.BlockSpec((bm, bn), lambda i, j: (i, j)),
        out_shape=jax.ShapeDtypeStruct((m, n_pad), out_dtype),
        compiler_params=pltpu.CompilerParams(
            dimension_semantics=("parallel", "arbitrary"), vmem_limit_bytes=VMEM_LIMIT_BYTES),
        name="matmul",
    )(x.astype(jnp.bfloat16), w.astype(jnp.bfloat16))
    return out[:, :n] if n_pad != n else out


ROUTE_TOKENS = (768, 512, 256, 128)
GATE_TOKENS = 128
NEG_INF = float("-inf")


def _top16(s, key, big):
    slot = lax.broadcasted_iota(jnp.int32, (PEER_TOPK, s.shape[1]), 0)
    vals = jnp.zeros((PEER_TOPK, s.shape[1]), jnp.float32)
    keys = jnp.zeros((PEER_TOPK, s.shape[1]), jnp.float32)
    for it in range(PEER_TOPK):
        m = jnp.max(s, axis=0, keepdims=True)
        k = jnp.min(jnp.where(s == m, key, big), axis=0, keepdims=True)
        vals = jnp.where(slot == it, m, vals)
        keys = jnp.where(slot == it, k, keys)
        s = jnp.where(key == k, NEG_INF, s)
    return vals, keys


def _route_kernel(h_ref, wq_ref, k_ref, i1_ref, i2_ref, g_ref):
    bt = h_ref.shape[0]
    q = jnp.dot(h_ref[...], wq_ref[...], preferred_element_type=jnp.float32).astype(jnp.bfloat16)
    nt = (((1,), (1,)), ((), ()))
    s1 = lax.dot_general(k_ref[0, 0], q[:, :PEER_NKEYS], nt, preferred_element_type=jnp.float32)
    s2 = lax.dot_general(k_ref[0, 1], q[:, PEER_NKEYS:], nt, preferred_element_type=jnp.float32)
    row = lax.broadcasted_iota(jnp.int32, (PEER_NKEYS, bt), 0).astype(jnp.float32)
    v1, r1 = _top16(s1, row, float(PEER_NKEYS))
    v2, r2 = _top16(s2, row, float(PEER_NKEYS))
    i16 = lax.broadcasted_iota(jnp.int32, (16, bt), 0).astype(jnp.float32)
    i8 = lax.broadcasted_iota(jnp.int32, (8, bt), 0).astype(jnp.float32)
    cand = [v1[0:1] + v2]
    flat = [i16]
    for a in range(1, 8):
        cand.append(v1[a:a + 1] + v2[0:8])
        flat.append(i8 + float(a * PEER_TOPK))
    cand.append(v1[8:16] + v2[0:1])
    flat.append((i8 + 8.0) * float(PEER_TOPK))
    best, key = _top16(jnp.concatenate(cand, axis=0), jnp.concatenate(flat, axis=0), float(PEER_TOPK * PEER_TOPK))
    key = key.astype(jnp.int32)
    a_sel = jnp.right_shift(key, 4)
    b_sel = jnp.bitwise_and(key, PEER_TOPK - 1)
    e1 = jnp.zeros_like(r1)
    e2 = jnp.zeros_like(r2)
    for r in range(PEER_TOPK):
        e1 = jnp.where(a_sel == r, r1[r:r + 1], e1)
        e2 = jnp.where(b_sel == r, r2[r:r + 1], e2)
    e1 = e1.astype(jnp.int32)
    e2 = e2.astype(jnp.int32)
    ex = jnp.exp(best - best[0:1])
    i1_ref[...] = e1
    i2_ref[...] = e2
    g_ref[...] = ex / jnp.sum(ex, axis=0, keepdims=True)


def _route(hb, wq, keys):
    t, d = hb.shape
    bt = next(b for b in ROUTE_TOKENS if t % b == 0)
    slot_spec = pl.BlockSpec((PEER_TOPK, bt), lambda i, h: (h, i))
    return pl.pallas_call(
        _route_kernel,
        grid=(t // bt, PEER_HEADS),
        in_specs=[pl.BlockSpec((bt, d), lambda i, h: (i, 0)),
                  pl.BlockSpec((d, PEER_QDIM), lambda i, h: (0, h)),
                  pl.BlockSpec((1, 2, PEER_NKEYS, PEER_QDIM // 2), lambda i, h: (h, 0, 0, 0))],
        out_specs=[slot_spec, slot_spec, slot_spec],
        out_shape=[jax.ShapeDtypeStruct((PEER_SLOTS, t), jnp.int32),
                   jax.ShapeDtypeStruct((PEER_SLOTS, t), jnp.int32),
                   jax.ShapeDtypeStruct((PEER_SLOTS, t), jnp.float32)],
        compiler_params=pltpu.CompilerParams(
            dimension_semantics=("parallel", "arbitrary"), vmem_limit_bytes=VMEM_LIMIT_BYTES),
        name="peer_route",
    )(hb, wq, keys)


GATE_GROUP = 16


def _gate_matrix_kernel(i1_ref, i2_ref, g_ref, o_ref, i1_t, g_t, stage):
    tb = o_ref.shape[1]
    i1_t[...] = i1_ref[...].T
    g_t[...] = g_ref[...].T
    row_id = lax.broadcasted_iota(jnp.int32, (PEER_NKEYS, PEER_SLOTS), 0)
    lane_id = lax.broadcasted_iota(jnp.int32, (PEER_SLOTS, PEER_NKEYS), 1)
    zero = jnp.zeros((PEER_NKEYS, PEER_SLOTS), jnp.bfloat16)

    for tp in range(tb // 2):
        ta = 2 * tp
        tb_ = ta + 1
        ca = jnp.where(row_id == i1_t[ta:ta + 1, :], g_t[ta:ta + 1, :], 0.0).astype(jnp.bfloat16)
        cb = jnp.where(row_id == i1_t[tb_:tb_ + 1, :], g_t[tb_:tb_ + 1, :], 0.0).astype(jnp.bfloat16)
        oa = jnp.where(i2_ref[:, ta:ta + 1] == lane_id, 1.0, 0.0).astype(jnp.bfloat16)
        ob = jnp.where(i2_ref[:, tb_:tb_ + 1] == lane_id, 1.0, 0.0).astype(jnp.bfloat16)
        lhs = jnp.concatenate([ca, cb], axis=1)
        rhs = jnp.concatenate([jnp.concatenate([oa, zero], axis=1),
                               jnp.concatenate([zero, ob], axis=1)], axis=0)
        out = jnp.dot(lhs, rhs, preferred_element_type=jnp.float32)
        stage[ta % GATE_GROUP] = out[:, :PEER_NKEYS]
        stage[tb_ % GATE_GROUP] = out[:, PEER_NKEYS:]
        if tb_ % GATE_GROUP == GATE_GROUP - 1:
            t0 = tb_ + 1 - GATE_GROUP
            o_ref[:, t0:t0 + GATE_GROUP, :] = jnp.swapaxes(stage[...], 0, 1).astype(o_ref.dtype)


def _gate_matrix(i1, i2, g):
    t = i1.shape[1]
    tb = GATE_TOKENS
    return pl.pallas_call(
        _gate_matrix_kernel,
        grid=(t // tb,),
        in_specs=[pl.BlockSpec((PEER_SLOTS, tb), lambda i: (0, i))] * 3,
        out_specs=pl.BlockSpec((PEER_NKEYS, tb, PEER_NKEYS), lambda i: (0, i, 0)),
        out_shape=jax.ShapeDtypeStruct((PEER_NKEYS, t, PEER_NKEYS), jnp.bfloat16),
        scratch_shapes=[pltpu.VMEM((tb, PEER_SLOTS), jnp.int32),
                        pltpu.VMEM((tb, PEER_SLOTS), jnp.float32),
                        pltpu.VMEM((GATE_GROUP, PEER_NKEYS, PEER_NKEYS), jnp.float32)],
        compiler_params=pltpu.CompilerParams(
            dimension_semantics=("parallel",), vmem_limit_bytes=VMEM_LIMIT_BYTES),
        name="peer_gate_matrix",
    )(i1, i2, g)


def _gelu_tanh(x):
    return 0.5 * x * (1.0 + jnp.tanh(0.7978845608028654 * (x + 0.044715 * x * x * x)))


FP8 = jnp.float8_e4m3fn
FP8_TARGET = 240.0


def _row_scale(x):
    amax = jnp.max(jnp.abs(x), axis=-1, keepdims=True)
    pos = amax > 0.0
    return jnp.where(pos, FP8_TARGET / amax, 1.0), jnp.where(pos, amax * (1.0 / FP8_TARGET), 1.0)


def _quant_kernel(x_ref, o_ref, s_ref):
    x = x_ref[0]
    scale, inv = _row_scale(x)
    o_ref[0] = (x * scale).astype(o_ref.dtype)
    s_ref[0] = jnp.broadcast_to(inv, s_ref.shape[1:])


def _quant_rows(x):
    nl, r, c = x.shape
    br = 1024
    q, s = pl.pallas_call(
        _quant_kernel,
        grid=(nl, r // br),
        in_specs=[pl.BlockSpec((1, br, c), lambda l, i: (l, i, 0))],
        out_specs=[pl.BlockSpec((1, br, c), lambda l, i: (l, i, 0)),
                   pl.BlockSpec((1, br, LANES), lambda l, i: (l, i, 0))],
        out_shape=[jax.ShapeDtypeStruct(x.shape, FP8), jax.ShapeDtypeStruct((nl, r, LANES), jnp.float32)],
        compiler_params=pltpu.CompilerParams(
            dimension_semantics=("parallel", "parallel"), vmem_limit_bytes=VMEM_LIMIT_BYTES),
        name="quant_rows",
    )(x)
    return q, s[:, :, 0][:, None, :]


def _peer_dense_kernel(h_ref, u_ref, su_ref, v_ref, sv_ref, g_ref, o_ref, h8_ref, sh_ref):
    j = pl.program_id(1)

    @pl.when(j == 0)
    def _():
        h = h_ref[...].astype(jnp.float32)
        scale, inv = _row_scale(h)
        h8_ref[...] = (h * scale).astype(h8_ref.dtype)
        sh_ref[...] = jnp.broadcast_to(inv, sh_ref.shape)

    nt = (((1,), (1,)), ((), ()))
    acc = lax.dot_general(h8_ref[...], u_ref[0], nt, preferred_element_type=jnp.float32)
    act = _gelu_tanh(acc * sh_ref[:, 0:1] * su_ref[0])
    gate = jnp.concatenate([g_ref[r] for r in range(g_ref.shape[0])], axis=1)
    w = act * gate.astype(jnp.float32) * sv_ref[0]
    scale, inv = _row_scale(w)
    contrib = jnp.dot((w * scale).astype(FP8), v_ref[0], preferred_element_type=jnp.float32) * inv

    @pl.when(j == 0)
    def _():
        o_ref[...] = contrib

    @pl.when(j != 0)
    def _():
        o_ref[...] += contrib


def _peer_dense(h, u, v, gmat, layer):
    t, d = h.shape
    (u8, su), (v8, sv) = u, v
    e = v8.shape[1]
    bm = _row_tile(t)
    be = 1024
    tab = pl.BlockSpec((1, be, d), lambda i, j: (layer, j, 0))
    sc = pl.BlockSpec((1, 1, be), lambda i, j: (layer, 0, j))
    return pl.pallas_call(
        _peer_dense_kernel,
        grid=(t // bm, e // be),
        in_specs=[pl.BlockSpec((bm, d), lambda i, j: (i, 0)), tab, sc, tab, sc,
                  pl.BlockSpec((be // PEER_NKEYS, bm, PEER_NKEYS), lambda i, j: (j, i, 0))],
        out_specs=pl.BlockSpec((bm, d), lambda i, j: (i, 0)),
        out_shape=jax.ShapeDtypeStruct((t, d), jnp.float32),
        scratch_shapes=[pltpu.VMEM((bm, d), FP8), pltpu.VMEM((bm, LANES), jnp.float32)],
        compiler_params=pltpu.CompilerParams(
            dimension_semantics=("parallel", "arbitrary"), vmem_limit_bytes=VMEM_LIMIT_BYTES),
        name="peer_dense",
    )(h, u8, su, v8, sv, gmat)


def _peer(h, wq, keys, u, v, layer):
    hb = h
    i1, i2, g = _route(hb, wq, keys)
    return _peer_dense(hb, u, v, _gate_matrix(i1, i2, g), layer)


ROPE_HALF = 32
NA_QROWS = 8
NA_KROWS = 16
SWA_QB = 256
SWA_KB = SWA_QB + 2 * WINDOW

P_CK, P_CV, P_DK, P_DV, P_CQ, P_DQ = 0, 8, 16, 18, 20, 28
QK_CK, QK_DK, QK_CQ, QK_DQ = 0, 8, 10, 20


def _rope_tables(n):
    t = np.arange(n)
    freqs = (np.float32(ROPE_BASE) ** (-np.arange(ROPE_HALF, dtype=np.float32) / np.float32(ROPE_HALF))).astype(np.float32)
    ang_r = (t // GRID_W).astype(np.float32)[:, None] * freqs
    ang_c = (t % GRID_W).astype(np.float32)[:, None] * freqs
    cos = np.concatenate([np.cos(ang_r), np.cos(ang_r), np.cos(ang_c), np.cos(ang_c)], axis=-1)
    sin = np.concatenate([-np.sin(ang_r), np.sin(ang_r), -np.sin(ang_c), np.sin(ang_c)], axis=-1)
    return cos.astype(np.float32), sin.astype(np.float32)


def _qk_prep_kernel(tbl_ref, p_ref, g_ref, cos_ref, sin_ref, o_ref):
    del tbl_ref
    x = p_ref[...].astype(jnp.float32)
    y = x * lax.rsqrt(jnp.mean(x * x, axis=-1, keepdims=True) + EPS) * g_ref[0]
    lane = lax.broadcasted_iota(jnp.int32, y.shape, 1)
    first = jnp.bitwise_and(lane, 2 * ROPE_HALF - 1) < ROPE_HALF
    partner = jnp.where(first, pltpu.roll(y, LANES - ROPE_HALF, 1), pltpu.roll(y, ROPE_HALF, 1))
    o_ref[...] = (y * cos_ref[0] + partner * sin_ref[0]).astype(o_ref.dtype)


def _qk_prep(p, table, gains, cos, sin):
    n = p.shape[0]
    ncol = table.shape[1]
    bt = 2048 if n % 2048 == 0 else n
    grid_spec = pltpu.PrefetchScalarGridSpec(
        num_scalar_prefetch=1,
        grid=(n // bt, ncol),
        in_specs=[pl.BlockSpec((bt, HEAD_DIM), lambda i, j, tbl: (i, tbl[0, j])),
                  pl.BlockSpec((1, 1, HEAD_DIM), lambda i, j, tbl: (tbl[1, j], 0, 0)),
                  pl.BlockSpec((1, bt, HEAD_DIM), lambda i, j, tbl: (tbl[2, j], i, 0)),
                  pl.BlockSpec((1, bt, HEAD_DIM), lambda i, j, tbl: (tbl[2, j], i, 0))],
        out_specs=pl.BlockSpec((bt, HEAD_DIM), lambda i, j, tbl: (i, j)),
    )
    return pl.pallas_call(
        _qk_prep_kernel,
        grid_spec=grid_spec,
        out_shape=jax.ShapeDtypeStruct((n, ncol * HEAD_DIM), jnp.bfloat16),
        compiler_params=pltpu.CompilerParams(
            dimension_semantics=("parallel", "arbitrary"), vmem_limit_bytes=VMEM_LIMIT_BYTES),
        name="qk_prep",
    )(table, p, gains, cos, sin)


def _na_block_types(n):
    rows = n // GRID_W
    return ((NA_QROWS, NA_QROWS - 4), (0, 0), (rows - NA_QROWS, rows - NA_KROWS))


def _na_row_ok(n, r0, rlo, qi, kj):
    rows = n // GRID_W
    rs = min(max(r0 + qi - WIN_H // 2, 0), rows - WIN_H)
    return rs <= rlo + kj < rs + WIN_H


def _na_pair_tables(rpb):
    nc = 2 * WIN_W - 1
    c = np.arange(GRID_W)[:, None]
    kc = np.arange(GRID_W)[None, :]
    cs = np.clip(c - WIN_W // 2, 0, GRID_W - WIN_W)
    col_ok = (kc >= cs) & (kc < cs + WIN_W)
    oh_col = (np.clip(kc - c + WIN_W - 1, 0, nc - 1)[None] == np.arange(nc)[:, None, None]).astype(np.float32)
    colb = jnp.einsum('hab,bcd->hacd', rpb.astype(jnp.float32), jnp.asarray(oh_col), precision=lax.Precision.HIGHEST)
    colb = jnp.where(jnp.asarray(col_ok)[None, None], colb, NEG_INF)
    none = jnp.full_like(colb[:, :1], NEG_INF)
    first = jnp.concatenate([none, colb], axis=1)
    second = jnp.concatenate([colb, none], axis=1)
    gone = jnp.full_like(first, NEG_INF)
    pair = lambda lo, hi: jnp.concatenate([lo, hi], axis=-1)
    return jnp.stack([pair(first, second), pair(first, gone), pair(gone, second)], axis=1)


def _softmax_pv(parts, extra_logit=None):
    m = parts[0][0].max(axis=-1, keepdims=True)
    for s, _ in parts[1:]:
        m = jnp.maximum(m, s.max(axis=-1, keepdims=True))
    if extra_logit is not None:
        m = jnp.maximum(m, extra_logit)
    den = jnp.exp(extra_logit - m) if extra_logit is not None else 0.0
    acc = None
    for s, v in parts:
        p = jnp.exp(s - m)
        den = den + p.sum(axis=-1, keepdims=True)
        pv = jnp.dot(p.astype(jnp.bfloat16), v, preferred_element_type=jnp.float32)
        acc = pv if acc is None else acc + pv
    return acc / den


def _na_kernel(q_ref, k_ref, v_ref, kc_ref, vc_ref, pt_ref, o_ref, bias_ref):
    n = q_ref.shape[0]
    qb = NA_QROWS * GRID_W
    kb = NA_KROWS * GRID_W
    nblk = n // qb
    nt = (((1,), (1,)), ((), ()))
    kc = kc_ref[...]
    vc = vc_ref[...]
    for kind, (r0, rlo) in enumerate(_na_block_types(n)):
        for qi in range(NA_QROWS):
            for kp in range(NA_KROWS // 2):
                ok0 = _na_row_ok(n, r0, rlo, qi, 2 * kp)
                ok1 = _na_row_ok(n, r0, rlo, qi, 2 * kp + 1)
                e = rlo + 2 * kp - (r0 + qi) + WIN_H - 1
                rows, cols = slice(qi * GRID_W, (qi + 1) * GRID_W), slice(kp * LANES, (kp + 1) * LANES)
                if ok0 or ok1:
                    bias_ref[kind, rows, cols] = pt_ref[0, 0 if (ok0 and ok1) else (1 if ok0 else 2), e + 1]
                else:
                    bias_ref[kind, rows, cols] = jnp.full((GRID_W, LANES), NEG_INF, jnp.float32)

    def body(b, carry):
        q0 = pl.multiple_of(b * qb, qb)
        ks = pl.multiple_of(jnp.clip(b * qb - (NA_KROWS - NA_QROWS) // 2 * GRID_W, 0, n - kb), 2 * LANES)
        kind = jnp.where(b == 0, 1, jnp.where(b == nblk - 1, 2, 0))
        q = q_ref[pl.ds(q0, qb), :]
        s_w = lax.dot_general(q, k_ref[pl.ds(ks, kb), :], nt, preferred_element_type=jnp.float32) + bias_ref[kind]
        s_c = lax.dot_general(q, kc, nt, preferred_element_type=jnp.float32)
        o = _softmax_pv([(s_w, v_ref[pl.ds(ks, kb), :]), (s_c, vc)])
        o_ref[pl.ds(q0, qb), :] = o.astype(o_ref.dtype)
        return carry

    lax.fori_loop(0, nblk, body, 0)


def _neighbourhood_attn(qk, p, qk_c, p_c, pair_tables):
    n = qk.shape[0]
    m = qk_c.shape[0]

    def full(col0):
        return pl.BlockSpec((n, HEAD_DIM), lambda h: (0, col0 + h))

    def ctx(col0):
        return pl.BlockSpec((m, HEAD_DIM), lambda h: (0, col0 + h))

    return pl.pallas_call(
        _na_kernel,
        grid=(C_HEADS,),
        in_specs=[full(QK_CQ), full(QK_CK), full(P_CV), ctx(QK_CK), ctx(P_CV),
                  pl.BlockSpec((1, 3, 2 * WIN_H, GRID_W, LANES), lambda h: (h, 0, 0, 0, 0))],
        out_specs=pl.BlockSpec((n, HEAD_DIM), lambda h: (0, h)),
        out_shape=jax.ShapeDtypeStruct((n, C_WIDTH), jnp.bfloat16),
        scratch_shapes=[pltpu.VMEM((3, NA_QROWS * GRID_W, NA_KROWS * GRID_W), jnp.float32)],
        compiler_params=pltpu.CompilerParams(
            dimension_semantics=("parallel",), vmem_limit_bytes=VMEM_LIMIT_BYTES),
        name="neighbourhood_attn",
    )(qk, qk, p, qk_c, p_c, pair_tables)


def _swa_kernel(q_ref, k_ref, v_ref, kc_ref, vc_ref, sink_ref, o_ref):
    n = q_ref.shape[0]
    nblk = n // SWA_QB
    nt = (((1,), (1,)), ((), ()))
    kc = kc_ref[...]
    vc = vc_ref[...]
    kv = pl.program_id(0)
    sink = jnp.concatenate(
        [jnp.broadcast_to(sink_ref[pl.ds(kv * D_GROUP + g, 1), 0:1], (SWA_QB, 1)) for g in range(D_GROUP)], axis=0)
    q_off = jnp.bitwise_and(lax.broadcasted_iota(jnp.int32, (D_GROUP * SWA_QB, SWA_KB), 0), SWA_QB - 1)
    k_off = lax.broadcasted_iota(jnp.int32, (D_GROUP * SWA_QB, SWA_KB), 1)
    rel = q_off - k_off

    def body(b, carry):
        q0 = pl.multiple_of(b * SWA_QB, SWA_QB)
        ks = pl.multiple_of(jnp.clip(q0 - WINDOW, 0, n - SWA_KB), LANES)
        q = jnp.concatenate([q_ref[pl.ds(q0, SWA_QB), g * HEAD_DIM:(g + 1) * HEAD_DIM] for g in range(D_GROUP)], axis=0)
        s_w = lax.dot_general(q, k_ref[pl.ds(ks, SWA_KB), :], nt, preferred_element_type=jnp.float32)
        s_w = jnp.where(jnp.abs(rel + (q0 - ks)) <= WINDOW, s_w, NEG_INF)
        s_c = lax.dot_general(q, kc, nt, preferred_element_type=jnp.float32)
        o = _softmax_pv([(s_w, v_ref[pl.ds(ks, SWA_KB), :]), (s_c, vc)], extra_logit=sink)
        for g in range(D_GROUP):
            o_ref[pl.ds(q0, SWA_QB), g * HEAD_DIM:(g + 1) * HEAD_DIM] = o[g * SWA_QB:(g + 1) * SWA_QB].astype(o_ref.dtype)
        return carry

    lax.fori_loop(0, nblk, body, 0)


def _window_attn(qk, p, qk_c, p_c, sink_rows):
    n = qk.shape[0]
    m = qk_c.shape[0]
    gw = D_GROUP * HEAD_DIM
    return pl.pallas_call(
        _swa_kernel,
        grid=(D_KV_HEADS,),
        in_specs=[pl.BlockSpec((n, gw), lambda kv: (0, QK_DQ // D_GROUP + kv)),
                  pl.BlockSpec((n, HEAD_DIM), lambda kv: (0, QK_DK + kv)),
                  pl.BlockSpec((n, HEAD_DIM), lambda kv: (0, P_DV + kv)),
                  pl.BlockSpec((m, HEAD_DIM), lambda kv: (0, QK_DK + kv)),
                  pl.BlockSpec((m, HEAD_DIM), lambda kv: (0, P_DV + kv)),
                  pl.BlockSpec((D_HEADS, LANES), lambda kv: (0, 0))],
        out_specs=pl.BlockSpec((n, gw), lambda kv: (0, kv)),
        out_shape=jax.ShapeDtypeStruct((n, D_WIDTH), jnp.bfloat16),
        compiler_params=pltpu.CompilerParams(
            dimension_semantics=("parallel",), vmem_limit_bytes=VMEM_LIMIT_BYTES),
        name="window_attn",
    )(qk, qk, p, qk_c, p_c, sink_rows)


def _mixer_cd(h_c, h_l, w_in, qk_g, rpb, sink, w_out):
    n = h_l.shape[0]
    m = h_c.shape[0]
    scale = HEAD_DIM ** -0.5
    p_l = _matmul(h_l, w_in, jnp.bfloat16)
    p_c = _matmul(h_c, w_in[:, :CD_KV_COLS], jnp.bfloat16)
    gains = jnp.stack([qk_g[0] * scale, qk_g[1], qk_g[2] * scale, qk_g[3]]).astype(jnp.float32)[:, None, :]
    cos, sin = _rope_tables(n)
    cs_l = (np.stack([np.ones_like(cos), cos]), np.stack([np.zeros_like(sin), sin]))
    cs_c = (np.ones((1, m, HEAD_DIM), np.float32), np.zeros((1, m, HEAD_DIM), np.float32))
    src = list(range(P_CK, P_CK + 8)) + [P_DK, P_DK + 1] + list(range(P_CQ, P_CQ + 8)) + [P_CQ, P_CQ] + list(range(P_DQ, P_DQ + 8))
    gain = [1] * 8 + [3] * 2 + [0] * 8 + [0, 0] + [2] * 8
    rope = [0] * 8 + [1] * 2 + [0] * 8 + [0, 0] + [1] * 8
    tbl_l = jnp.asarray(np.array([src, gain, rope], np.int32))
    tbl_c = jnp.asarray(np.array([src[:10], gain[:10], [0] * 10], np.int32))
    qk_l = _qk_prep(p_l, tbl_l, gains, jnp.asarray(cs_l[0]), jnp.asarray(cs_l[1]))
    qk_c = _qk_prep(p_c, tbl_c, gains, jnp.asarray(cs_c[0]), jnp.asarray(cs_c[1]))
    c_out = _neighbourhood_attn(qk_l, p_l, qk_c, p_c, _na_pair_tables(rpb))
    sink_rows = jnp.broadcast_to(sink.astype(jnp.float32)[:, None], (D_HEADS, LANES))
    d_out = _window_attn(qk_l, p_l, qk_c, p_c, sink_rows)
    return _matmul(jnp.concatenate([c_out, d_out], axis=-1), w_out)


SCAN_L = 128
GLA_SUB = 16
N_SCAN_STATES = 2 * A_HEADS
GATE_LR_LANE = 4 * A_HEADS


def _log_sigmoid(x):
    return jnp.minimum(x, 0.0) - jnp.log(1.0 + jnp.exp(-jnp.abs(x)))


def _tri_mask(rev):
    r = lax.broadcasted_iota(jnp.int32, (SCAN_L, SCAN_L), 0)
    c = lax.broadcasted_iota(jnp.int32, (SCAN_L, SCAN_L), 1)
    return (c >= r) if rev else (c <= r)


def _cumsum_rows(tri, x):
    hi = x.astype(jnp.bfloat16)
    r1 = x - hi.astype(jnp.float32)
    mid = r1.astype(jnp.bfloat16)
    lo = (r1 - mid.astype(jnp.float32)).astype(jnp.bfloat16)
    dot = lambda p: jnp.dot(tri, p, preferred_element_type=jnp.float32)
    return dot(hi) + dot(mid) + dot(lo)


def _conv_kernel(x_ref, w_ref, b_ref, o_ref, *, seg, scale):
    x = x_ref[...].astype(jnp.float32)
    n = x.shape[0]
    t = lax.broadcasted_iota(jnp.int32, x.shape, 0)
    first = (t == 0) | (t == seg)
    last = (t == seg - 1) | (t == n - 1)
    prev = jnp.where(first, 0.0, pltpu.roll(x, 1, 0))
    nxt = jnp.where(last, 0.0, pltpu.roll(x, n - 1, 0))
    y = w_ref[0:1, :] * prev + w_ref[1:2, :] * x + w_ref[2:3, :] * nxt + b_ref[...]
    y = y * jax.nn.sigmoid(y)
    j = pl.program_id(0)
    o_ref[...] = (y * jnp.where(j >= A_HEADS, scale, 1.0)).astype(o_ref.dtype)


def _short_conv_silu(p, conv_w, conv_b, seg):
    t = p.shape[0]
    nblk = 2 * A_WIDTH // LANES
    return pl.pallas_call(
        functools.partial(_conv_kernel, seg=seg, scale=HEAD_DIM ** -0.5),
        grid=(nblk,),
        in_specs=[pl.BlockSpec((t, LANES), lambda j: (0, j)),
                  pl.BlockSpec((8, LANES), lambda j: (0, j)),
                  pl.BlockSpec((1, LANES), lambda j: (0, j))],
        out_specs=pl.BlockSpec((t, LANES), lambda j: (0, j)),
        out_shape=jax.ShapeDtypeStruct((t, 2 * A_WIDTH), jnp.bfloat16),
        compiler_params=pltpu.CompilerParams(
            dimension_semantics=("parallel",), vmem_limit_bytes=VMEM_LIMIT_BYTES),
        name="short_conv_silu",
    )(p, jnp.pad(conv_w, ((0, 8 - CONV_W), (0, 0))), conv_b[None, :])


def _scan_chunk_index(j, n_ctx_chunks, n_chunks, rev):
    if not rev:
        return j
    return jnp.where(j < n_ctx_chunks, n_ctx_chunks - 1 - j, n_chunks + n_ctx_chunks - 1 - j)


def _mlstm_kernel(qf_ref, kf_ref, vf_ref, gf_ref, qb_ref, kb_ref, vb_ref, gb_ref, gbias_ref,
                  yf_ref, yb_ref, c_ref, m_ref):
    @pl.when(pl.program_id(0) == 0)
    def _():
        c_ref[...] = jnp.zeros_like(c_ref)
        m_ref[...] = jnp.zeros_like(m_ref)

    lane = lax.broadcasted_iota(jnp.int32, (SCAN_L, LANES), 1)
    is_forget = jnp.bitwise_and(lane, 2 * A_HEADS - 1) >= A_HEADS
    ones = jnp.ones((SCAN_L, HEAD_DIM), jnp.bfloat16)
    nt = (((1,), (1,)), ((), ()))
    tn = (((0,), (0,)), ((), ()))
    for z, (q_ref, k_ref, v_ref, g_ref, y_ref) in enumerate(
            ((qf_ref, kf_ref, vf_ref, gf_ref, yf_ref), (qb_ref, kb_ref, vb_ref, gb_ref, yb_ref))):
        rev = z == 1
        mask = _tri_mask(rev)
        gates = g_ref[...] + gbias_ref[...]
        gates = jnp.where(is_forget, _log_sigmoid(gates), gates)
        csum = _cumsum_rows(mask.astype(jnp.bfloat16), jnp.where(is_forget, gates, 0.0))
        gates_t = gates.T
        csum_t = csum.T
        end = 0 if rev else SCAN_L - 1
        for h in range(A_HEADS):
            li, lf = z * 2 * A_HEADS + h, z * 2 * A_HEADS + A_HEADS + h
            sl = slice(h * HEAD_DIM, (h + 1) * HEAD_DIM)
            q, k, v = q_ref[:, sl], k_ref[:, sl], v_ref[:, sl]
            b_col, b_row = csum[:, lf:lf + 1], csum_t[lf:lf + 1, :]
            i_col, i_row = gates[:, li:li + 1], gates_t[li:li + 1, :]
            st = z * A_HEADS + h
            m_prev = m_ref[st][0:1, 0:1]
            dlog = jnp.where(mask, b_col - b_row + i_row, NEG_INF)
            m_inter = b_col + m_prev
            m_t = jnp.maximum(m_inter, jnp.max(dlog, axis=-1, keepdims=True))
            s = lax.dot_general(q, k, nt, preferred_element_type=jnp.float32)
            sc = (s * jnp.exp(dlog - m_t)).astype(jnp.bfloat16)
            w_inter = jnp.exp(m_inter - m_t)
            v_ext = jnp.concatenate([v, ones], axis=1)
            r = (jnp.dot(sc, v_ext, preferred_element_type=jnp.float32)
                 + w_inter * jnp.dot(q, c_ref[st].astype(jnp.bfloat16), preferred_element_type=jnp.float32))
            num, den = r[:, :HEAD_DIM], r[:, HEAD_DIM:]
            y_ref[:, sl] = num / jnp.maximum(jnp.abs(den), jnp.exp(-m_t))
            m_new = m_t[end:end + 1, :]
            b_end = b_col[end:end + 1, :]
            w_end = jnp.exp(b_end - b_col + i_col - m_new)
            decay = jnp.exp(b_end + m_prev - m_new)
            kw = (k.astype(jnp.float32) * w_end).astype(jnp.bfloat16)
            c_ref[st] = decay * c_ref[st] + lax.dot_general(kw, v_ext, tn, preferred_element_type=jnp.float32)
            m_ref[st] = jnp.broadcast_to(m_new, (8, LANES))


def _scan_specs(n_ctx_chunks, n_chunks, width, col0, rev):
    return pl.BlockSpec((SCAN_L, width),
                        lambda j: (_scan_chunk_index(j, n_ctx_chunks, n_chunks, rev), col0))


def _mlstm(qk, p, gates, gate_bias, seg):
    t = qk.shape[0]
    nck = t // SCAN_L
    ncc = seg // SCAN_L
    ins, specs = [], []
    for rev in (False, True):
        ins += [qk, qk, p, gates]
        specs += [_scan_specs(ncc, nck, A_WIDTH, 0, rev), _scan_specs(ncc, nck, A_WIDTH, 1, rev),
                  _scan_specs(ncc, nck, A_WIDTH, 2, rev), _scan_specs(ncc, nck, LANES, 0, rev)]
    return pl.pallas_call(
        _mlstm_kernel,
        grid=(nck,),
        in_specs=specs + [pl.BlockSpec((1, LANES), lambda j: (0, 0))],
        out_specs=[_scan_specs(ncc, nck, A_WIDTH, 0, False), _scan_specs(ncc, nck, A_WIDTH, 0, True)],
        out_shape=[jax.ShapeDtypeStruct((t, A_WIDTH), jnp.float32)] * 2,
        scratch_shapes=[pltpu.VMEM((N_SCAN_STATES, HEAD_DIM, 2 * HEAD_DIM), jnp.float32),
                        pltpu.VMEM((N_SCAN_STATES, 8, LANES), jnp.float32)],
        compiler_params=pltpu.CompilerParams(
            dimension_semantics=("arbitrary",), vmem_limit_bytes=VMEM_LIMIT_BYTES),
        name="mlstm_scan",
    )(*ins, gate_bias)


def _gla_kernel(qf_ref, kf_ref, vf_ref, gf_ref, qb_ref, kb_ref, vb_ref, gb_ref, w2_ref, ab_ref,
                of_ref, ob_ref, s_ref):
    @pl.when(pl.program_id(0) == 0)
    def _():
        s_ref[...] = jnp.zeros_like(s_ref)

    nsub = SCAN_L // GLA_SUB
    row = lax.broadcasted_iota(jnp.int32, (SCAN_L, LANES), 0)
    lane = lax.broadcasted_iota(jnp.int32, (SCAN_L, LANES), 1)
    head_lane = (lane < B_KEY_DIM, lane >= B_KEY_DIM)
    r_sq = lax.broadcasted_iota(jnp.int32, (SCAN_L, SCAN_L), 0)
    c_sq = lax.broadcasted_iota(jnp.int32, (SCAN_L, SCAN_L), 1)
    sub_of_row = jnp.right_shift(row, 4)
    in_sub = jnp.bitwise_and(row, GLA_SUB - 1)
    sel_r = lax.broadcasted_iota(jnp.int32, (LANES, 2 * LANES), 0)
    sel_c = lax.broadcasted_iota(jnp.int32, (LANES, 2 * LANES), 1)
    head_sum = ((sel_r < B_KEY_DIM) == (sel_c < LANES)).astype(jnp.bfloat16)
    nt = (((1,), (1,)), ((), ()))
    tn = (((0,), (0,)), ((), ()))
    for z, (q_ref, k_ref, v_ref, g_ref, o_ref) in enumerate(
            ((qf_ref, kf_ref, vf_ref, gf_ref, of_ref), (qb_ref, kb_ref, vb_ref, gb_ref, ob_ref))):
        rev = z == 1
        tri = _tri_mask(rev).astype(jnp.bfloat16)
        lr = g_ref[...].astype(jnp.bfloat16)
        end = 0 if rev else SCAN_L - 1
        for pr in range(B_HEADS // 2):
            ls = slice(pr * LANES, (pr + 1) * LANES)
            za = jnp.dot(lr, w2_ref[z, :, ls], preferred_element_type=jnp.float32) + ab_ref[z:z + 1, ls]
            b = _cumsum_rows(tri, _log_sigmoid(za) * (1.0 / GLA_TAU))
            q = q_ref[:, ls].astype(jnp.float32) * (B_KEY_DIM ** -0.5)
            k = k_ref[:, ls].astype(jnp.float32)
            v = v_ref[:, 2 * pr * HEAD_DIM:(2 * pr + 2) * HEAD_DIM]
            b_ref_rows = jnp.zeros_like(b)
            for j in range(nsub):
                e = j * GLA_SUB if rev else j * GLA_SUB + GLA_SUB - 1
                b_ref_rows = jnp.where(sub_of_row == j, b[e:e + 1, :], b_ref_rows)
            k_sub = (k * jnp.exp(b_ref_rows - b)).astype(jnp.bfloat16)
            k_sub_h = [jnp.where(head_lane[hh], k_sub, jnp.zeros_like(k_sub)) for hh in range(2)]
            attn =[jnp.zeros((SCAN_L, SCAN_L), jnp.float32), jnp.zeros((SCAN_L, SCAN_L), jnp.float32)]
            for j in range(nsub):
                e = j * GLA_SUB if rev else j * GLA_SUB + GLA_SUB - 1
                if j == (0 if rev else nsub - 1):
                    continue
                later = (sub_of_row < j) if rev else (sub_of_row > j)
                qj = (q * jnp.exp(jnp.where(later, b - b[e:e + 1, :], NEG_INF))).astype(jnp.bfloat16)
                for hh in range(2):
                    kj = jnp.where(sub_of_row == j, k_sub_h[hh], jnp.zeros_like(k_sub))
                    attn[hh] = attn[hh] + lax.dot_general(qj, kj, nt, preferred_element_type=jnp.float32)
            for d in range(GLA_SUB):
                shift = (SCAN_L - d) % SCAN_L if rev else d
                k_s = pltpu.roll(k, shift, 0) if d else k
                b_s = pltpu.roll(b, shift, 0) if d else b
                ok_row = (in_sub <= GLA_SUB - 1 - d) if rev else (in_sub >= d)
                f = (q * k_s * jnp.exp(jnp.where(ok_row, b - b_s, NEG_INF))).astype(jnp.bfloat16)
                red = jnp.dot(f, head_sum, preferred_element_type=jnp.float32)
                on_diag = (c_sq == r_sq + d) if rev else (c_sq == r_sq - d)
                for hh in range(2):
                    attn[hh] = attn[hh] + jnp.where(on_diag, red[:, hh * LANES:(hh + 1) * LANES], 0.0)
            q_in = q * jnp.exp(b)
            b_end = b[end:end + 1, :]
            k_out = (k * jnp.exp(b_end - b)).astype(jnp.bfloat16)
            st = z * (B_HEADS // 2) + pr
            s_t = s_ref[st]
            s_bf = s_t.astype(jnp.bfloat16)
            for hh in range(2):
                vs = slice(hh * HEAD_DIM, (hh + 1) * HEAD_DIM)
                qh = jnp.where(head_lane[hh], q_in, 0.0).astype(jnp.bfloat16)
                o = (jnp.dot(attn[hh].astype(jnp.bfloat16), v[:, vs], preferred_element_type=jnp.float32)
                     + lax.dot_general(qh, s_bf[vs, :], nt, preferred_element_type=jnp.float32))
                o_ref[:, (2 * pr + hh) * HEAD_DIM:(2 * pr + hh + 1) * HEAD_DIM] = o
            s_ref[st] = s_t * jnp.exp(b_end) + lax.dot_general(v, k_out, tn, preferred_element_type=jnp.float32)


def _gla(p, gates, w2, alpha_b, seg):
    t = p.shape[0]
    nck = t // SCAN_L
    ncc = seg // SCAN_L
    ins, specs = [], []
    for rev in (False, True):
        ins += [p, p, p, gates]
        specs += [_scan_specs(ncc, nck, B_KEY_WIDTH, P_BQ // B_KEY_WIDTH, rev),
                  _scan_specs(ncc, nck, B_KEY_WIDTH, P_BK // B_KEY_WIDTH, rev),
                  _scan_specs(ncc, nck, B_VAL_WIDTH, P_BV // B_VAL_WIDTH, rev),
                  _scan_specs(ncc, nck, LANES, 0, rev)]
    return pl.pallas_call(
        _gla_kernel,
        grid=(nck,),
        in_specs=specs + [pl.BlockSpec((2, LANES, B_KEY_WIDTH), lambda j: (0, 0, 0)),
                          pl.BlockSpec((2, B_KEY_WIDTH), lambda j: (0, 0))],
        out_specs=[_scan_specs(ncc, nck, B_VAL_WIDTH, 0, False), _scan_specs(ncc, nck, B_VAL_WIDTH, 0, True)],
        out_shape=[jax.ShapeDtypeStruct((t, B_VAL_WIDTH), jnp.float32)] * 2,
        scratch_shapes=[pltpu.VMEM((B_HEADS, 2 * HEAD_DIM, LANES), jnp.float32)],
        compiler_params=pltpu.CompilerParams(
            dimension_semantics=("arbitrary",), vmem_limit_bytes=VMEM_LIMIT_BYTES),
        name="gla_scan",
    )(*ins, w2, alpha_b)


def _head_out_kernel(af_ref, ab_ref, bf_ref, bb_ref, gate_ref, g_ref, o_ref):
    is_gla = pl.program_id(1) >= A_HEADS
    y = jnp.where(is_gla, bf_ref[...] + bb_ref[...], af_ref[...] + ab_ref[...])
    yn = y * lax.rsqrt(jnp.mean(y * y, axis=-1, keepdims=True) + EPS) * g_ref[0]
    gate = gate_ref[...].astype(jnp.float32)
    o_ref[...] = (yn * jax.nn.sigmoid(gate) * jnp.where(is_gla, gate, 1.0)).astype(o_ref.dtype)


def _head_out(ya, yb, p, head_g):
    t = p.shape[0]
    bt = _row_tile(t)
    nh = A_HEADS + B_HEADS
    a_spec = pl.BlockSpec((bt, HEAD_DIM), lambda i, j: (i, jnp.minimum(j, A_HEADS - 1)))
    b_spec = pl.BlockSpec((bt, HEAD_DIM), lambda i, j: (i, jnp.maximum(j - A_HEADS, 0)))

    def gate_col(i, j):
        return (i, jnp.where(j < A_HEADS, P_AO + j, P_BR + j - A_HEADS))

    return pl.pallas_call(
        _head_out_kernel,
        grid=(t // bt, nh),
        in_specs=[a_spec, a_spec, b_spec, b_spec,
                  pl.BlockSpec((bt, HEAD_DIM), gate_col),
                  pl.BlockSpec((1, 1, HEAD_DIM), lambda i, j: (j // A_HEADS, 0, 0))],
        out_specs=pl.BlockSpec((bt, HEAD_DIM), lambda i, j: (i, j)),
        out_shape=jax.ShapeDtypeStruct((t, nh * HEAD_DIM), jnp.bfloat16),
        compiler_params=pltpu.CompilerParams(
            dimension_semantics=("parallel", "arbitrary"), vmem_limit_bytes=VMEM_LIMIT_BYTES),
        name="head_out",
    )(ya[0], ya[1], yb[0], yb[1], p, head_g.astype(jnp.float32)[:, None, :])


P_AO = 24
P_BQ, P_BK, P_BV = 4096, 4608, 5120
P_BR = 48


def _mixer_ab(h, w_in, conv_w, conv_b, gate_b, alpha_w2, alpha_b, head_g, w_out, seg):
    bf = jnp.bfloat16
    cols = np.cumsum((0,) + AB_SPLITS)
    pick = lambda *ids: jnp.concatenate([w_in[:, cols[i]:cols[i + 1]] for i in ids], axis=1)
    w_main = pick(0, 1, 2, 3, 5, 6, 7, 8).astype(bf)
    w_gate = jnp.pad(pick(4, 9), ((0, 0), (0, LANES - 4 * A_HEADS - 2 * GLA_RANK))).astype(bf)
    hb = h
    p = _matmul(hb, w_main, bf)
    gates = _matmul(hb, w_gate, jnp.float32)
    gate_bias = jnp.pad(gate_b.reshape(1, -1).astype(jnp.float32), ((0, 0), (0, LANES - 4 * A_HEADS)))
    w2 = jnp.zeros((2, LANES, B_KEY_WIDTH), jnp.float32)
    for z in range(2):
        w2 = w2.at[z, GATE_LR_LANE + z * GLA_RANK:GATE_LR_LANE + (z + 1) * GLA_RANK].set(alpha_w2[z])
    qk = _short_conv_silu(p, conv_w, conv_b, seg)
    ya = _mlstm(qk, p, gates, gate_bias, seg)
    yb = _gla(p, gates, w2.astype(bf), alpha_b.astype(jnp.float32), seg)
    return _matmul(_head_out(ya, yb, p, head_g), w_out)


MOD_ROWS = 256


def _res_mod_kernel(x_ref, y_ref, p_ref, xo_ref, h_ref):
    p = p_ref[0]
    x = x_ref[...] + p[0:1, :] * y_ref[...]
    xo_ref[...] = x
    r = x * lax.rsqrt(jnp.mean(x * x, axis=-1, keepdims=True) + EPS) * p[1:2, :]
    h_ref[...] = (r * (1.0 + p[3:4, :]) + p[2:3, :]).astype(h_ref.dtype)


def _mod_kernel(x_ref, p_ref, h_ref):
    p = p_ref[0]
    x = x_ref[...]
    r = x * lax.rsqrt(jnp.mean(x * x, axis=-1, keepdims=True) + EPS) * p[1:2, :]
    h_ref[...] = (r * (1.0 + p[3:4, :]) + p[2:3, :]).astype(h_ref.dtype)


def _mod_params(gate_c, gate_l, gain, shift_c, shift_l, scale_c, scale_l):
    rows = lambda g, sh, sc: jnp.concatenate(
        [g.reshape(1, -1), gain.reshape(1, -1), sh.reshape(1, -1), sc.reshape(1, -1),
         jnp.zeros((4, gain.shape[-1]), jnp.float32)], axis=0)
    return jnp.stack([rows(gate_c, shift_c, scale_c), rows(gate_l, shift_l, scale_l)]).astype(jnp.float32)


def _res_mod(x, y, params, seg, row0=0, n_rows=None):
    d = x.shape[1]
    n_rows = x.shape[0] - row0 if n_rows is None else n_rows
    bt = MOD_ROWS
    off, seg_blocks = row0 // bt, seg // bt
    row_spec = pl.BlockSpec((bt, d), lambda i: (i + off, 0))
    out_spec = pl.BlockSpec((bt, d), lambda i: (i, 0))
    p_spec = pl.BlockSpec((1, 8, d), lambda i: (jnp.where(i + off < seg_blocks, 0, 1), 0, 0))
    cp = pltpu.CompilerParams(dimension_semantics=("parallel",), vmem_limit_bytes=VMEM_LIMIT_BYTES)
    h_shape = jax.ShapeDtypeStruct((n_rows, d), jnp.bfloat16)
    if y is None:
        return pl.pallas_call(_mod_kernel, grid=(n_rows // bt,), in_specs=[row_spec, p_spec], out_specs=out_spec,
                              out_shape=h_shape, compiler_params=cp, name="modulate")(x, params)
    return pl.pallas_call(
        _res_mod_kernel, grid=(n_rows // bt,), in_specs=[row_spec, row_spec, p_spec],
        out_specs=[out_spec, out_spec],
        out_shape=[jax.ShapeDtypeStruct((n_rows, d), jnp.float32), h_shape],
        compiler_params=cp, name="residual_modulate")(x, y, params)


def _modulation_kernel(c_ref, w_ref, b_ref, o_ref):
    act = c_ref[...]
    act = (act * jax.nn.sigmoid(act)).astype(jnp.bfloat16)
    o_ref[0] = jnp.dot(act, w_ref[0].astype(jnp.bfloat16), preferred_element_type=jnp.float32) + b_ref[0]


def _modulation(c, c_ctx, mod_w, mod_b):
    nl, d, n = mod_w.shape
    cond = jnp.concatenate([c_ctx.reshape(1, d), c.reshape(1, d), jnp.zeros((6, d), jnp.float32)], axis=0)
    bn = n // 8
    return pl.pallas_call(
        _modulation_kernel,
        grid=(nl, n // bn),
        in_specs=[pl.BlockSpec((8, d), lambda l, j: (0, 0)),
                  pl.BlockSpec((1, d, bn), lambda l, j: (l, 0, j)),
                  pl.BlockSpec((1, 1, bn), lambda l, j: (l, 0, j))],
        out_specs=pl.BlockSpec((1, 8, bn), lambda l, j: (l, 0, j)),
        out_shape=jax.ShapeDtypeStruct((nl, 8, n), jnp.float32),
        compiler_params=pltpu.CompilerParams(
            dimension_semantics=("parallel", "parallel"), vmem_limit_bytes=VMEM_LIMIT_BYTES),
        name="modulation",
    )(cond, mod_w, mod_b[:, None, :])


def kernel(x, c, ctx, c_ctx, mod_w, mod_b, norm_g, ab_w_in, ab_conv_w, ab_conv_b, ab_gate_b, ab_alpha_w2,
           ab_alpha_b, ab_head_g, ab_w_out, cd_w_in, cd_qk_g, cd_rpb, cd_sink, cd_w_out, peer_w_q,
           peer_sub_keys, peer_u, peer_v):
    assert DEPTH == 2
    bf = jnp.bfloat16
    m = ctx.shape[1]
    xs = jnp.concatenate([ctx[0], x[0]], axis=0)
    ub = _quant_rows(peer_u)
    vb = _quant_rows(peer_v)
    mod_all = _modulation(c, c_ctx, mod_w, mod_b)
    mods = [(jnp.split(mod_all[l, 0], 6), jnp.split(mod_all[l, 1], 6)) for l in range(DEPTH)]
    one = jnp.ones_like(mods[0][0][0])

    def params(l, sub, gate_c, gate_l):
        mc, ml = mods[l]
        return _mod_params(gate_c, gate_l, norm_g[l, sub], mc[3 * sub], ml[3 * sub], mc[3 * sub + 1], ml[3 * sub + 1])

    (mc0, ml0), (mc1, ml1) = mods
    h = _res_mod(xs, None, params(0, 0, one, one), m)
    y = _mixer_ab(h, ab_w_in[0], ab_conv_w[0], ab_conv_b[0], ab_gate_b[0], ab_alpha_w2[0], ab_alpha_b[0],
                  ab_head_g[0], ab_w_out[0].astype(bf), m)
    xs, h = _res_mod(xs, y, params(0, 1, mc0[2], ml0[2]), m)
    y = _peer(h, peer_w_q[0].astype(bf), peer_sub_keys[0].astype(bf), ub, vb, 0)
    p10 = params(1, 0, mc0[5], ml0[5])
    _, hc = _res_mod(xs, y, p10, m, 0, m)
    xl, hl = _res_mod(xs, y, p10, m, m)
    y = _mixer_cd(hc, hl, cd_w_in[0].astype(bf), cd_qk_g[0], cd_rpb[0], cd_sink[0], cd_w_out[0].astype(bf))
    xl, h = _res_mod(xl, y, params(1, 1, ml1[2], ml1[2]), 0)
    y = _peer(h, peer_w_q[1].astype(bf), peer_sub_keys[1].astype(bf), ub, vb, 1)
    return (xl + ml1[5] * y)[None]
```

```python
import functools

import numpy as np
import jax
import jax.numpy as jnp
from jax import lax
from jax.experimental import pallas as pl
from jax.experimental.pallas import tpu as pltpu

D_MODEL = 2048
SEQ = 8192
DEPTH = 2
GRID_W = 64
CTX_LEN = 256
HEAD_DIM = 128
N_GROUP_HEADS = 8
EPS = 1e-6
A_HEADS = 8
A_WIDTH = 1024
CONV_W = 3
B_HEADS = 8
B_KEY_DIM = 64
B_KEY_WIDTH = 512
B_VAL_WIDTH = 1024
GLA_RANK = 16
GLA_TAU = 16.0
SCAN_CHUNK = 64
C_HEADS = 8
C_WIDTH = 1024
WIN_H = 8
WIN_W = 16
D_HEADS = 8
D_KV_HEADS = 2
D_GROUP = 4
D_WIDTH = 1024
D_KV_WIDTH = 256
WINDOW = 128
ROPE_BASE = 10000.0
PEER_HEADS = 8
PEER_NKEYS = 128
PEER_EXPERTS = PEER_NKEYS * PEER_NKEYS
PEER_QDIM = 256
PEER_TOPK = 16
PEER_SLOTS = PEER_HEADS * PEER_TOPK
AB_SPLITS = (A_WIDTH, A_WIDTH, A_WIDTH, A_WIDTH, 4 * A_HEADS,
             B_KEY_WIDTH, B_KEY_WIDTH, B_VAL_WIDTH, B_VAL_WIDTH, 2 * GLA_RANK)
CD_SPLITS = (C_WIDTH, C_WIDTH, D_KV_WIDTH, D_KV_WIDTH, C_WIDTH, D_WIDTH)
CD_KV_COLS = sum(CD_SPLITS[:4])

VMEM_LIMIT_BYTES = 56 * 1024 * 1024
LANES = 128


def _row_tile(m):
    for t in (1024, 768, 512, 256, 128):
        if m % t == 0:
            return t
    raise ValueError(f"unsupported row count {m}")


def _mm_kernel(x_ref, w_ref, o_ref):
    o_ref[...] = jnp.dot(x_ref[...], w_ref[...], preferred_element_type=jnp.float32).astype(o_ref.dtype)


def _matmul(x, w, out_dtype=jnp.float32):
    m, k = x.shape
    n = w.shape[1]
    bn = min(512, n)
    n_pad = -(-n // bn) * bn
    if n_pad != n:
        w = jnp.pad(w, ((0, 0), (0, n_pad - n)))
    bm = _row_tile(m)
    out = pl.pallas_call(
        _mm_kernel,
        grid=(m // bm, n_pad // bn),
        in_specs=[pl.BlockSpec((bm, k), lambda i, j: (i, 0)),
                  pl.BlockSpec((k, bn), lambda i, j: (0, j))],
        out_specs=pl.BlockSpec((bm, bn), lambda i, j: (i, j)),
        out_shape=jax.ShapeDtypeStruct((m, n_pad), out_dtype),
        compiler_params=pltpu.CompilerParams(
            dimension_semantics=("parallel", "arbitrary"), vmem_limit_bytes=VMEM_LIMIT_BYTES),
        name="matmul",
    )(x.astype(jnp.bfloat16), w.astype(jnp.bfloat16))
    return out[:, :n] if n_pad != n else out


ROUTE_TOKENS = (768, 512, 256, 128)
GATE_TOKENS = 128
NEG_INF = float("-inf")


def _top16(s, key, big):
    slot = lax.broadcasted_iota(jnp.int32, (PEER_TOPK, s.shape[1]), 0)
    vals = jnp.zeros((PEER_TOPK, s.shape[1]), jnp.float32)
    keys = jnp.zeros((PEER_TOPK, s.shape[1]), jnp.float32)
    for it in range(PEER_TOPK):
        m = jnp.max(s, axis=0, keepdims=True)
        k = jnp.min(jnp.where(s == m, key, big), axis=0, keepdims=True)
        vals = jnp.where(slot == it, m, vals)
        keys = jnp.where(slot == it, k, keys)
        s = jnp.where(key == k, NEG_INF, s)
    return vals, keys


def _route_kernel(h_ref, wq_ref, k_ref, i1_ref, i2_ref, g_ref):
    bt = h_ref.shape[0]
    q = jnp.dot(h_ref[...], wq_ref[...], preferred_element_type=jnp.float32).astype(jnp.bfloat16)
    nt = (((1,), (1,)), ((), ()))
    s1 = lax.dot_general(k_ref[0, 0], q[:, :PEER_NKEYS], nt, preferred_element_type=jnp.float32)
    s2 = lax.dot_general(k_ref[0, 1], q[:, PEER_NKEYS:], nt, preferred_element_type=jnp.float32)
    row = lax.broadcasted_iota(jnp.int32, (PEER_NKEYS, bt), 0).astype(jnp.float32)
    v1, r1 = _top16(s1, row, float(PEER_NKEYS))
    v2, r2 = _top16(s2, row, float(PEER_NKEYS))
    i16 = lax.broadcasted_iota(jnp.int32, (16, bt), 0).astype(jnp.float32)
    i8 = lax.broadcasted_iota(jnp.int32, (8, bt), 0).astype(jnp.float32)
    cand = [v1[0:1] + v2]
    flat = [i16]
    for a in range(1, 8):
        cand.append(v1[a:a + 1] + v2[0:8])
        flat.append(i8 + float(a * PEER_TOPK))
    cand.append(v1[8:16] + v2[0:1])
    flat.append((i8 + 8.0) * float(PEER_TOPK))
    best, key = _top16(jnp.concatenate(cand, axis=0), jnp.concatenate(flat, axis=0), float(PEER_TOPK * PEER_TOPK))
    key = key.astype(jnp.int32)
    a_sel = jnp.right_shift(key, 4)
    b_sel = jnp.bitwise_and(key, PEER_TOPK - 1)
    e1 = jnp.zeros_like(r1)
    e2 = jnp.zeros_like(r2)
    for r in range(PEER_TOPK):
        e1 = jnp.where(a_sel == r, r1[r:r + 1], e1)
        e2 = jnp.where(b_sel == r, r2[r:r + 1], e2)
    e1 = e1.astype(jnp.int32)
    e2 = e2.astype(jnp.int32)
    ex = jnp.exp(best - best[0:1])
    i1_ref[...] = e1
    i2_ref[...] = e2
    g_ref[...] = ex / jnp.sum(ex, axis=0, keepdims=True)


def _route(hb, wq, keys):
    t, d = hb.shape
    bt = next(b for b in ROUTE_TOKENS if t % b == 0)
    slot_spec = pl.BlockSpec((PEER_TOPK, bt), lambda i, h: (h, i))
    return pl.pallas_call(
        _route_kernel,
        grid=(t // bt, PEER_HEADS),
        in_specs=[pl.BlockSpec((bt, d), lambda i, h: (i, 0)),
                  pl.BlockSpec((d, PEER_QDIM), lambda i, h: (0, h)),
                  pl.BlockSpec((1, 2, PEER_NKEYS, PEER_QDIM // 2), lambda i, h: (h, 0, 0, 0))],
        out_specs=[slot_spec, slot_spec, slot_spec],
        out_shape=[jax.ShapeDtypeStruct((PEER_SLOTS, t), jnp.int32),
                   jax.ShapeDtypeStruct((PEER_SLOTS, t), jnp.int32),
                   jax.ShapeDtypeStruct((PEER_SLOTS, t), jnp.float32)],
        compiler_params=pltpu.CompilerParams(
            dimension_semantics=("parallel", "arbitrary"), vmem_limit_bytes=VMEM_LIMIT_BYTES),
        name="peer_route",
    )(hb, wq, keys)


GATE_GROUP = 16


def _gate_matrix_kernel(i1_ref, i2_ref, g_ref, o_ref, i1_t, g_t, stage):
    tb = o_ref.shape[1]
    i1_t[...] = i1_ref[...].T
    g_t[...] = g_ref[...].T
    row_id = lax.broadcasted_iota(jnp.int32, (PEER_NKEYS, PEER_SLOTS), 0)
    lane_id = lax.broadcasted_iota(jnp.int32, (PEER_SLOTS, PEER_NKEYS), 1)
    zero = jnp.zeros((PEER_NKEYS, PEER_SLOTS), jnp.bfloat16)

    for tp in range(tb // 2):
        ta = 2 * tp
        tb_ = ta + 1
        ca = jnp.where(row_id == i1_t[ta:ta + 1, :], g_t[ta:ta + 1, :], 0.0).astype(jnp.bfloat16)
        cb = jnp.where(row_id == i1_t[tb_:tb_ + 1, :], g_t[tb_:tb_ + 1, :], 0.0).astype(jnp.bfloat16)
        oa = jnp.where(i2_ref[:, ta:ta + 1] == lane_id, 1.0, 0.0).astype(jnp.bfloat16)
        ob = jnp.where(i2_ref[:, tb_:tb_ + 1] == lane_id, 1.0, 0.0).astype(jnp.bfloat16)
        lhs = jnp.concatenate([ca, cb], axis=1)
        rhs = jnp.concatenate([jnp.concatenate([oa, zero], axis=1),
                               jnp.concatenate([zero, ob], axis=1)], axis=0)
        out = jnp.dot(lhs, rhs, preferred_element_type=jnp.float32)
        stage[ta % GATE_GROUP] = out[:, :PEER_NKEYS]
        stage[tb_ % GATE_GROUP] = out[:, PEER_NKEYS:]
        if tb_ % GATE_GROUP == GATE_GROUP - 1:
            t0 = tb_ + 1 - GATE_GROUP
            o_ref[:, t0:t0 + GATE_GROUP, :] = jnp.swapaxes(stage[...], 0, 1).astype(o_ref.dtype)


def _gate_matrix(i1, i2, g):
    t = i1.shape[1]
    tb = GATE_TOKENS
    return pl.pallas_call(
        _gate_matrix_kernel,
        grid=(t // tb,),
        in_specs=[pl.BlockSpec((PEER_SLOTS, tb), lambda i: (0, i))] * 3,
        out_specs=pl.BlockSpec((PEER_NKEYS, tb, PEER_NKEYS), lambda i: (0, i, 0)),
        out_shape=jax.ShapeDtypeStruct((PEER_NKEYS, t, PEER_NKEYS), jnp.bfloat16),
        scratch_shapes=[pltpu.VMEM((tb, PEER_SLOTS), jnp.int32),
                        pltpu.VMEM((tb, PEER_SLOTS), jnp.float32),
                        pltpu.VMEM((GATE_GROUP, PEER_NKEYS, PEER_NKEYS), jnp.float32)],
        compiler_params=pltpu.CompilerParams(
            dimension_semantics=("parallel",), vmem_limit_bytes=VMEM_LIMIT_BYTES),
        name="peer_gate_matrix",
    )(i1, i2, g)


def _gelu_tanh(x):
    return 0.5 * x * (1.0 + jnp.tanh(0.7978845608028654 * (x + 0.044715 * x * x * x)))


FP8 = jnp.float8_e4m3fn
PEER_ROW_GROUPS = 2
FP8_TARGET = 240.0


def _row_scale(x):
    amax = jnp.max(jnp.abs(x), axis=-1, keepdims=True)
    pos = amax > 0.0
    return jnp.where(pos, FP8_TARGET / amax, 1.0), jnp.where(pos, amax * (1.0 / FP8_TARGET), 1.0)


def _quant_kernel(x_ref, o_ref, s_ref):
    x = x_ref[0]
    scale, inv = _row_scale(x)
    o_ref[0] = (x * scale).astype(o_ref.dtype)
    s_ref[0] = jnp.broadcast_to(inv, s_ref.shape[1:])


def _quant_rows(x):
    nl, r, c = x.shape
    br = 1024
    q, s = pl.pallas_call(
        _quant_kernel,
        grid=(nl, r // br),
        in_specs=[pl.BlockSpec((1, br, c), lambda l, i: (l, i, 0))],
        out_specs=[pl.BlockSpec((1, br, c), lambda l, i: (l, i, 0)),
                   pl.BlockSpec((1, br, LANES), lambda l, i: (l, i, 0))],
        out_shape=[jax.ShapeDtypeStruct(x.shape, FP8), jax.ShapeDtypeStruct((nl, r, LANES), jnp.float32)],
        compiler_params=pltpu.CompilerParams(
            dimension_semantics=("parallel", "parallel"), vmem_limit_bytes=VMEM_LIMIT_BYTES),
        name="quant_rows",
    )(x)
    return q, s[:, :, 0][:, None, :]


def _peer_dense_kernel(h_ref, u_ref, su_ref, v_ref, sv_ref, g_ref, o_ref, h8_ref, sh_ref):
    j = pl.program_id(1)

    @pl.when(j == 0)
    def _():
        h = h_ref[...].astype(jnp.float32)
        scale, inv = _row_scale(h)
        h8_ref[...] = (h * scale).astype(h8_ref.dtype)
        sh_ref[...] = jnp.broadcast_to(inv, sh_ref.shape)

    nt = (((1,), (1,)), ((), ()))
    bm = h8_ref.shape[0]
    rows = [slice(r * bm // PEER_ROW_GROUPS, (r + 1) * bm // PEER_ROW_GROUPS) for r in range(PEER_ROW_GROUPS)]
    accs = [lax.dot_general(h8_ref[rs, :], u_ref[0], nt, preferred_element_type=jnp.float32) for rs in rows]
    ws = []
    for rs, acc in zip(rows, accs):
        act = _gelu_tanh(acc * sh_ref[rs, 0:1] * su_ref[0])
        gate = jnp.concatenate([g_ref[r, rs, :] for r in range(g_ref.shape[0])], axis=1)
        w = act * gate.astype(jnp.float32) * sv_ref[0]
        scale, inv = _row_scale(w)
        ws.append(((w * scale).astype(FP8), inv))
    contribs = [jnp.dot(w8, v_ref[0], preferred_element_type=jnp.float32) * inv for w8, inv in ws]

    @pl.when(j == 0)
    def _():
        for rs, contrib in zip(rows, contribs):
            o_ref[rs, :] = contrib

    @pl.when(j != 0)
    def _():
        for rs, contrib in zip(rows, contribs):
            o_ref[rs, :] += contrib


def _peer_dense(h, u, v, gmat, layer):
    t, d = h.shape
    (u8, su), (v8, sv) = u, v
    e = v8.shape[1]
    bm = _row_tile(t)
    be = 1024
    tab = pl.BlockSpec((1, be, d), lambda i, j: (layer, j, 0))
    sc = pl.BlockSpec((1, 1, be), lambda i, j: (layer, 0, j))
    return pl.pallas_call(
        _peer_dense_kernel,
        grid=(t // bm, e // be),
        in_specs=[pl.BlockSpec((bm, d), lambda i, j: (i, 0)), tab, sc, tab, sc,
                  pl.BlockSpec((be // PEER_NKEYS, bm, PEER_NKEYS), lambda i, j: (j, i, 0))],
        out_specs=pl.BlockSpec((bm, d), lambda i, j: (i, 0)),
        out_shape=jax.ShapeDtypeStruct((t, d), jnp.float32),
        scratch_shapes=[pltpu.VMEM((bm, d), FP8), pltpu.VMEM((bm, LANES), jnp.float32)],
        compiler_params=pltpu.CompilerParams(
            dimension_semantics=("parallel", "arbitrary"), vmem_limit_bytes=VMEM_LIMIT_BYTES),
        name="peer_dense",
    )(h, u8, su, v8, sv, gmat)


def _peer(h, wq, keys, u, v, layer):
    hb = h
    i1, i2, g = _route(hb, wq, keys)
    return _peer_dense(hb, u, v, _gate_matrix(i1, i2, g), layer)


ROPE_HALF = 32
NA_QROWS = 8
NA_KROWS = 16
SWA_QB = 256
SWA_KB = SWA_QB + 2 * WINDOW

P_CK, P_CV, P_DK, P_DV, P_CQ, P_DQ = 0, 8, 16, 18, 20, 28
QK_CK, QK_DK, QK_CQ, QK_DQ = 0, 8, 10, 20


def _rope_tables(n):
    t = np.arange(n)
    freqs = (np.float32(ROPE_BASE) ** (-np.arange(ROPE_HALF, dtype=np.float32) / np.float32(ROPE_HALF))).astype(np.float32)
    ang_r = (t // GRID_W).astype(np.float32)[:, None] * freqs
    ang_c = (t % GRID_W).astype(np.float32)[:, None] * freqs
    cos = np.concatenate([np.cos(ang_r), np.cos(ang_r), np.cos(ang_c), np.cos(ang_c)], axis=-1)
    sin = np.concatenate([-np.sin(ang_r), np.sin(ang_r), -np.sin(ang_c), np.sin(ang_c)], axis=-1)
    return cos.astype(np.float32), sin.astype(np.float32)


def _qk_prep_kernel(tbl_ref, p_ref, g_ref, cos_ref, sin_ref, o_ref):
    del tbl_ref
    x = p_ref[...].astype(jnp.float32)
    y = x * lax.rsqrt(jnp.mean(x * x, axis=-1, keepdims=True) + EPS) * g_ref[0]
    lane = lax.broadcasted_iota(jnp.int32, y.shape, 1)
    first = jnp.bitwise_and(lane, 2 * ROPE_HALF - 1) < ROPE_HALF
    partner = jnp.where(first, pltpu.roll(y, LANES - ROPE_HALF, 1), pltpu.roll(y, ROPE_HALF, 1))
    o_ref[...] = (y * cos_ref[0] + partner * sin_ref[0]).astype(o_ref.dtype)


def _qk_prep(p, table, gains, cos, sin):
    n = p.shape[0]
    ncol = table.shape[1]
    bt = 2048 if n % 2048 == 0 else n
    grid_spec = pltpu.PrefetchScalarGridSpec(
        num_scalar_prefetch=1,
        grid=(n // bt, ncol),
        in_specs=[pl.BlockSpec((bt, HEAD_DIM), lambda i, j, tbl: (i, tbl[0, j])),
                  pl.BlockSpec((1, 1, HEAD_DIM), lambda i, j, tbl: (tbl[1, j], 0, 0)),
                  pl.BlockSpec((1, bt, HEAD_DIM), lambda i, j, tbl: (tbl[2, j], i, 0)),
                  pl.BlockSpec((1, bt, HEAD_DIM), lambda i, j, tbl: (tbl[2, j], i, 0))],
        out_specs=pl.BlockSpec((bt, HEAD_DIM), lambda i, j, tbl: (i, j)),
    )
    return pl.pallas_call(
        _qk_prep_kernel,
        grid_spec=grid_spec,
        out_shape=jax.ShapeDtypeStruct((n, ncol * HEAD_DIM), jnp.bfloat16),
        compiler_params=pltpu.CompilerParams(
            dimension_semantics=("parallel", "arbitrary"), vmem_limit_bytes=VMEM_LIMIT_BYTES),
        name="qk_prep",
    )(table, p, gains, cos, sin)


def _na_block_types(n):
    rows = n // GRID_W
    return ((NA_QROWS, NA_QROWS - 4), (0, 0), (rows - NA_QROWS, rows - NA_KROWS))


def _na_row_ok(n, r0, rlo, qi, kj):
    rows = n // GRID_W
    rs = min(max(r0 + qi - WIN_H // 2, 0), rows - WIN_H)
    return rs <= rlo + kj < rs + WIN_H


def _na_pair_tables(rpb):
    nc = 2 * WIN_W - 1
    c = np.arange(GRID_W)[:, None]
    kc = np.arange(GRID_W)[None, :]
    cs = np.clip(c - WIN_W // 2, 0, GRID_W - WIN_W)
    col_ok = (kc >= cs) & (kc < cs + WIN_W)
    oh_col = (np.clip(kc - c + WIN_W - 1, 0, nc - 1)[None] == np.arange(nc)[:, None, None]).astype(np.float32)
    colb = jnp.einsum('hab,bcd->hacd', rpb.astype(jnp.float32), jnp.asarray(oh_col), precision=lax.Precision.HIGHEST)
    colb = jnp.where(jnp.asarray(col_ok)[None, None], colb, NEG_INF)
    none = jnp.full_like(colb[:, :1], NEG_INF)
    first = jnp.concatenate([none, colb], axis=1)
    second = jnp.concatenate([colb, none], axis=1)
    gone = jnp.full_like(first, NEG_INF)
    pair = lambda lo, hi: jnp.concatenate([lo, hi], axis=-1)
    return jnp.stack([pair(first, second), pair(first, gone), pair(gone, second)], axis=1)


def _softmax_pv(parts, extra_logit=None):
    m = parts[0][0].max(axis=-1, keepdims=True)
    for s, _ in parts[1:]:
        m = jnp.maximum(m, s.max(axis=-1, keepdims=True))
    if extra_logit is not None:
        m = jnp.maximum(m, extra_logit)
    den = jnp.exp(extra_logit - m) if extra_logit is not None else 0.0
    acc = None
    for s, v in parts:
        p = jnp.exp(s - m)
        den = den + p.sum(axis=-1, keepdims=True)
        pv = jnp.dot(p.astype(jnp.bfloat16), v, preferred_element_type=jnp.float32)
        acc = pv if acc is None else acc + pv
    return acc / den


def _na_kernel(q_ref, k_ref, v_ref, kc_ref, vc_ref, pt_ref, o_ref, bias_ref):
    n = q_ref.shape[0]
    qb = NA_QROWS * GRID_W
    kb = NA_KROWS * GRID_W
    nblk = n // qb
    nt = (((1,), (1,)), ((), ()))
    kc = kc_ref[...]
    vc = vc_ref[...]
    for kind, (r0, rlo) in enumerate(_na_block_types(n)):
        for qi in range(NA_QROWS):
            for kp in range(NA_KROWS // 2):
                ok0 = _na_row_ok(n, r0, rlo, qi, 2 * kp)
                ok1 = _na_row_ok(n, r0, rlo, qi, 2 * kp + 1)
                e = rlo + 2 * kp - (r0 + qi) + WIN_H - 1
                rows, cols = slice(qi * GRID_W, (qi + 1) * GRID_W), slice(kp * LANES, (kp + 1) * LANES)
                if ok0 or ok1:
                    bias_ref[kind, rows, cols] = pt_ref[0, 0 if (ok0 and ok1) else (1 if ok0 else 2), e + 1]
                else:
                    bias_ref[kind, rows, cols] = jnp.full((GRID_W, LANES), NEG_INF, jnp.float32)

    def body(b, carry):
        q0 = pl.multiple_of(b * qb, qb)
        ks = pl.multiple_of(jnp.clip(b * qb - (NA_KROWS - NA_QROWS) // 2 * GRID_W, 0, n - kb), 2 * LANES)
        kind = jnp.where(b == 0, 1, jnp.where(b == nblk - 1, 2, 0))
        q = q_ref[pl.ds(q0, qb), :]
        s_w = lax.dot_general(q, k_ref[pl.ds(ks, kb), :], nt, preferred_element_type=jnp.float32) + bias_ref[kind]
        s_c = lax.dot_general(q, kc, nt, preferred_element_type=jnp.float32)
        o = _softmax_pv([(s_w, v_ref[pl.ds(ks, kb), :]), (s_c, vc)])
        o_ref[pl.ds(q0, qb), :] = o.astype(o_ref.dtype)
        return carry

    lax.fori_loop(0, nblk, body, 0)


def _neighbourhood_attn(qk, p, qk_c, p_c, pair_tables):
    n = qk.shape[0]
    m = qk_c.shape[0]

    def full(col0):
        return pl.BlockSpec((n, HEAD_DIM), lambda h: (0, col0 + h))

    def ctx(col0):
        return pl.BlockSpec((m, HEAD_DIM), lambda h: (0, col0 + h))

    return pl.pallas_call(
        _na_kernel,
        grid=(C_HEADS,),
        in_specs=[full(QK_CQ), full(QK_CK), full(P_CV), ctx(QK_CK), ctx(P_CV),
                  pl.BlockSpec((1, 3, 2 * WIN_H, GRID_W, LANES), lambda h: (h, 0, 0, 0, 0))],
        out_specs=pl.BlockSpec((n, HEAD_DIM), lambda h: (0, h)),
        out_shape=jax.ShapeDtypeStruct((n, C_WIDTH), jnp.bfloat16),
        scratch_shapes=[pltpu.VMEM((3, NA_QROWS * GRID_W, NA_KROWS * GRID_W), jnp.float32)],
        compiler_params=pltpu.CompilerParams(
            dimension_semantics=("parallel",), vmem_limit_bytes=VMEM_LIMIT_BYTES),
        name="neighbourhood_attn",
    )(qk, qk, p, qk_c, p_c, pair_tables)


def _swa_kernel(q_ref, k_ref, v_ref, kc_ref, vc_ref, sink_ref, o_ref):
    n = q_ref.shape[0]
    nblk = n // SWA_QB
    nt = (((1,), (1,)), ((), ()))
    kc = kc_ref[...]
    vc = vc_ref[...]
    kv = pl.program_id(0)
    sink = jnp.concatenate(
        [jnp.broadcast_to(sink_ref[pl.ds(kv * D_GROUP + g, 1), 0:1], (SWA_QB, 1)) for g in range(D_GROUP)], axis=0)
    q_off = jnp.bitwise_and(lax.broadcasted_iota(jnp.int32, (D_GROUP * SWA_QB, SWA_KB), 0), SWA_QB - 1)
    k_off = lax.broadcasted_iota(jnp.int32, (D_GROUP * SWA_QB, SWA_KB), 1)
    rel = q_off - k_off

    def body(b, carry):
        q0 = pl.multiple_of(b * SWA_QB, SWA_QB)
        ks = pl.multiple_of(jnp.clip(q0 - WINDOW, 0, n - SWA_KB), LANES)
        q = jnp.concatenate([q_ref[pl.ds(q0, SWA_QB), g * HEAD_DIM:(g + 1) * HEAD_DIM] for g in range(D_GROUP)], axis=0)
        s_w = lax.dot_general(q, k_ref[pl.ds(ks, SWA_KB), :], nt, preferred_element_type=jnp.float32)
        s_w = jnp.where(jnp.abs(rel + (q0 - ks)) <= WINDOW, s_w, NEG_INF)
        s_c = lax.dot_general(q, kc, nt, preferred_element_type=jnp.float32)
        o = _softmax_pv([(s_w, v_ref[pl.ds(ks, SWA_KB), :]), (s_c, vc)], extra_logit=sink)
        for g in range(D_GROUP):
            o_ref[pl.ds(q0, SWA_QB), g * HEAD_DIM:(g + 1) * HEAD_DIM] = o[g * SWA_QB:(g + 1) * SWA_QB].astype(o_ref.dtype)
        return carry

    lax.fori_loop(0, nblk, body, 0)


def _window_attn(qk, p, qk_c, p_c, sink_rows):
    n = qk.shape[0]
    m = qk_c.shape[0]
    gw = D_GROUP * HEAD_DIM
    return pl.pallas_call(
        _swa_kernel,
        grid=(D_KV_HEADS,),
        in_specs=[pl.BlockSpec((n, gw), lambda kv: (0, QK_DQ // D_GROUP + kv)),
                  pl.BlockSpec((n, HEAD_DIM), lambda kv: (0, QK_DK + kv)),
                  pl.BlockSpec((n, HEAD_DIM), lambda kv: (0, P_DV + kv)),
                  pl.BlockSpec((m, HEAD_DIM), lambda kv: (0, QK_DK + kv)),
                  pl.BlockSpec((m, HEAD_DIM), lambda kv: (0, P_DV + kv)),
                  pl.BlockSpec((D_HEADS, LANES), lambda kv: (0, 0))],
        out_specs=pl.BlockSpec((n, gw), lambda kv: (0, kv)),
        out_shape=jax.ShapeDtypeStruct((n, D_WIDTH), jnp.bfloat16),
        compiler_params=pltpu.CompilerParams(
            dimension_semantics=("parallel",), vmem_limit_bytes=VMEM_LIMIT_BYTES),
        name="window_attn",
    )(qk, qk, p, qk_c, p_c, sink_rows)


def _mixer_cd(h_c, h_l, w_in, qk_g, rpb, sink, w_out):
    n = h_l.shape[0]
    m = h_c.shape[0]
    scale = HEAD_DIM ** -0.5
    p_l = _matmul(h_l, w_in, jnp.bfloat16)
    p_c = _matmul(h_c, w_in[:, :CD_KV_COLS], jnp.bfloat16)
    gains = jnp.stack([qk_g[0] * scale, qk_g[1], qk_g[2] * scale, qk_g[3]]).astype(jnp.float32)[:, None, :]
    cos, sin = _rope_tables(n)
    cs_l = (np.stack([np.ones_like(cos), cos]), np.stack([np.zeros_like(sin), sin]))
    cs_c = (np.ones((1, m, HEAD_DIM), np.float32), np.zeros((1, m, HEAD_DIM), np.float32))
    src = list(range(P_CK, P_CK + 8)) + [P_DK, P_DK + 1] + list(range(P_CQ, P_CQ + 8)) + [P_CQ, P_CQ] + list(range(P_DQ, P_DQ + 8))
    gain = [1] * 8 + [3] * 2 + [0] * 8 + [0, 0] + [2] * 8
    rope = [0] * 8 + [1] * 2 + [0] * 8 + [0, 0] + [1] * 8
    tbl_l = jnp.asarray(np.array([src, gain, rope], np.int32))
    tbl_c = jnp.asarray(np.array([src[:10], gain[:10], [0] * 10], np.int32))
    qk_l = _qk_prep(p_l, tbl_l, gains, jnp.asarray(cs_l[0]), jnp.asarray(cs_l[1]))
    qk_c = _qk_prep(p_c, tbl_c, gains, jnp.asarray(cs_c[0]), jnp.asarray(cs_c[1]))
    c_out = _neighbourhood_attn(qk_l, p_l, qk_c, p_c, _na_pair_tables(rpb))
    sink_rows = jnp.broadcast_to(sink.astype(jnp.float32)[:, None], (D_HEADS, LANES))
    d_out = _window_attn(qk_l, p_l, qk_c, p_c, sink_rows)
    return _matmul(jnp.concatenate([c_out, d_out], axis=-1), w_out)


SCAN_L = 128
GLA_SUB = 4
N_SCAN_STATES = 2 * A_HEADS
GATE_LR_LANE = 4 * A_HEADS


def _log_sigmoid(x):
    return jnp.minimum(x, 0.0) - jnp.log(1.0 + jnp.exp(-jnp.abs(x)))


def _tri_mask(rev):
    r = lax.broadcasted_iota(jnp.int32, (SCAN_L, SCAN_L), 0)
    c = lax.broadcasted_iota(jnp.int32, (SCAN_L, SCAN_L), 1)
    return (c >= r) if rev else (c <= r)


def _cumsum_rows(tri, x):
    hi = x.astype(jnp.bfloat16)
    r1 = x - hi.astype(jnp.float32)
    mid = r1.astype(jnp.bfloat16)
    lo = (r1 - mid.astype(jnp.float32)).astype(jnp.bfloat16)
    dot = lambda p: jnp.dot(tri, p, preferred_element_type=jnp.float32)
    return dot(hi) + dot(mid) + dot(lo)


def _conv_kernel(x_ref, w_ref, b_ref, o_ref, *, seg, scale):
    x = x_ref[...].astype(jnp.float32)
    n = x.shape[0]
    t = lax.broadcasted_iota(jnp.int32, x.shape, 0)
    first = (t == 0) | (t == seg)
    last = (t == seg - 1) | (t == n - 1)
    prev = jnp.where(first, 0.0, pltpu.roll(x, 1, 0))
    nxt = jnp.where(last, 0.0, pltpu.roll(x, n - 1, 0))
    y = w_ref[0:1, :] * prev + w_ref[1:2, :] * x + w_ref[2:3, :] * nxt + b_ref[...]
    y = y * jax.nn.sigmoid(y)
    j = pl.program_id(0)
    o_ref[...] = (y * jnp.where(j >= A_HEADS, scale, 1.0)).astype(o_ref.dtype)


def _short_conv_silu(p, conv_w, conv_b, seg):
    t = p.shape[0]
    nblk = 2 * A_WIDTH // LANES
    return pl.pallas_call(
        functools.partial(_conv_kernel, seg=seg, scale=HEAD_DIM ** -0.5),
        grid=(nblk,),
        in_specs=[pl.BlockSpec((t, LANES), lambda j: (0, j)),
                  pl.BlockSpec((8, LANES), lambda j: (0, j)),
                  pl.BlockSpec((1, LANES), lambda j: (0, j))],
        out_specs=pl.BlockSpec((t, LANES), lambda j: (0, j)),
        out_shape=jax.ShapeDtypeStruct((t, 2 * A_WIDTH), jnp.bfloat16),
        compiler_params=pltpu.CompilerParams(
            dimension_semantics=("parallel",), vmem_limit_bytes=VMEM_LIMIT_BYTES),
        name="short_conv_silu",
    )(p, jnp.pad(conv_w, ((0, 8 - CONV_W), (0, 0))), conv_b[None, :])


def _scan_chunk_index(j, n_ctx_chunks, n_chunks, rev):
    if not rev:
        return j
    return jnp.where(j < n_ctx_chunks, n_ctx_chunks - 1 - j, n_chunks + n_ctx_chunks - 1 - j)


def _mlstm_kernel(qf_ref, kf_ref, vf_ref, gf_ref, qb_ref, kb_ref, vb_ref, gb_ref, gbias_ref,
                  yf_ref, yb_ref, c_ref, m_ref):
    @pl.when(pl.program_id(0) == 0)
    def _():
        c_ref[...] = jnp.zeros_like(c_ref)
        m_ref[...] = jnp.zeros_like(m_ref)

    lane = lax.broadcasted_iota(jnp.int32, (SCAN_L, LANES), 1)
    is_forget = jnp.bitwise_and(lane, 2 * A_HEADS - 1) >= A_HEADS
    ones = jnp.ones((SCAN_L, HEAD_DIM), jnp.bfloat16)
    nt = (((1,), (1,)), ((), ()))
    tn = (((0,), (0,)), ((), ()))
    for z, (q_ref, k_ref, v_ref, g_ref, y_ref) in enumerate(
            ((qf_ref, kf_ref, vf_ref, gf_ref, yf_ref), (qb_ref, kb_ref, vb_ref, gb_ref, yb_ref))):
        rev = z == 1
        mask = _tri_mask(rev)
        gates = g_ref[...] + gbias_ref[...]
        gates = jnp.where(is_forget, _log_sigmoid(gates), gates)
        csum = _cumsum_rows(mask.astype(jnp.bfloat16), jnp.where(is_forget, gates, 0.0))
        gates_t = gates.T
        csum_t = csum.T
        end = 0 if rev else SCAN_L - 1
        heads = []
        for h in range(A_HEADS):
            li, lf = z * 2 * A_HEADS + h, z * 2 * A_HEADS + A_HEADS + h
            sl = slice(h * HEAD_DIM, (h + 1) * HEAD_DIM)
            q, k, v = q_ref[:, sl], k_ref[:, sl], v_ref[:, sl]
            b_col, b_row = csum[:, lf:lf + 1], csum_t[lf:lf + 1, :]
            i_col, i_row = gates[:, li:li + 1], gates_t[li:li + 1, :]
            st = z * A_HEADS + h
            m_prev = m_ref[st][0:1, 0:1]
            dlog = jnp.where(mask, b_col - b_row + i_row, NEG_INF)
            m_inter = b_col + m_prev
            m_t = jnp.maximum(m_inter, jnp.max(dlog, axis=-1, keepdims=True))
            s = lax.dot_general(q, k, nt, preferred_element_type=jnp.float32)
            qc = jnp.dot(q, c_ref[st].astype(jnp.bfloat16), preferred_element_type=jnp.float32)
            heads.append(dict(sl=sl, st=st, k=k, v_ext=jnp.concatenate([v, ones], axis=1), b_col=b_col, i_col=i_col,
                              m_prev=m_prev, dlog=dlog, m_inter=m_inter, m_t=m_t, s=s, qc=qc))
        for hd in heads:
            sc = (hd["s"] * jnp.exp(hd["dlog"] - hd["m_t"])).astype(jnp.bfloat16)
            w_inter = jnp.exp(hd["m_inter"] - hd["m_t"])
            r = jnp.dot(sc, hd["v_ext"], preferred_element_type=jnp.float32) + w_inter * hd["qc"]
            num, den = r[:, :HEAD_DIM], r[:, HEAD_DIM:]
            y_ref[:, hd["sl"]] = num / jnp.maximum(jnp.abs(den), jnp.exp(-hd["m_t"]))
        for hd in heads:
            st = hd["st"]
            m_new = hd["m_t"][end:end + 1, :]
            b_end = hd["b_col"][end:end + 1, :]
            w_end = jnp.exp(b_end - hd["b_col"] + hd["i_col"] - m_new)
            decay = jnp.exp(b_end + hd["m_prev"] - m_new)
            kw = (hd["k"].astype(jnp.float32) * w_end).astype(jnp.bfloat16)
            c_ref[st] = decay * c_ref[st] + lax.dot_general(kw, hd["v_ext"], tn, preferred_element_type=jnp.float32)
            m_ref[st] = jnp.broadcast_to(m_new, (8, LANES))


def _scan_specs(n_ctx_chunks, n_chunks, width, col0, rev):
    return pl.BlockSpec((SCAN_L, width),
                        lambda j: (_scan_chunk_index(j, n_ctx_chunks, n_chunks, rev), col0))


def _mlstm(qk, p, gates, gate_bias, seg):
    t = qk.shape[0]
    nck = t // SCAN_L
    ncc = seg // SCAN_L
    ins, specs = [], []
    for rev in (False, True):
        ins += [qk, qk, p, gates]
        specs += [_scan_specs(ncc, nck, A_WIDTH, 0, rev), _scan_specs(ncc, nck, A_WIDTH, 1, rev),
                  _scan_specs(ncc, nck, A_WIDTH, 2, rev), _scan_specs(ncc, nck, LANES, 0, rev)]
    return pl.pallas_call(
        _mlstm_kernel,
        grid=(nck,),
        in_specs=specs + [pl.BlockSpec((1, LANES), lambda j: (0, 0))],
        out_specs=[_scan_specs(ncc, nck, A_WIDTH, 0, False), _scan_specs(ncc, nck, A_WIDTH, 0, True)],
        out_shape=[jax.ShapeDtypeStruct((t, A_WIDTH), jnp.float32)] * 2,
        scratch_shapes=[pltpu.VMEM((N_SCAN_STATES, HEAD_DIM, 2 * HEAD_DIM), jnp.float32),
                        pltpu.VMEM((N_SCAN_STATES, 8, LANES), jnp.float32)],
        compiler_params=pltpu.CompilerParams(
            dimension_semantics=("arbitrary",), vmem_limit_bytes=VMEM_LIMIT_BYTES),
        name="mlstm_scan",
    )(*ins, gate_bias)


def _gla_kernel(qf_ref, kf_ref, vf_ref, gf_ref, qb_ref, kb_ref, vb_ref, gb_ref, w2_ref, ab_ref,
                of_ref, ob_ref, s_ref):
    @pl.when(pl.program_id(0) == 0)
    def _():
        s_ref[...] = jnp.zeros_like(s_ref)

    bf = jnp.bfloat16
    row = lax.broadcasted_iota(jnp.int32, (SCAN_L, LANES), 0)
    lane = lax.broadcasted_iota(jnp.int32, (SCAN_L, LANES), 1)
    head_lane = (lane < B_KEY_DIM, lane >= B_KEY_DIM)
    r_sq = lax.broadcasted_iota(jnp.int32, (SCAN_L, SCAN_L), 0)
    c_sq = lax.broadcasted_iota(jnp.int32, (SCAN_L, SCAN_L), 1)
    in_sub = jnp.bitwise_and(row, GLA_SUB - 1)
    sel_r = lax.broadcasted_iota(jnp.int32, (LANES, 2 * LANES), 0)
    sel_c = lax.broadcasted_iota(jnp.int32, (LANES, 2 * LANES), 1)
    head_sum = ((sel_r < B_KEY_DIM) == (sel_c < LANES)).astype(bf)
    nt = (((1,), (1,)), ((), ()))
    tn = (((0,), (0,)), ((), ()))
    for z, (q_ref, k_ref, v_ref, g_ref, o_ref) in enumerate(
            ((qf_ref, kf_ref, vf_ref, gf_ref, of_ref), (qb_ref, kb_ref, vb_ref, gb_ref, ob_ref))):
        rev = z == 1
        tri = _tri_mask(rev).astype(bf)
        lr = g_ref[...].astype(bf)
        end = 0 if rev else SCAN_L - 1
        levels = []
        m = SCAN_L // 2
        while m >= GLA_SUB:
            shift = (2 * m).bit_length() - 1
            upper = jnp.bitwise_and(row, 2 * m - 1) >= m
            mid = jnp.left_shift(jnp.right_shift(r_sq, shift), shift) + (m if rev else m - 1)
            levels.append(dict(
                keys=upper if rev else ~upper,
                queries=~upper if rev else upper,
                pick_mid=(c_sq == mid).astype(bf),
                same=jnp.right_shift(r_sq, shift) == jnp.right_shift(c_sq, shift)))
            m //= 2
        pick_all = jnp.concatenate([lv["pick_mid"] for lv in levels], axis=0)
        for pr in range(B_HEADS // 2):
            ls = slice(pr * LANES, (pr + 1) * LANES)
            za = jnp.dot(lr, w2_ref[z, :, ls], preferred_element_type=jnp.float32) + ab_ref[z:z + 1, ls]
            b = _cumsum_rows(tri, _log_sigmoid(za) * (1.0 / GLA_TAU))
            q = q_ref[:, ls].astype(jnp.float32) * (B_KEY_DIM ** -0.5)
            k = k_ref[:, ls].astype(jnp.float32)
            v = v_ref[:, 2 * pr * HEAD_DIM:(2 * pr + 2) * HEAD_DIM]
            b_hi = b.astype(bf)
            b_lo = (b - b_hi.astype(jnp.float32)).astype(bf)
            picked = jnp.dot(pick_all, jnp.concatenate([b_hi, b_lo], axis=1), preferred_element_type=jnp.float32)
            q_ms, k_ms = [], []
            for li, lv in enumerate(levels):
                b_mid = picked[li * SCAN_L:(li + 1) * SCAN_L, :LANES] + picked[li * SCAN_L:(li + 1) * SCAN_L, LANES:]
                q_ms.append((q * jnp.exp(jnp.where(lv["queries"], b - b_mid, NEG_INF))).astype(bf))
                k_m = k * jnp.exp(jnp.where(lv["keys"], b_mid - b, NEG_INF))
                k_ms.append([jnp.where(head_lane[hh], k_m, 0.0).astype(bf) for hh in range(2)])
            parts = [[lax.dot_general(q_ms[li], k_ms[li][hh], nt, preferred_element_type=jnp.float32)
                      for hh in range(2)] for li in range(len(levels))]
            fs = []
            for d in range(GLA_SUB):
                shift = (SCAN_L - d) % SCAN_L if rev else d
                k_s = pltpu.roll(k, shift, 0) if d else k
                b_s = pltpu.roll(b, shift, 0) if d else b
                ok_row = (in_sub <= GLA_SUB - 1 - d) if rev else (in_sub >= d)
                fs.append((q * k_s * jnp.exp(jnp.where(ok_row, b - b_s, NEG_INF))).astype(bf))
            red = jnp.dot(jnp.concatenate(fs, axis=0), head_sum, preferred_element_type=jnp.float32)
            attn = []
            for hh in range(2):
                acc = jnp.zeros((SCAN_L, SCAN_L), jnp.float32)
                for li, lv in enumerate(levels):
                    acc = acc + jnp.where(lv["same"], parts[li][hh], 0.0)
                for d in range(GLA_SUB):
                    on_diag = (c_sq == r_sq + d) if rev else (c_sq == r_sq - d)
                    acc = acc + jnp.where(on_diag, red[d * SCAN_L:(d + 1) * SCAN_L, hh * LANES:(hh + 1) * LANES], 0.0)
                attn.append(acc)
            q_in = q * jnp.exp(b)
            b_end = b[end:end + 1, :]
            k_out = (k * jnp.exp(b_end - b)).astype(bf)
            st = z * (B_HEADS // 2) + pr
            s_t = s_ref[st]
            s_bf = s_t.astype(bf)
            for hh in range(2):
                vs = slice(hh * HEAD_DIM, (hh + 1) * HEAD_DIM)
                qh = jnp.where(head_lane[hh], q_in, 0.0).astype(bf)
                o = (jnp.dot(attn[hh].astype(bf), v[:, vs], preferred_element_type=jnp.float32)
                     + lax.dot_general(qh, s_bf[vs, :], nt, preferred_element_type=jnp.float32))
                o_ref[:, (2 * pr + hh) * HEAD_DIM:(2 * pr + hh + 1) * HEAD_DIM] = o
            s_ref[st] = s_t * jnp.exp(b_end) + lax.dot_general(v, k_out, tn, preferred_element_type=jnp.float32)


def _gla(p, gates, w2, alpha_b, seg):
    t = p.shape[0]
    nck = t // SCAN_L
    ncc = seg // SCAN_L
    ins, specs = [], []
    for rev in (False, True):
        ins += [p, p, p, gates]
        specs += [_scan_specs(ncc, nck, B_KEY_WIDTH, P_BQ // B_KEY_WIDTH, rev),
                  _scan_specs(ncc, nck, B_KEY_WIDTH, P_BK // B_KEY_WIDTH, rev),
                  _scan_specs(ncc, nck, B_VAL_WIDTH, P_BV // B_VAL_WIDTH, rev),
                  _scan_specs(ncc, nck, LANES, 0, rev)]
    return pl.pallas_call(
        _gla_kernel,
        grid=(nck,),
        in_specs=specs + [pl.BlockSpec((2, LANES, B_KEY_WIDTH), lambda j: (0, 0, 0)),
                          pl.BlockSpec((2, B_KEY_WIDTH), lambda j: (0, 0))],
        out_specs=[_scan_specs(ncc, nck, B_VAL_WIDTH, 0, False), _scan_specs(ncc, nck, B_VAL_WIDTH, 0, True)],
        out_shape=[jax.ShapeDtypeStruct((t, B_VAL_WIDTH), jnp.float32)] * 2,
        scratch_shapes=[pltpu.VMEM((B_HEADS, 2 * HEAD_DIM, LANES), jnp.float32)],
        compiler_params=pltpu.CompilerParams(
            dimension_semantics=("arbitrary",), vmem_limit_bytes=VMEM_LIMIT_BYTES),
        name="gla_scan",
    )(*ins, w2, alpha_b)


def _head_out_kernel(af_ref, ab_ref, bf_ref, bb_ref, gate_ref, g_ref, o_ref):
    is_gla = pl.program_id(1) >= A_HEADS
    y = jnp.where(is_gla, bf_ref[...] + bb_ref[...], af_ref[...] + ab_ref[...])
    yn = y * lax.rsqrt(jnp.mean(y * y, axis=-1, keepdims=True) + EPS) * g_ref[0]
    gate = gate_ref[...].astype(jnp.float32)
    o_ref[...] = (yn * jax.nn.sigmoid(gate) * jnp.where(is_gla, gate, 1.0)).astype(o_ref.dtype)


def _head_out(ya, yb, p, head_g):
    t = p.shape[0]
    bt = _row_tile(t)
    nh = A_HEADS + B_HEADS
    a_spec = pl.BlockSpec((bt, HEAD_DIM), lambda i, j: (i, jnp.minimum(j, A_HEADS - 1)))
    b_spec = pl.BlockSpec((bt, HEAD_DIM), lambda i, j: (i, jnp.maximum(j - A_HEADS, 0)))

    def gate_col(i, j):
        return (i, jnp.where(j < A_HEADS, P_AO + j, P_BR + j - A_HEADS))

    return pl.pallas_call(
        _head_out_kernel,
        grid=(t // bt, nh),
        in_specs=[a_spec, a_spec, b_spec, b_spec,
                  pl.BlockSpec((bt, HEAD_DIM), gate_col),
                  pl.BlockSpec((1, 1, HEAD_DIM), lambda i, j: (j // A_HEADS, 0, 0))],
        out_specs=pl.BlockSpec((bt, HEAD_DIM), lambda i, j: (i, j)),
        out_shape=jax.ShapeDtypeStruct((t, nh * HEAD_DIM), jnp.bfloat16),
        compiler_params=pltpu.CompilerParams(
            dimension_semantics=("parallel", "arbitrary"), vmem_limit_bytes=VMEM_LIMIT_BYTES),
        name="head_out",
    )(ya[0], ya[1], yb[0], yb[1], p, head_g.astype(jnp.float32)[:, None, :])


P_AO = 24
P_BQ, P_BK, P_BV = 4096, 4608, 5120
P_BR = 48


def _mixer_ab(h, w_in, conv_w, conv_b, gate_b, alpha_w2, alpha_b, head_g, w_out, seg):
    bf = jnp.bfloat16
    cols = np.cumsum((0,) + AB_SPLITS)
    pick = lambda *ids: jnp.concatenate([w_in[:, cols[i]:cols[i + 1]] for i in ids], axis=1)
    w_main = pick(0, 1, 2, 3, 5, 6, 7, 8).astype(bf)
    w_gate = jnp.pad(pick(4, 9), ((0, 0), (0, LANES - 4 * A_HEADS - 2 * GLA_RANK))).astype(bf)
    hb = h
    p = _matmul(hb, w_main, bf)
    gates = _matmul(hb, w_gate, jnp.float32)
    gate_bias = jnp.pad(gate_b.reshape(1, -1).astype(jnp.float32), ((0, 0), (0, LANES - 4 * A_HEADS)))
    w2 = jnp.zeros((2, LANES, B_KEY_WIDTH), jnp.float32)
    for z in range(2):
        w2 = w2.at[z, GATE_LR_LANE + z * GLA_RANK:GATE_LR_LANE + (z + 1) * GLA_RANK].set(alpha_w2[z])
    qk = _short_conv_silu(p, conv_w, conv_b, seg)
    ya = _mlstm(qk, p, gates, gate_bias, seg)
    yb = _gla(p, gates, w2.astype(bf), alpha_b.astype(jnp.float32), seg)
    return _matmul(_head_out(ya, yb, p, head_g), w_out)


MOD_ROWS = 256


def _res_mod_kernel(x_ref, y_ref, p_ref, xo_ref, h_ref):
    p = p_ref[0]
    x = x_ref[...] + p[0:1, :] * y_ref[...]
    xo_ref[...] = x
    r = x * lax.rsqrt(jnp.mean(x * x, axis=-1, keepdims=True) + EPS) * p[1:2, :]
    h_ref[...] = (r * (1.0 + p[3:4, :]) + p[2:3, :]).astype(h_ref.dtype)


def _mod_kernel(x_ref, p_ref, h_ref):
    p = p_ref[0]
    x = x_ref[...]
    r = x * lax.rsqrt(jnp.mean(x * x, axis=-1, keepdims=True) + EPS) * p[1:2, :]
    h_ref[...] = (r * (1.0 + p[3:4, :]) + p[2:3, :]).astype(h_ref.dtype)


def _mod_params(gate_c, gate_l, gain, shift_c, shift_l, scale_c, scale_l):
    rows = lambda g, sh, sc: jnp.concatenate(
        [g.reshape(1, -1), gain.reshape(1, -1), sh.reshape(1, -1), sc.reshape(1, -1),
         jnp.zeros((4, gain.shape[-1]), jnp.float32)], axis=0)
    return jnp.stack([rows(gate_c, shift_c, scale_c), rows(gate_l, shift_l, scale_l)]).astype(jnp.float32)


def _res_mod(x, y, params, seg, row0=0, n_rows=None):
    d = x.shape[1]
    n_rows = x.shape[0] - row0 if n_rows is None else n_rows
    bt = MOD_ROWS
    off, seg_blocks = row0 // bt, seg // bt
    row_spec = pl.BlockSpec((bt, d), lambda i: (i + off, 0))
    out_spec = pl.BlockSpec((bt, d), lambda i: (i, 0))
    p_spec = pl.BlockSpec((1, 8, d), lambda i: (jnp.where(i + off < seg_blocks, 0, 1), 0, 0))
    cp = pltpu.CompilerParams(dimension_semantics=("parallel",), vmem_limit_bytes=VMEM_LIMIT_BYTES)
    h_shape = jax.ShapeDtypeStruct((n_rows, d), jnp.bfloat16)
    if y is None:
        return pl.pallas_call(_mod_kernel, grid=(n_rows // bt,), in_specs=[row_spec, p_spec], out_specs=out_spec,
                              out_shape=h_shape, compiler_params=cp, name="modulate")(x, params)
    return pl.pallas_call(
        _res_mod_kernel, grid=(n_rows // bt,), in_specs=[row_spec, row_spec, p_spec],
        out_specs=[out_spec, out_spec],
        out_shape=[jax.ShapeDtypeStruct((n_rows, d), jnp.float32), h_shape],
        compiler_params=cp, name="residual_modulate")(x, y, params)


def _modulation_kernel(c_ref, w_ref, b_ref, o_ref):
    act = c_ref[...]
    act = (act * jax.nn.sigmoid(act)).astype(jnp.bfloat16)
    o_ref[0] = jnp.dot(act, w_ref[0].astype(jnp.bfloat16), preferred_element_type=jnp.float32) + b_ref[0]


def _modulation(c, c_ctx, mod_w, mod_b):
    nl, d, n = mod_w.shape
    cond = jnp.concatenate([c_ctx.reshape(1, d), c.reshape(1, d), jnp.zeros((6, d), jnp.float32)], axis=0)
    bn = n // 8
    return pl.pallas_call(
        _modulation_kernel,
        grid=(nl, n // bn),
        in_specs=[pl.BlockSpec((8, d), lambda l, j: (0, 0)),
                  pl.BlockSpec((1, d, bn), lambda l, j: (l, 0, j)),
                  pl.BlockSpec((1, 1, bn), lambda l, j: (l, 0, j))],
        out_specs=pl.BlockSpec((1, 8, bn), lambda l, j: (l, 0, j)),
        out_shape=jax.ShapeDtypeStruct((nl, 8, n), jnp.float32),
        compiler_params=pltpu.CompilerParams(
            dimension_semantics=("parallel", "parallel"), vmem_limit_bytes=VMEM_LIMIT_BYTES),
        name="modulation",
    )(cond, mod_w, mod_b[:, None, :])


def kernel(x, c, ctx, c_ctx, mod_w, mod_b, norm_g, ab_w_in, ab_conv_w, ab_conv_b, ab_gate_b, ab_alpha_w2,
           ab_alpha_b, ab_head_g, ab_w_out, cd_w_in, cd_qk_g, cd_rpb, cd_sink, cd_w_out, peer_w_q,
           peer_sub_keys, peer_u, peer_v):
    assert DEPTH == 2
    bf = jnp.bfloat16
    m = ctx.shape[1]
    xs = jnp.concatenate([ctx[0], x[0]], axis=0)
    ub = _quant_rows(peer_u)
    vb = _quant_rows(peer_v)
    mod_all = _modulation(c, c_ctx, mod_w, mod_b)
    mods = [(jnp.split(mod_all[l, 0], 6), jnp.split(mod_all[l, 1], 6)) for l in range(DEPTH)]
    one = jnp.ones_like(mods[0][0][0])

    def params(l, sub, gate_c, gate_l):
        mc, ml = mods[l]
        return _mod_params(gate_c, gate_l, norm_g[l, sub], mc[3 * sub], ml[3 * sub], mc[3 * sub + 1], ml[3 * sub + 1])

    (mc0, ml0), (mc1, ml1) = mods
    h = _res_mod(xs, None, params(0, 0, one, one), m)
    y = _mixer_ab(h, ab_w_in[0], ab_conv_w[0], ab_conv_b[0], ab_gate_b[0], ab_alpha_w2[0], ab_alpha_b[0],
                  ab_head_g[0], ab_w_out[0].astype(bf), m)
    xs, h = _res_mod(xs, y, params(0, 1, mc0[2], ml0[2]), m)
    y = _peer(h, peer_w_q[0].astype(bf), peer_sub_keys[0].astype(bf), ub, vb, 0)
    p10 = params(1, 0, mc0[5], ml0[5])
    _, hc = _res_mod(xs, y, p10, m, 0, m)
    xl, hl = _res_mod(xs, y, p10, m, m)
    y = _mixer_cd(hc, hl, cd_w_in[0].astype(bf), cd_qk_g[0], cd_rpb[0], cd_sink[0], cd_w_out[0].astype(bf))
    xl, h = _res_mod(xl, y, params(1, 1, ml1[2], ml1[2]), 0)
    y = _peer(h, peer_w_q[1].astype(bf), peer_sub_keys[1].astype(bf), ub, vb, 1)
    return (xl + ml1[5] * y)[None]
```

```python
import functools

import numpy as np
import jax
import jax.numpy as jnp
from jax import lax
from jax.experimental import pallas as pl
from jax.experimental.pallas import tpu as pltpu

D_MODEL = 2048
SEQ = 8192
DEPTH = 2
GRID_W = 64
CTX_LEN = 256
HEAD_DIM = 128
N_GROUP_HEADS = 8
EPS = 1e-6
A_HEADS = 8
A_WIDTH = 1024
CONV_W = 3
B_HEADS = 8
B_KEY_DIM = 64
B_KEY_WIDTH = 512
B_VAL_WIDTH = 1024
GLA_RANK = 16
GLA_TAU = 16.0
SCAN_CHUNK = 64
C_HEADS = 8
C_WIDTH = 1024
WIN_H = 8
WIN_W = 16
D_HEADS = 8
D_KV_HEADS = 2
D_GROUP = 4
D_WIDTH = 1024
D_KV_WIDTH = 256
WINDOW = 128
ROPE_BASE = 10000.0
PEER_HEADS = 8
PEER_NKEYS = 128
PEER_EXPERTS = PEER_NKEYS * PEER_NKEYS
PEER_QDIM = 256
PEER_TOPK = 16
PEER_SLOTS = PEER_HEADS * PEER_TOPK
AB_SPLITS = (A_WIDTH, A_WIDTH, A_WIDTH, A_WIDTH, 4 * A_HEADS,
             B_KEY_WIDTH, B_KEY_WIDTH, B_VAL_WIDTH, B_VAL_WIDTH, 2 * GLA_RANK)
CD_SPLITS = (C_WIDTH, C_WIDTH, D_KV_WIDTH, D_KV_WIDTH, C_WIDTH, D_WIDTH)
CD_KV_COLS = sum(CD_SPLITS[:4])

VMEM_LIMIT_BYTES = 56 * 1024 * 1024
LANES = 128


def _row_tile(m):
    for t in (1024, 768, 512, 256, 128):
        if m % t == 0:
            return t
    raise ValueError(f"unsupported row count {m}")


def _mm_kernel(x_ref, w_ref, o_ref):
    o_ref[...] = jnp.dot(x_ref[...], w_ref[...], preferred_element_type=jnp.float32).astype(o_ref.dtype)


def _matmul(x, w, out_dtype=jnp.float32):
    m, k = x.shape
    n = w.shape[1]
    bn = 1024 if n % 1024 == 0 else min(512, n)
    n_pad = -(-n // bn) * bn
    if n_pad != n:
        w = jnp.pad(w, ((0, 0), (0, n_pad - n)))
    bm = _row_tile(m)
    out = pl.pallas_call(
        _mm_kernel,
        grid=(m // bm, n_pad // bn),
        in_specs=[pl.BlockSpec((bm, k), lambda i, j: (i, 0)),
                  pl.BlockSpec((k, bn), lambda i, j: (0, j))],
        out_specs=pl.BlockSpec((bm, bn), lambda i, j: (i, j)),
        out_shape=jax.ShapeDtypeStruct((m, n_pad), out_dtype),
        compiler_params=pltpu.CompilerParams(
            dimension_semantics=("parallel", "arbitrary"), vmem_limit_bytes=VMEM_LIMIT_BYTES),
        name="matmul",
    )(x.astype(jnp.bfloat16), w.astype(jnp.bfloat16))
    return out[:, :n] if n_pad != n else out


ROUTE_TOKENS = (768, 512, 256, 128)
GATE_TOKENS = 128
NEG_INF = float("-inf")


def _top16(s, key, big):
    slot = lax.broadcasted_iota(jnp.int32, (PEER_TOPK, s.shape[1]), 0)
    vals = jnp.zeros((PEER_TOPK, s.shape[1]), jnp.float32)
    keys = jnp.zeros((PEER_TOPK, s.shape[1]), jnp.float32)
    for it in range(PEER_TOPK):
        m = jnp.max(s, axis=0, keepdims=True)
        k = jnp.min(jnp.where(s == m, key, big), axis=0, keepdims=True)
        vals = jnp.where(slot == it, m, vals)
        keys = jnp.where(slot == it, k, keys)
        s = jnp.where(key == k, NEG_INF, s)
    return vals, keys


def _route_kernel(h_ref, wq_ref, k_ref, i1_ref, i2_ref, g_ref):
    bt = h_ref.shape[0]
    q = jnp.dot(h_ref[...], wq_ref[...], preferred_element_type=jnp.float32).astype(jnp.bfloat16)
    nt = (((1,), (1,)), ((), ()))
    s1 = lax.dot_general(k_ref[0, 0], q[:, :PEER_NKEYS], nt, preferred_element_type=jnp.float32)
    s2 = lax.dot_general(k_ref[0, 1], q[:, PEER_NKEYS:], nt, preferred_element_type=jnp.float32)
    row = lax.broadcasted_iota(jnp.int32, (PEER_NKEYS, bt), 0).astype(jnp.float32)
    v1, r1 = _top16(s1, row, float(PEER_NKEYS))
    v2, r2 = _top16(s2, row, float(PEER_NKEYS))
    i16 = lax.broadcasted_iota(jnp.int32, (16, bt), 0).astype(jnp.float32)
    i8 = lax.broadcasted_iota(jnp.int32, (8, bt), 0).astype(jnp.float32)
    cand = [v1[0:1] + v2]
    flat = [i16]
    for a in range(1, 8):
        cand.append(v1[a:a + 1] + v2[0:8])
        flat.append(i8 + float(a * PEER_TOPK))
    cand.append(v1[8:16] + v2[0:1])
    flat.append((i8 + 8.0) * float(PEER_TOPK))
    best, key = _top16(jnp.concatenate(cand, axis=0), jnp.concatenate(flat, axis=0), float(PEER_TOPK * PEER_TOPK))
    key = key.astype(jnp.int32)
    a_sel = jnp.right_shift(key, 4)
    b_sel = jnp.bitwise_and(key, PEER_TOPK - 1)
    e1 = jnp.zeros_like(r1)
    e2 = jnp.zeros_like(r2)
    for r in range(PEER_TOPK):
        e1 = jnp.where(a_sel == r, r1[r:r + 1], e1)
        e2 = jnp.where(b_sel == r, r2[r:r + 1], e2)
    e1 = e1.astype(jnp.int32)
    e2 = e2.astype(jnp.int32)
    ex = jnp.exp(best - best[0:1])
    i1_ref[...] = e1
    i2_ref[...] = e2
    g_ref[...] = ex / jnp.sum(ex, axis=0, keepdims=True)


def _route(hb, wq, keys):
    t, d = hb.shape
    bt = next(b for b in ROUTE_TOKENS if t % b == 0)
    slot_spec = pl.BlockSpec((PEER_TOPK, bt), lambda i, h: (h, i))
    return pl.pallas_call(
        _route_kernel,
        grid=(t // bt, PEER_HEADS),
        in_specs=[pl.BlockSpec((bt, d), lambda i, h: (i, 0)),
                  pl.BlockSpec((d, PEER_QDIM), lambda i, h: (0, h)),
                  pl.BlockSpec((1, 2, PEER_NKEYS, PEER_QDIM // 2), lambda i, h: (h, 0, 0, 0))],
        out_specs=[slot_spec, slot_spec, slot_spec],
        out_shape=[jax.ShapeDtypeStruct((PEER_SLOTS, t), jnp.int32),
                   jax.ShapeDtypeStruct((PEER_SLOTS, t), jnp.int32),
                   jax.ShapeDtypeStruct((PEER_SLOTS, t), jnp.float32)],
        compiler_params=pltpu.CompilerParams(
            dimension_semantics=("parallel", "arbitrary"), vmem_limit_bytes=VMEM_LIMIT_BYTES),
        name="peer_route",
    )(hb, wq, keys)


GATE_GROUP = 16


def _gate_matrix_kernel(i1_ref, i2_ref, g_ref, o_ref, i1_t, g_t, stage):
    tb = o_ref.shape[1]
    i1_t[...] = i1_ref[...].T
    g_t[...] = g_ref[...].T
    row_id = lax.broadcasted_iota(jnp.int32, (PEER_NKEYS, PEER_SLOTS), 0)
    lane_id = lax.broadcasted_iota(jnp.int32, (PEER_SLOTS, PEER_NKEYS), 1)
    zero = jnp.zeros((PEER_NKEYS, PEER_SLOTS), jnp.bfloat16)

    for tp in range(tb // 2):
        ta = 2 * tp
        tb_ = ta + 1
        ca = jnp.where(row_id == i1_t[ta:ta + 1, :], g_t[ta:ta + 1, :], 0.0).astype(jnp.bfloat16)
        cb = jnp.where(row_id == i1_t[tb_:tb_ + 1, :], g_t[tb_:tb_ + 1, :], 0.0).astype(jnp.bfloat16)
        oa = jnp.where(i2_ref[:, ta:ta + 1] == lane_id, 1.0, 0.0).astype(jnp.bfloat16)
        ob = jnp.where(i2_ref[:, tb_:tb_ + 1] == lane_id, 1.0, 0.0).astype(jnp.bfloat16)
        lhs = jnp.concatenate([ca, cb], axis=1)
        rhs = jnp.concatenate([jnp.concatenate([oa, zero], axis=1),
                               jnp.concatenate([zero, ob], axis=1)], axis=0)
        out = jnp.dot(lhs, rhs, preferred_element_type=jnp.float32)
        stage[ta % GATE_GROUP] = out[:, :PEER_NKEYS]
        stage[tb_ % GATE_GROUP] = out[:, PEER_NKEYS:]
        if tb_ % GATE_GROUP == GATE_GROUP - 1:
            t0 = tb_ + 1 - GATE_GROUP
            o_ref[:, t0:t0 + GATE_GROUP, :] = jnp.swapaxes(stage[...], 0, 1).astype(o_ref.dtype)


def _gate_matrix(i1, i2, g):
    t = i1.shape[1]
    tb = GATE_TOKENS
    return pl.pallas_call(
        _gate_matrix_kernel,
        grid=(t // tb,),
        in_specs=[pl.BlockSpec((PEER_SLOTS, tb), lambda i: (0, i))] * 3,
        out_specs=pl.BlockSpec((PEER_NKEYS, tb, PEER_NKEYS), lambda i: (0, i, 0)),
        out_shape=jax.ShapeDtypeStruct((PEER_NKEYS, t, PEER_NKEYS), jnp.bfloat16),
        scratch_shapes=[pltpu.VMEM((tb, PEER_SLOTS), jnp.int32),
                        pltpu.VMEM((tb, PEER_SLOTS), jnp.float32),
                        pltpu.VMEM((GATE_GROUP, PEER_NKEYS, PEER_NKEYS), jnp.float32)],
        compiler_params=pltpu.CompilerParams(
            dimension_semantics=("parallel",), vmem_limit_bytes=VMEM_LIMIT_BYTES),
        name="peer_gate_matrix",
    )(i1, i2, g)


def _gelu_tanh(x):
    return 0.5 * x * (1.0 + jnp.tanh(0.7978845608028654 * (x + 0.044715 * x * x * x)))


FP8 = jnp.float8_e4m3fn
PEER_ROW_GROUPS = 2
FP8_TARGET = 240.0


def _row_scale(x):
    amax = jnp.max(jnp.abs(x), axis=-1, keepdims=True)
    pos = amax > 0.0
    return jnp.where(pos, FP8_TARGET / amax, 1.0), jnp.where(pos, amax * (1.0 / FP8_TARGET), 1.0)


def _quant_kernel(x_ref, o_ref, s_ref):
    x = x_ref[0]
    scale, inv = _row_scale(x)
    o_ref[0] = (x * scale).astype(o_ref.dtype)
    s_ref[0] = jnp.broadcast_to(inv, s_ref.shape[1:])


def _quant_rows(x):
    nl, r, c = x.shape
    br = 1024
    q, s = pl.pallas_call(
        _quant_kernel,
        grid=(nl, r // br),
        in_specs=[pl.BlockSpec((1, br, c), lambda l, i: (l, i, 0))],
        out_specs=[pl.BlockSpec((1, br, c), lambda l, i: (l, i, 0)),
                   pl.BlockSpec((1, br, LANES), lambda l, i: (l, i, 0))],
        out_shape=[jax.ShapeDtypeStruct(x.shape, FP8), jax.ShapeDtypeStruct((nl, r, LANES), jnp.float32)],
        compiler_params=pltpu.CompilerParams(
            dimension_semantics=("parallel", "parallel"), vmem_limit_bytes=VMEM_LIMIT_BYTES),
        name="quant_rows",
    )(x)
    return q, s[:, :, 0][:, None, :]


def _peer_dense_kernel(h_ref, u_ref, su_ref, v_ref, sv_ref, g_ref, o_ref, h8_ref, sh_ref):
    j = pl.program_id(1)

    @pl.when(j == 0)
    def _():
        h = h_ref[...].astype(jnp.float32)
        scale, inv = _row_scale(h)
        h8_ref[...] = (h * scale).astype(h8_ref.dtype)
        sh_ref[...] = jnp.broadcast_to(inv, sh_ref.shape)

    nt = (((1,), (1,)), ((), ()))
    bm = h8_ref.shape[0]
    rows = [slice(r * bm // PEER_ROW_GROUPS, (r + 1) * bm // PEER_ROW_GROUPS) for r in range(PEER_ROW_GROUPS)]
    accs = [lax.dot_general(h8_ref[rs, :], u_ref[0], nt, preferred_element_type=jnp.float32) for rs in rows]
    ws = []
    for rs, acc in zip(rows, accs):
        act = _gelu_tanh(acc * sh_ref[rs, 0:1] * su_ref[0])
        gate = jnp.concatenate([g_ref[r, rs, :] for r in range(g_ref.shape[0])], axis=1)
        w = act * gate.astype(jnp.float32) * sv_ref[0]
        scale, inv = _row_scale(w)
        ws.append(((w * scale).astype(FP8), inv))
    contribs = [jnp.dot(w8, v_ref[0], preferred_element_type=jnp.float32) * inv for w8, inv in ws]

    @pl.when(j == 0)
    def _():
        for rs, contrib in zip(rows, contribs):
            o_ref[rs, :] = contrib

    @pl.when(j != 0)
    def _():
        for rs, contrib in zip(rows, contribs):
            o_ref[rs, :] += contrib


def _peer_dense(h, u, v, gmat, layer):
    t, d = h.shape
    (u8, su), (v8, sv) = u, v
    e = v8.shape[1]
    bm = _row_tile(t)
    be = 1024
    tab = pl.BlockSpec((1, be, d), lambda i, j: (layer, j, 0))
    sc = pl.BlockSpec((1, 1, be), lambda i, j: (layer, 0, j))
    return pl.pallas_call(
        _peer_dense_kernel,
        grid=(t // bm, e // be),
        in_specs=[pl.BlockSpec((bm, d), lambda i, j: (i, 0)), tab, sc, tab, sc,
                  pl.BlockSpec((be // PEER_NKEYS, bm, PEER_NKEYS), lambda i, j: (j, i, 0))],
        out_specs=pl.BlockSpec((bm, d), lambda i, j: (i, 0)),
        out_shape=jax.ShapeDtypeStruct((t, d), jnp.float32),
        scratch_shapes=[pltpu.VMEM((bm, d), FP8), pltpu.VMEM((bm, LANES), jnp.float32)],
        compiler_params=pltpu.CompilerParams(
            dimension_semantics=("parallel", "arbitrary"), vmem_limit_bytes=VMEM_LIMIT_BYTES),
        name="peer_dense",
    )(h, u8, su, v8, sv, gmat)


def _peer(h, wq, keys, u, v, layer):
    hb = h
    i1, i2, g = _route(hb, wq, keys)
    return _peer_dense(hb, u, v, _gate_matrix(i1, i2, g), layer)


ROPE_HALF = 32
NA_QROWS = 8
NA_KROWS = 16
SWA_QB = 256
SWA_KB = SWA_QB + 2 * WINDOW

P_CK, P_CV, P_DK, P_DV, P_CQ, P_DQ = 0, 8, 16, 18, 20, 28
QK_CK, QK_DK, QK_CQ, QK_DQ = 0, 8, 10, 20


def _rope_tables(n):
    t = np.arange(n)
    freqs = (np.float32(ROPE_BASE) ** (-np.arange(ROPE_HALF, dtype=np.float32) / np.float32(ROPE_HALF))).astype(np.float32)
    ang_r = (t // GRID_W).astype(np.float32)[:, None] * freqs
    ang_c = (t % GRID_W).astype(np.float32)[:, None] * freqs
    cos = np.concatenate([np.cos(ang_r), np.cos(ang_r), np.cos(ang_c), np.cos(ang_c)], axis=-1)
    sin = np.concatenate([-np.sin(ang_r), np.sin(ang_r), -np.sin(ang_c), np.sin(ang_c)], axis=-1)
    return cos.astype(np.float32), sin.astype(np.float32)


def _qk_prep_kernel(tbl_ref, p_ref, g_ref, cos_ref, sin_ref, o_ref):
    del tbl_ref
    x = p_ref[...].astype(jnp.float32)
    y = x * lax.rsqrt(jnp.mean(x * x, axis=-1, keepdims=True) + EPS) * g_ref[0]
    lane = lax.broadcasted_iota(jnp.int32, y.shape, 1)
    first = jnp.bitwise_and(lane, 2 * ROPE_HALF - 1) < ROPE_HALF
    partner = jnp.where(first, pltpu.roll(y, LANES - ROPE_HALF, 1), pltpu.roll(y, ROPE_HALF, 1))
    o_ref[...] = (y * cos_ref[0] + partner * sin_ref[0]).astype(o_ref.dtype)


def _qk_prep(p, table, gains, cos, sin):
    n = p.shape[0]
    ncol = table.shape[1]
    bt = 2048 if n % 2048 == 0 else n
    grid_spec = pltpu.PrefetchScalarGridSpec(
        num_scalar_prefetch=1,
        grid=(n // bt, ncol),
        in_specs=[pl.BlockSpec((bt, HEAD_DIM), lambda i, j, tbl: (i, tbl[0, j])),
                  pl.BlockSpec((1, 1, HEAD_DIM), lambda i, j, tbl: (tbl[1, j], 0, 0)),
                  pl.BlockSpec((1, bt, HEAD_DIM), lambda i, j, tbl: (tbl[2, j], i, 0)),
                  pl.BlockSpec((1, bt, HEAD_DIM), lambda i, j, tbl: (tbl[2, j], i, 0))],
        out_specs=pl.BlockSpec((bt, HEAD_DIM), lambda i, j, tbl: (i, j)),
    )
    return pl.pallas_call(
        _qk_prep_kernel,
        grid_spec=grid_spec,
        out_shape=jax.ShapeDtypeStruct((n, ncol * HEAD_DIM), jnp.bfloat16),
        compiler_params=pltpu.CompilerParams(
            dimension_semantics=("parallel", "arbitrary"), vmem_limit_bytes=VMEM_LIMIT_BYTES),
        name="qk_prep",
    )(table, p, gains, cos, sin)


def _na_block_types(n):
    rows = n // GRID_W
    return ((NA_QROWS, NA_QROWS - 4), (0, 0), (rows - NA_QROWS, rows - NA_KROWS))


def _na_row_ok(n, r0, rlo, qi, kj):
    rows = n // GRID_W
    rs = min(max(r0 + qi - WIN_H // 2, 0), rows - WIN_H)
    return rs <= rlo + kj < rs + WIN_H


def _na_pair_tables(rpb):
    nc = 2 * WIN_W - 1
    c = np.arange(GRID_W)[:, None]
    kc = np.arange(GRID_W)[None, :]
    cs = np.clip(c - WIN_W // 2, 0, GRID_W - WIN_W)
    col_ok = (kc >= cs) & (kc < cs + WIN_W)
    oh_col = (np.clip(kc - c + WIN_W - 1, 0, nc - 1)[None] == np.arange(nc)[:, None, None]).astype(np.float32)
    colb = jnp.einsum('hab,bcd->hacd', rpb.astype(jnp.float32), jnp.asarray(oh_col), precision=lax.Precision.HIGHEST)
    colb = jnp.where(jnp.asarray(col_ok)[None, None], colb, NEG_INF)
    none = jnp.full_like(colb[:, :1], NEG_INF)
    first = jnp.concatenate([none, colb], axis=1)
    second = jnp.concatenate([colb, none], axis=1)
    gone = jnp.full_like(first, NEG_INF)
    pair = lambda lo, hi: jnp.concatenate([lo, hi], axis=-1)
    return jnp.stack([pair(first, second), pair(first, gone), pair(gone, second)], axis=1)


def _softmax_pv(parts, extra_logit=None):
    m = parts[0][0].max(axis=-1, keepdims=True)
    for s, _ in parts[1:]:
        m = jnp.maximum(m, s.max(axis=-1, keepdims=True))
    if extra_logit is not None:
        m = jnp.maximum(m, extra_logit)
    den = jnp.exp(extra_logit - m) if extra_logit is not None else 0.0
    acc = None
    for s, v in parts:
        p = jnp.exp(s - m)
        den = den + p.sum(axis=-1, keepdims=True)
        pv = jnp.dot(p.astype(jnp.bfloat16), v, preferred_element_type=jnp.float32)
        acc = pv if acc is None else acc + pv
    return acc / den


def _na_kernel(q_ref, k_ref, v_ref, kc_ref, vc_ref, pt_ref, o_ref, bias_ref):
    n = q_ref.shape[0]
    qb = NA_QROWS * GRID_W
    kb = NA_KROWS * GRID_W
    nblk = n // qb
    nt = (((1,), (1,)), ((), ()))
    kc = kc_ref[...]
    vc = vc_ref[...]
    for kind, (r0, rlo) in enumerate(_na_block_types(n)):
        for qi in range(NA_QROWS):
            for kp in range(NA_KROWS // 2):
                ok0 = _na_row_ok(n, r0, rlo, qi, 2 * kp)
                ok1 = _na_row_ok(n, r0, rlo, qi, 2 * kp + 1)
                e = rlo + 2 * kp - (r0 + qi) + WIN_H - 1
                rows, cols = slice(qi * GRID_W, (qi + 1) * GRID_W), slice(kp * LANES, (kp + 1) * LANES)
                if ok0 or ok1:
                    bias_ref[kind, rows, cols] = pt_ref[0, 0 if (ok0 and ok1) else (1 if ok0 else 2), e + 1]
                else:
                    bias_ref[kind, rows, cols] = jnp.full((GRID_W, LANES), NEG_INF, jnp.float32)

    def body(b, carry):
        q0 = pl.multiple_of(b * qb, qb)
        ks = pl.multiple_of(jnp.clip(b * qb - (NA_KROWS - NA_QROWS) // 2 * GRID_W, 0, n - kb), 2 * LANES)
        kind = jnp.where(b == 0, 1, jnp.where(b == nblk - 1, 2, 0))
        q = q_ref[pl.ds(q0, qb), :]
        s_w = lax.dot_general(q, k_ref[pl.ds(ks, kb), :], nt, preferred_element_type=jnp.float32) + bias_ref[kind]
        s_c = lax.dot_general(q, kc, nt, preferred_element_type=jnp.float32)
        o = _softmax_pv([(s_w, v_ref[pl.ds(ks, kb), :]), (s_c, vc)])
        o_ref[pl.ds(q0, qb), :] = o.astype(o_ref.dtype)
        return carry

    lax.fori_loop(0, nblk, body, 0, unroll=2)


def _neighbourhood_attn(qk, p, qk_c, p_c, pair_tables):
    n = qk.shape[0]
    m = qk_c.shape[0]

    def full(col0):
        return pl.BlockSpec((n, HEAD_DIM), lambda h: (0, col0 + h))

    def ctx(col0):
        return pl.BlockSpec((m, HEAD_DIM), lambda h: (0, col0 + h))

    return pl.pallas_call(
        _na_kernel,
        grid=(C_HEADS,),
        in_specs=[full(QK_CQ), full(QK_CK), full(P_CV), ctx(QK_CK), ctx(P_CV),
                  pl.BlockSpec((1, 3, 2 * WIN_H, GRID_W, LANES), lambda h: (h, 0, 0, 0, 0))],
        out_specs=pl.BlockSpec((n, HEAD_DIM), lambda h: (0, h)),
        out_shape=jax.ShapeDtypeStruct((n, C_WIDTH), jnp.bfloat16),
        scratch_shapes=[pltpu.VMEM((3, NA_QROWS * GRID_W, NA_KROWS * GRID_W), jnp.float32)],
        compiler_params=pltpu.CompilerParams(
            dimension_semantics=("parallel",), vmem_limit_bytes=VMEM_LIMIT_BYTES),
        name="neighbourhood_attn",
    )(qk, qk, p, qk_c, p_c, pair_tables)


def _swa_kernel(q_ref, k_ref, v_ref, kc_ref, vc_ref, sink_ref, o_ref):
    n = q_ref.shape[0]
    nblk = n // SWA_QB
    nt = (((1,), (1,)), ((), ()))
    kc = kc_ref[...]
    vc = vc_ref[...]
    kv = pl.program_id(0)
    sink = jnp.concatenate(
        [jnp.broadcast_to(sink_ref[pl.ds(kv * D_GROUP + g, 1), 0:1], (SWA_QB, 1)) for g in range(D_GROUP)], axis=0)
    q_off = jnp.bitwise_and(lax.broadcasted_iota(jnp.int32, (D_GROUP * SWA_QB, SWA_KB), 0), SWA_QB - 1)
    k_off = lax.broadcasted_iota(jnp.int32, (D_GROUP * SWA_QB, SWA_KB), 1)
    rel = q_off - k_off

    def body(b, carry):
        q0 = pl.multiple_of(b * SWA_QB, SWA_QB)
        ks = pl.multiple_of(jnp.clip(q0 - WINDOW, 0, n - SWA_KB), LANES)
        q = jnp.concatenate([q_ref[pl.ds(q0, SWA_QB), g * HEAD_DIM:(g + 1) * HEAD_DIM] for g in range(D_GROUP)], axis=0)
        s_w = lax.dot_general(q, k_ref[pl.ds(ks, SWA_KB), :], nt, preferred_element_type=jnp.float32)
        s_w = jnp.where(jnp.abs(rel + (q0 - ks)) <= WINDOW, s_w, NEG_INF)
        s_c = lax.dot_general(q, kc, nt, preferred_element_type=jnp.float32)
        o = _softmax_pv([(s_w, v_ref[pl.ds(ks, SWA_KB), :]), (s_c, vc)], extra_logit=sink)
        for g in range(D_GROUP):
            o_ref[pl.ds(q0, SWA_QB), g * HEAD_DIM:(g + 1) * HEAD_DIM] = o[g * SWA_QB:(g + 1) * SWA_QB].astype(o_ref.dtype)
        return carry

    lax.fori_loop(0, nblk, body, 0)


def _window_attn(qk, p, qk_c, p_c, sink_rows):
    n = qk.shape[0]
    m = qk_c.shape[0]
    gw = D_GROUP * HEAD_DIM
    return pl.pallas_call(
        _swa_kernel,
        grid=(D_KV_HEADS,),
        in_specs=[pl.BlockSpec((n, gw), lambda kv: (0, QK_DQ // D_GROUP + kv)),
                  pl.BlockSpec((n, HEAD_DIM), lambda kv: (0, QK_DK + kv)),
                  pl.BlockSpec((n, HEAD_DIM), lambda kv: (0, P_DV + kv)),
                  pl.BlockSpec((m, HEAD_DIM), lambda kv: (0, QK_DK + kv)),
                  pl.BlockSpec((m, HEAD_DIM), lambda kv: (0, P_DV + kv)),
                  pl.BlockSpec((D_HEADS, LANES), lambda kv: (0, 0))],
        out_specs=pl.BlockSpec((n, gw), lambda kv: (0, kv)),
        out_shape=jax.ShapeDtypeStruct((n, D_WIDTH), jnp.bfloat16),
        compiler_params=pltpu.CompilerParams(
            dimension_semantics=("parallel",), vmem_limit_bytes=VMEM_LIMIT_BYTES),
        name="window_attn",
    )(qk, qk, p, qk_c, p_c, sink_rows)


def _mixer_cd(h_c, h_l, w_in, qk_g, rpb, sink, w_out):
    n = h_l.shape[0]
    m = h_c.shape[0]
    scale = HEAD_DIM ** -0.5
    p_l = _matmul(h_l, w_in, jnp.bfloat16)
    p_c = _matmul(h_c, w_in[:, :CD_KV_COLS], jnp.bfloat16)
    gains = jnp.stack([qk_g[0] * scale, qk_g[1], qk_g[2] * scale, qk_g[3]]).astype(jnp.float32)[:, None, :]
    cos, sin = _rope_tables(n)
    cs_l = (np.stack([np.ones_like(cos), cos]), np.stack([np.zeros_like(sin), sin]))
    cs_c = (np.ones((1, m, HEAD_DIM), np.float32), np.zeros((1, m, HEAD_DIM), np.float32))
    src = list(range(P_CK, P_CK + 8)) + [P_DK, P_DK + 1] + list(range(P_CQ, P_CQ + 8)) + [P_CQ, P_CQ] + list(range(P_DQ, P_DQ + 8))
    gain = [1] * 8 + [3] * 2 + [0] * 8 + [0, 0] + [2] * 8
    rope = [0] * 8 + [1] * 2 + [0] * 8 + [0, 0] + [1] * 8
    tbl_l = jnp.asarray(np.array([src, gain, rope], np.int32))
    tbl_c = jnp.asarray(np.array([src[:10], gain[:10], [0] * 10], np.int32))
    qk_l = _qk_prep(p_l, tbl_l, gains, jnp.asarray(cs_l[0]), jnp.asarray(cs_l[1]))
    qk_c = _qk_prep(p_c, tbl_c, gains, jnp.asarray(cs_c[0]), jnp.asarray(cs_c[1]))
    c_out = _neighbourhood_attn(qk_l, p_l, qk_c, p_c, _na_pair_tables(rpb))
    sink_rows = jnp.broadcast_to(sink.astype(jnp.float32)[:, None], (D_HEADS, LANES))
    d_out = _window_attn(qk_l, p_l, qk_c, p_c, sink_rows)
    return _matmul(jnp.concatenate([c_out, d_out], axis=-1), w_out)


SCAN_L = 128
GLA_SUB = 4
N_SCAN_STATES = 2 * A_HEADS
GATE_LR_LANE = 4 * A_HEADS


def _log_sigmoid(x):
    return jnp.minimum(x, 0.0) - jnp.log(1.0 + jnp.exp(-jnp.abs(x)))


def _tri_mask(rev):
    r = lax.broadcasted_iota(jnp.int32, (SCAN_L, SCAN_L), 0)
    c = lax.broadcasted_iota(jnp.int32, (SCAN_L, SCAN_L), 1)
    return (c >= r) if rev else (c <= r)


def _cumsum_rows(tri, x):
    hi = x.astype(jnp.bfloat16)
    r1 = x - hi.astype(jnp.float32)
    mid = r1.astype(jnp.bfloat16)
    lo = (r1 - mid.astype(jnp.float32)).astype(jnp.bfloat16)
    dot = lambda p: jnp.dot(tri, p, preferred_element_type=jnp.float32)
    return dot(hi) + dot(mid) + dot(lo)


def _conv_kernel(x_ref, w_ref, b_ref, o_ref, *, seg, scale):
    x = x_ref[...].astype(jnp.float32)
    n = x.shape[0]
    t = lax.broadcasted_iota(jnp.int32, x.shape, 0)
    first = (t == 0) | (t == seg)
    last = (t == seg - 1) | (t == n - 1)
    prev = jnp.where(first, 0.0, pltpu.roll(x, 1, 0))
    nxt = jnp.where(last, 0.0, pltpu.roll(x, n - 1, 0))
    y = w_ref[0:1, :] * prev + w_ref[1:2, :] * x + w_ref[2:3, :] * nxt + b_ref[...]
    y = y * jax.nn.sigmoid(y)
    j = pl.program_id(0)
    o_ref[...] = (y * jnp.where(j >= A_HEADS, scale, 1.0)).astype(o_ref.dtype)


def _short_conv_silu(p, conv_w, conv_b, seg):
    t = p.shape[0]
    nblk = 2 * A_WIDTH // LANES
    return pl.pallas_call(
        functools.partial(_conv_kernel, seg=seg, scale=HEAD_DIM ** -0.5),
        grid=(nblk,),
        in_specs=[pl.BlockSpec((t, LANES), lambda j: (0, j)),
                  pl.BlockSpec((8, LANES), lambda j: (0, j)),
                  pl.BlockSpec((1, LANES), lambda j: (0, j))],
        out_specs=pl.BlockSpec((t, LANES), lambda j: (0, j)),
        out_shape=jax.ShapeDtypeStruct((t, 2 * A_WIDTH), jnp.bfloat16),
        compiler_params=pltpu.CompilerParams(
            dimension_semantics=("parallel",), vmem_limit_bytes=VMEM_LIMIT_BYTES),
        name="short_conv_silu",
    )(p, jnp.pad(conv_w, ((0, 8 - CONV_W), (0, 0))), conv_b[None, :])


def _scan_chunk_index(j, n_ctx_chunks, n_chunks, rev):
    if not rev:
        return j
    return jnp.where(j < n_ctx_chunks, n_ctx_chunks - 1 - j, n_chunks + n_ctx_chunks - 1 - j)


def _mlstm_kernel(qf_ref, kf_ref, vf_ref, gf_ref, qb_ref, kb_ref, vb_ref, gb_ref, gbias_ref,
                  yf_ref, yb_ref, c_ref, m_ref):
    @pl.when(pl.program_id(0) == 0)
    def _():
        c_ref[...] = jnp.zeros_like(c_ref)
        m_ref[...] = jnp.zeros_like(m_ref)

    lane = lax.broadcasted_iota(jnp.int32, (SCAN_L, LANES), 1)
    is_forget = jnp.bitwise_and(lane, 2 * A_HEADS - 1) >= A_HEADS
    ones = jnp.ones((SCAN_L, HEAD_DIM), jnp.bfloat16)
    nt = (((1,), (1,)), ((), ()))
    tn = (((0,), (0,)), ((), ()))
    for z, (q_ref, k_ref, v_ref, g_ref, y_ref) in enumerate(
            ((qf_ref, kf_ref, vf_ref, gf_ref, yf_ref), (qb_ref, kb_ref, vb_ref, gb_ref, yb_ref))):
        rev = z == 1
        mask = _tri_mask(rev)
        gates = g_ref[...] + gbias_ref[...]
        gates = jnp.where(is_forget, _log_sigmoid(gates), gates)
        csum = _cumsum_rows(mask.astype(jnp.bfloat16), jnp.where(is_forget, gates, 0.0))
        gates_t = gates.T
        csum_t = csum.T
        end = 0 if rev else SCAN_L - 1
        heads = []
        for h in range(A_HEADS):
            li, lf = z * 2 * A_HEADS + h, z * 2 * A_HEADS + A_HEADS + h
            sl = slice(h * HEAD_DIM, (h + 1) * HEAD_DIM)
            q, k, v = q_ref[:, sl], k_ref[:, sl], v_ref[:, sl]
            b_col, b_row = csum[:, lf:lf + 1], csum_t[lf:lf + 1, :]
            i_col, i_row = gates[:, li:li + 1], gates_t[li:li + 1, :]
            st = z * A_HEADS + h
            m_prev = m_ref[st][0:1, 0:1]
            dlog = jnp.where(mask, b_col - b_row + i_row, NEG_INF)
            m_inter = b_col + m_prev
            m_t = jnp.maximum(m_inter, jnp.max(dlog, axis=-1, keepdims=True))
            s = lax.dot_general(q, k, nt, preferred_element_type=jnp.float32)
            qc = jnp.dot(q, c_ref[st].astype(jnp.bfloat16), preferred_element_type=jnp.float32)
            heads.append(dict(sl=sl, st=st, k=k, v_ext=jnp.concatenate([v, ones], axis=1), b_col=b_col, i_col=i_col,
                              m_prev=m_prev, dlog=dlog, m_inter=m_inter, m_t=m_t, s=s, qc=qc))
        for hd in heads:
            sc = (hd["s"] * jnp.exp(hd["dlog"] - hd["m_t"])).astype(jnp.bfloat16)
            w_inter = jnp.exp(hd["m_inter"] - hd["m_t"])
            r = jnp.dot(sc, hd["v_ext"], preferred_element_type=jnp.float32) + w_inter * hd["qc"]
            num, den = r[:, :HEAD_DIM], r[:, HEAD_DIM:]
            y_ref[:, hd["sl"]] = num / jnp.maximum(jnp.abs(den), jnp.exp(-hd["m_t"]))
        for hd in heads:
            st = hd["st"]
            m_new = hd["m_t"][end:end + 1, :]
            b_end = hd["b_col"][end:end + 1, :]
            w_end = jnp.exp(b_end - hd["b_col"] + hd["i_col"] - m_new)
            decay = jnp.exp(b_end + hd["m_prev"] - m_new)
            kw = (hd["k"].astype(jnp.float32) * w_end).astype(jnp.bfloat16)
            c_ref[st] = decay * c_ref[st] + lax.dot_general(kw, hd["v_ext"], tn, preferred_element_type=jnp.float32)
            m_ref[st] = jnp.broadcast_to(m_new, (8, LANES))


def _scan_specs(n_ctx_chunks, n_chunks, width, col0, rev):
    return pl.BlockSpec((SCAN_L, width),
                        lambda j: (_scan_chunk_index(j, n_ctx_chunks, n_chunks, rev), col0))


def _mlstm(qk, p, gates, gate_bias, seg):
    t = qk.shape[0]
    nck = t // SCAN_L
    ncc = seg // SCAN_L
    ins, specs = [], []
    for rev in (False, True):
        ins += [qk, qk, p, gates]
        specs += [_scan_specs(ncc, nck, A_WIDTH, 0, rev), _scan_specs(ncc, nck, A_WIDTH, 1, rev),
                  _scan_specs(ncc, nck, A_WIDTH, 2, rev), _scan_specs(ncc, nck, LANES, 0, rev)]
    return pl.pallas_call(
        _mlstm_kernel,
        grid=(nck,),
        in_specs=specs + [pl.BlockSpec((1, LANES), lambda j: (0, 0))],
        out_specs=[_scan_specs(ncc, nck, A_WIDTH, 0, False), _scan_specs(ncc, nck, A_WIDTH, 0, True)],
        out_shape=[jax.ShapeDtypeStruct((t, A_WIDTH), jnp.float32)] * 2,
        scratch_shapes=[pltpu.VMEM((N_SCAN_STATES, HEAD_DIM, 2 * HEAD_DIM), jnp.float32),
                        pltpu.VMEM((N_SCAN_STATES, 8, LANES), jnp.float32)],
        compiler_params=pltpu.CompilerParams(
            dimension_semantics=("arbitrary",), vmem_limit_bytes=VMEM_LIMIT_BYTES),
        name="mlstm_scan",
    )(*ins, gate_bias)


def _gla_kernel(qf_ref, kf_ref, vf_ref, gf_ref, qb_ref, kb_ref, vb_ref, gb_ref, w2_ref, ab_ref,
                of_ref, ob_ref, s_ref):
    @pl.when(pl.program_id(0) == 0)
    def _():
        s_ref[...] = jnp.zeros_like(s_ref)

    bf = jnp.bfloat16
    row = lax.broadcasted_iota(jnp.int32, (SCAN_L, LANES), 0)
    lane = lax.broadcasted_iota(jnp.int32, (SCAN_L, LANES), 1)
    head_lane = (lane < B_KEY_DIM, lane >= B_KEY_DIM)
    r_sq = lax.broadcasted_iota(jnp.int32, (SCAN_L, SCAN_L), 0)
    c_sq = lax.broadcasted_iota(jnp.int32, (SCAN_L, SCAN_L), 1)
    in_sub = jnp.bitwise_and(row, GLA_SUB - 1)
    sel_r = lax.broadcasted_iota(jnp.int32, (LANES, 2 * LANES), 0)
    sel_c = lax.broadcasted_iota(jnp.int32, (LANES, 2 * LANES), 1)
    head_sum = ((sel_r < B_KEY_DIM) == (sel_c < LANES)).astype(bf)
    nt = (((1,), (1,)), ((), ()))
    tn = (((0,), (0,)), ((), ()))
    for z, (q_ref, k_ref, v_ref, g_ref, o_ref) in enumerate(
            ((qf_ref, kf_ref, vf_ref, gf_ref, of_ref), (qb_ref, kb_ref, vb_ref, gb_ref, ob_ref))):
        rev = z == 1
        tri = _tri_mask(rev).astype(bf)
        lr = g_ref[...].astype(bf)
        end = 0 if rev else SCAN_L - 1
        levels = []
        m = SCAN_L // 2
        while m >= GLA_SUB:
            shift = (2 * m).bit_length() - 1
            upper = jnp.bitwise_and(row, 2 * m - 1) >= m
            mid = jnp.left_shift(jnp.right_shift(r_sq, shift), shift) + (m if rev else m - 1)
            levels.append(dict(
                keys=upper if rev else ~upper,
                queries=~upper if rev else upper,
                pick_mid=(c_sq == mid).astype(bf),
                same=jnp.right_shift(r_sq, shift) == jnp.right_shift(c_sq, shift)))
            m //= 2
        pick_all = jnp.concatenate([lv["pick_mid"] for lv in levels], axis=0)
        for pr in range(B_HEADS // 2):
            ls = slice(pr * LANES, (pr + 1) * LANES)
            za = jnp.dot(lr, w2_ref[z, :, ls], preferred_element_type=jnp.float32) + ab_ref[z:z + 1, ls]
            b = _cumsum_rows(tri, _log_sigmoid(za) * (1.0 / GLA_TAU))
            q = q_ref[:, ls].astype(jnp.float32) * (B_KEY_DIM ** -0.5)
            k = k_ref[:, ls].astype(jnp.float32)
            v = v_ref[:, 2 * pr * HEAD_DIM:(2 * pr + 2) * HEAD_DIM]
            b_hi = b.astype(bf)
            b_lo = (b - b_hi.astype(jnp.float32)).astype(bf)
            picked = jnp.dot(pick_all, jnp.concatenate([b_hi, b_lo], axis=1), preferred_element_type=jnp.float32)
            q_ms, k_ms = [], []
            for li, lv in enumerate(levels):
                b_mid = picked[li * SCAN_L:(li + 1) * SCAN_L, :LANES] + picked[li * SCAN_L:(li + 1) * SCAN_L, LANES:]
                q_ms.append((q * jnp.exp(jnp.where(lv["queries"], b - b_mid, NEG_INF))).astype(bf))
                k_m = k * jnp.exp(jnp.where(lv["keys"], b_mid - b, NEG_INF))
                k_ms.append([jnp.where(head_lane[hh], k_m, 0.0).astype(bf) for hh in range(2)])
            parts = [[lax.dot_general(q_ms[li], k_ms[li][hh], nt, preferred_element_type=jnp.float32)
                      for hh in range(2)] for li in range(len(levels))]
            fs = []
            for d in range(GLA_SUB):
                shift = (SCAN_L - d) % SCAN_L if rev else d
                k_s = pltpu.roll(k, shift, 0) if d else k
                b_s = pltpu.roll(b, shift, 0) if d else b
                ok_row = (in_sub <= GLA_SUB - 1 - d) if rev else (in_sub >= d)
                fs.append((q * k_s * jnp.exp(jnp.where(ok_row, b - b_s, NEG_INF))).astype(bf))
            red = jnp.dot(jnp.concatenate(fs, axis=0), head_sum, preferred_element_type=jnp.float32)
            attn = []
            for hh in range(2):
                acc = jnp.zeros((SCAN_L, SCAN_L), jnp.float32)
                for li, lv in enumerate(levels):
                    acc = acc + jnp.where(lv["same"], parts[li][hh], 0.0)
                for d in range(GLA_SUB):
                    on_diag = (c_sq == r_sq + d) if rev else (c_sq == r_sq - d)
                    acc = acc + jnp.where(on_diag, red[d * SCAN_L:(d + 1) * SCAN_L, hh * LANES:(hh + 1) * LANES], 0.0)
                attn.append(acc)
            q_in = q * jnp.exp(b)
            b_end = b[end:end + 1, :]
            k_out = (k * jnp.exp(b_end - b)).astype(bf)
            st = z * (B_HEADS // 2) + pr
            s_t = s_ref[st]
            s_bf = s_t.astype(bf)
            for hh in range(2):
                vs = slice(hh * HEAD_DIM, (hh + 1) * HEAD_DIM)
                qh = jnp.where(head_lane[hh], q_in, 0.0).astype(bf)
                o = (jnp.dot(attn[hh].astype(bf), v[:, vs], preferred_element_type=jnp.float32)
                     + lax.dot_general(qh, s_bf[vs, :], nt, preferred_element_type=jnp.float32))
                o_ref[:, (2 * pr + hh) * HEAD_DIM:(2 * pr + hh + 1) * HEAD_DIM] = o
            s_ref[st] = s_t * jnp.exp(b_end) + lax.dot_general(v, k_out, tn, preferred_element_type=jnp.float32)


def _gla(p, gates, w2, alpha_b, seg):
    t = p.shape[0]
    nck = t // SCAN_L
    ncc = seg // SCAN_L
    ins, specs = [], []
    for rev in (False, True):
        ins += [p, p, p, gates]
        specs += [_scan_specs(ncc, nck, B_KEY_WIDTH, P_BQ // B_KEY_WIDTH, rev),
                  _scan_specs(ncc, nck, B_KEY_WIDTH, P_BK // B_KEY_WIDTH, rev),
                  _scan_specs(ncc, nck, B_VAL_WIDTH, P_BV // B_VAL_WIDTH, rev),
                  _scan_specs(ncc, nck, LANES, 0, rev)]
    return pl.pallas_call(
        _gla_kernel,
        grid=(nck,),
        in_specs=specs + [pl.BlockSpec((2, LANES, B_KEY_WIDTH), lambda j: (0, 0, 0)),
                          pl.BlockSpec((2, B_KEY_WIDTH), lambda j: (0, 0))],
        out_specs=[_scan_specs(ncc, nck, B_VAL_WIDTH, 0, False), _scan_specs(ncc, nck, B_VAL_WIDTH, 0, True)],
        out_shape=[jax.ShapeDtypeStruct((t, B_VAL_WIDTH), jnp.float32)] * 2,
        scratch_shapes=[pltpu.VMEM((B_HEADS, 2 * HEAD_DIM, LANES), jnp.float32)],
        compiler_params=pltpu.CompilerParams(
            dimension_semantics=("arbitrary",), vmem_limit_bytes=VMEM_LIMIT_BYTES),
        name="gla_scan",
    )(*ins, w2, alpha_b)


HEAD_OUT_GROUP = 4


def _head_out_kernel(af_ref, ab_ref, bf_ref, bb_ref, gate_ref, g_ref, o_ref):
    is_gla = pl.program_id(1) >= A_HEADS // HEAD_OUT_GROUP
    for hh in range(HEAD_OUT_GROUP):
        sl = slice(hh * HEAD_DIM, (hh + 1) * HEAD_DIM)
        y = jnp.where(is_gla, bf_ref[:, sl] + bb_ref[:, sl], af_ref[:, sl] + ab_ref[:, sl])
        yn = y * lax.rsqrt(jnp.mean(y * y, axis=-1, keepdims=True) + EPS) * g_ref[0]
        gate = gate_ref[:, sl].astype(jnp.float32)
        o_ref[:, sl] = (yn * jax.nn.sigmoid(gate) * jnp.where(is_gla, gate, 1.0)).astype(o_ref.dtype)


def _head_out(ya, yb, p, head_g):
    t = p.shape[0]
    bt = _row_tile(t)
    w = HEAD_OUT_GROUP * HEAD_DIM
    na, nb = A_HEADS // HEAD_OUT_GROUP, B_HEADS // HEAD_OUT_GROUP
    a_spec = pl.BlockSpec((bt, w), lambda i, j: (i, jnp.minimum(j, na - 1)))
    b_spec = pl.BlockSpec((bt, w), lambda i, j: (i, jnp.maximum(j - na, 0)))

    def gate_col(i, j):
        return (i, jnp.where(j < na, P_AO // HEAD_OUT_GROUP + j, P_BR // HEAD_OUT_GROUP + j - na))

    return pl.pallas_call(
        _head_out_kernel,
        grid=(t // bt, na + nb),
        in_specs=[a_spec, a_spec, b_spec, b_spec,
                  pl.BlockSpec((bt, w), gate_col),
                  pl.BlockSpec((1, 1, HEAD_DIM), lambda i, j: (j // na, 0, 0))],
        out_specs=pl.BlockSpec((bt, w), lambda i, j: (i, j)),
        out_shape=jax.ShapeDtypeStruct((t, (A_HEADS + B_HEADS) * HEAD_DIM), jnp.bfloat16),
        compiler_params=pltpu.CompilerParams(
            dimension_semantics=("parallel", "arbitrary"), vmem_limit_bytes=VMEM_LIMIT_BYTES),
        name="head_out",
    )(ya[0], ya[1], yb[0], yb[1], p, head_g.astype(jnp.float32)[:, None, :])


P_AO = 24
P_BQ, P_BK, P_BV = 4096, 4608, 5120
P_BR = 48


def _mixer_ab(h, w_in, conv_w, conv_b, gate_b, alpha_w2, alpha_b, head_g, w_out, seg):
    bf = jnp.bfloat16
    cols = np.cumsum((0,) + AB_SPLITS)
    pick = lambda *ids: jnp.concatenate([w_in[:, cols[i]:cols[i + 1]] for i in ids], axis=1)
    w_main = pick(0, 1, 2, 3, 5, 6, 7, 8).astype(bf)
    w_gate = jnp.pad(pick(4, 9), ((0, 0), (0, LANES - 4 * A_HEADS - 2 * GLA_RANK))).astype(bf)
    hb = h
    p = _matmul(hb, w_main, bf)
    gates = _matmul(hb, w_gate, jnp.float32)
    gate_bias = jnp.pad(gate_b.reshape(1, -1).astype(jnp.float32), ((0, 0), (0, LANES - 4 * A_HEADS)))
    w2 = jnp.zeros((2, LANES, B_KEY_WIDTH), jnp.float32)
    for z in range(2):
        w2 = w2.at[z, GATE_LR_LANE + z * GLA_RANK:GATE_LR_LANE + (z + 1) * GLA_RANK].set(alpha_w2[z])
    qk = _short_conv_silu(p, conv_w, conv_b, seg)
    ya = _mlstm(qk, p, gates, gate_bias, seg)
    yb = _gla(p, gates, w2.astype(bf), alpha_b.astype(jnp.float32), seg)
    return _matmul(_head_out(ya, yb, p, head_g), w_out)


MOD_ROWS = 256


def _res_mod_kernel(x_ref, y_ref, p_ref, xo_ref, h_ref):
    p = p_ref[0]
    x = x_ref[...] + p[0:1, :] * y_ref[...]
    xo_ref[...] = x
    r = x * lax.rsqrt(jnp.mean(x * x, axis=-1, keepdims=True) + EPS) * p[1:2, :]
    h_ref[...] = (r * (1.0 + p[3:4, :]) + p[2:3, :]).astype(h_ref.dtype)


def _mod_kernel(x_ref, p_ref, h_ref):
    p = p_ref[0]
    x = x_ref[...]
    r = x * lax.rsqrt(jnp.mean(x * x, axis=-1, keepdims=True) + EPS) * p[1:2, :]
    h_ref[...] = (r * (1.0 + p[3:4, :]) + p[2:3, :]).astype(h_ref.dtype)


def _mod_params(gate_c, gate_l, gain, shift_c, shift_l, scale_c, scale_l):
    rows = lambda g, sh, sc: jnp.concatenate(
        [g.reshape(1, -1), gain.reshape(1, -1), sh.reshape(1, -1), sc.reshape(1, -1),
         jnp.zeros((4, gain.shape[-1]), jnp.float32)], axis=0)
    return jnp.stack([rows(gate_c, shift_c, scale_c), rows(gate_l, shift_l, scale_l)]).astype(jnp.float32)


def _res_mod(x, y, params, seg, row0=0, n_rows=None):
    d = x.shape[1]
    n_rows = x.shape[0] - row0 if n_rows is None else n_rows
    bt = MOD_ROWS
    off, seg_blocks = row0 // bt, seg // bt
    row_spec = pl.BlockSpec((bt, d), lambda i: (i + off, 0))
    out_spec = pl.BlockSpec((bt, d), lambda i: (i, 0))
    p_spec = pl.BlockSpec((1, 8, d), lambda i: (jnp.where(i + off < seg_blocks, 0, 1), 0, 0))
    cp = pltpu.CompilerParams(dimension_semantics=("parallel",), vmem_limit_bytes=VMEM_LIMIT_BYTES)
    h_shape = jax.ShapeDtypeStruct((n_rows, d), jnp.bfloat16)
    if y is None:
        return pl.pallas_call(_mod_kernel, grid=(n_rows // bt,), in_specs=[row_spec, p_spec], out_specs=out_spec,
                              out_shape=h_shape, compiler_params=cp, name="modulate")(x, params)
    return pl.pallas_call(
        _res_mod_kernel, grid=(n_rows // bt,), in_specs=[row_spec, row_spec, p_spec],
        out_specs=[out_spec, out_spec],
        out_shape=[jax.ShapeDtypeStruct((n_rows, d), jnp.float32), h_shape],
        compiler_params=cp, name="residual_modulate")(x, y, params)


def _modulation_kernel(c_ref, w_ref, b_ref, o_ref):
    act = c_ref[...]
    act = (act * jax.nn.sigmoid(act)).astype(jnp.bfloat16)
    o_ref[0] = jnp.dot(act, w_ref[0].astype(jnp.bfloat16), preferred_element_type=jnp.float32) + b_ref[0]


def _modulation(c, c_ctx, mod_w, mod_b):
    nl, d, n = mod_w.shape
    cond = jnp.concatenate([c_ctx.reshape(1, d), c.reshape(1, d), jnp.zeros((6, d), jnp.float32)], axis=0)
    bn = n // 8
    return pl.pallas_call(
        _modulation_kernel,
        grid=(nl, n // bn),
        in_specs=[pl.BlockSpec((8, d), lambda l, j: (0, 0)),
                  pl.BlockSpec((1, d, bn), lambda l, j: (l, 0, j)),
                  pl.BlockSpec((1, 1, bn), lambda l, j: (l, 0, j))],
        out_specs=pl.BlockSpec((1, 8, bn), lambda l, j: (l, 0, j)),
        out_shape=jax.ShapeDtypeStruct((nl, 8, n), jnp.float32),
        compiler_params=pltpu.CompilerParams(
            dimension_semantics=("parallel", "parallel"), vmem_limit_bytes=VMEM_LIMIT_BYTES),
        name="modulation",
    )(cond, mod_w, mod_b[:, None, :])


def kernel(x, c, ctx, c_ctx, mod_w, mod_b, norm_g, ab_w_in, ab_conv_w, ab_conv_b, ab_gate_b, ab_alpha_w2,
           ab_alpha_b, ab_head_g, ab_w_out, cd_w_in, cd_qk_g, cd_rpb, cd_sink, cd_w_out, peer_w_q,
           peer_sub_keys, peer_u, peer_v):
    assert DEPTH == 2
    bf = jnp.bfloat16
    m = ctx.shape[1]
    xs = jnp.concatenate([ctx[0], x[0]], axis=0)
    ub = _quant_rows(peer_u)
    vb = _quant_rows(peer_v)
    mod_all = _modulation(c, c_ctx, mod_w, mod_b)
    mods = [(jnp.split(mod_all[l, 0], 6), jnp.split(mod_all[l, 1], 6)) for l in range(DEPTH)]
    one = jnp.ones_like(mods[0][0][0])

    def params(l, sub, gate_c, gate_l):
        mc, ml = mods[l]
        return _mod_params(gate_c, gate_l, norm_g[l, sub], mc[3 * sub], ml[3 * sub], mc[3 * sub + 1], ml[3 * sub + 1])

    (mc0, ml0), (mc1, ml1) = mods
    h = _res_mod(xs, None, params(0, 0, one, one), m)
    y = _mixer_ab(h, ab_w_in[0], ab_conv_w[0], ab_conv_b[0], ab_gate_b[0], ab_alpha_w2[0], ab_alpha_b[0],
                  ab_head_g[0], ab_w_out[0].astype(bf), m)
    xs, h = _res_mod(xs, y, params(0, 1, mc0[2], ml0[2]), m)
    y = _peer(h, peer_w_q[0].astype(bf), peer_sub_keys[0].astype(bf), ub, vb, 0)
    p10 = params(1, 0, mc0[5], ml0[5])
    _, hc = _res_mod(xs, y, p10, m, 0, m)
    xl, hl = _res_mod(xs, y, p10, m, m)
    y = _mixer_cd(hc, hl, cd_w_in[0].astype(bf), cd_qk_g[0], cd_rpb[0], cd_sink[0], cd_w_out[0].astype(bf))
    xl, h = _res_mod(xl, y, params(1, 1, ml1[2], ml1[2]), 0)
    y = _peer(h, peer_w_q[1].astype(bf), peer_sub_keys[1].astype(bf), ub, vb, 1)
    return (xl + ml1[5] * y)[None]
```

```python
import functools

import numpy as np
import jax
import jax.numpy as jnp
from jax import lax
from jax.experimental import pallas as pl
from jax.experimental.pallas import tpu as pltpu

D_MODEL = 2048
SEQ = 8192
DEPTH = 2
GRID_W = 64
CTX_LEN = 256
HEAD_DIM = 128
N_GROUP_HEADS = 8
EPS = 1e-6
A_HEADS = 8
A_WIDTH = 1024
CONV_W = 3
B_HEADS = 8
B_KEY_DIM = 64
B_KEY_WIDTH = 512
B_VAL_WIDTH = 1024
GLA_RANK = 16
GLA_TAU = 16.0
SCAN_CHUNK = 64
C_HEADS = 8
C_WIDTH = 1024
WIN_H = 8
WIN_W = 16
D_HEADS = 8
D_KV_HEADS = 2
D_GROUP = 4
D_WIDTH = 1024
D_KV_WIDTH = 256
WINDOW = 128
ROPE_BASE = 10000.0
PEER_HEADS = 8
PEER_NKEYS = 128
PEER_EXPERTS = PEER_NKEYS * PEER_NKEYS
PEER_QDIM = 256
PEER_TOPK = 16
PEER_SLOTS = PEER_HEADS * PEER_TOPK
AB_SPLITS = (A_WIDTH, A_WIDTH, A_WIDTH, A_WIDTH, 4 * A_HEADS,
             B_KEY_WIDTH, B_KEY_WIDTH, B_VAL_WIDTH, B_VAL_WIDTH, 2 * GLA_RANK)
CD_SPLITS = (C_WIDTH, C_WIDTH, D_KV_WIDTH, D_KV_WIDTH, C_WIDTH, D_WIDTH)
CD_KV_COLS = sum(CD_SPLITS[:4])

VMEM_LIMIT_BYTES = 56 * 1024 * 1024
LANES = 128


def _row_tile(m):
    for t in (1024, 768, 512, 256, 128):
        if m % t == 0:
            return t
    raise ValueError(f"unsupported row count {m}")


def _mm_kernel(x_ref, w_ref, o_ref):
    o_ref[...] = jnp.dot(x_ref[...], w_ref[...], preferred_element_type=jnp.float32).astype(o_ref.dtype)


def _matmul(x, w, out_dtype=jnp.float32):
    m, k = x.shape
    n = w.shape[1]
    bn = 1024 if n % 1024 == 0 else min(512, n)
    n_pad = -(-n // bn) * bn
    if n_pad != n:
        w = jnp.pad(w, ((0, 0), (0, n_pad - n)))
    bm = _row_tile(m)
    out = pl.pallas_call(
        _mm_kernel,
        grid=(m // bm, n_pad // bn),
        in_specs=[pl.BlockSpec((bm, k), lambda i, j: (i, 0)),
                  pl.BlockSpec((k, bn), lambda i, j: (0, j))],
        out_specs=pl.BlockSpec((bm, bn), lambda i, j: (i, j)),
        out_shape=jax.ShapeDtypeStruct((m, n_pad), out_dtype),
        compiler_params=pltpu.CompilerParams(
            dimension_semantics=("parallel", "arbitrary"), vmem_limit_bytes=VMEM_LIMIT_BYTES),
        name="matmul",
    )(x.astype(jnp.bfloat16), w.astype(jnp.bfloat16))
    return out[:, :n] if n_pad != n else out


ROUTE_TOKENS = (768, 512, 256, 128)
GATE_TOKENS = 128
NEG_INF = float("-inf")


def _top16(s, key, big, vals_ref, keys_ref):
    for it in range(PEER_TOPK):
        m = jnp.max(s, axis=0, keepdims=True)
        k = jnp.min(jnp.where(s == m, key, big), axis=0, keepdims=True)
        vals_ref[it:it + 1, :] = m
        keys_ref[it:it + 1, :] = k
        s = jnp.where(key == k, NEG_INF, s)
    return vals_ref[...], keys_ref[...]


def _route_kernel(h_ref, wq_ref, k_ref, i1_ref, i2_ref, g_ref, *top_refs):
    bt = h_ref.shape[0]
    q = jnp.dot(h_ref[...], wq_ref[...], preferred_element_type=jnp.float32).astype(jnp.bfloat16)
    nt = (((1,), (1,)), ((), ()))
    s1 = lax.dot_general(k_ref[0, 0], q[:, :PEER_NKEYS], nt, preferred_element_type=jnp.float32)
    s2 = lax.dot_general(k_ref[0, 1], q[:, PEER_NKEYS:], nt, preferred_element_type=jnp.float32)
    row = lax.broadcasted_iota(jnp.int32, (PEER_NKEYS, bt), 0).astype(jnp.float32)
    v1, r1 = _top16(s1, row, float(PEER_NKEYS), *top_refs[0:2])
    v2, r2 = _top16(s2, row, float(PEER_NKEYS), *top_refs[2:4])
    i16 = lax.broadcasted_iota(jnp.int32, (16, bt), 0).astype(jnp.float32)
    i8 = lax.broadcasted_iota(jnp.int32, (8, bt), 0).astype(jnp.float32)
    cand = [v1[0:1] + v2]
    flat = [i16]
    for a in range(1, 8):
        cand.append(v1[a:a + 1] + v2[0:8])
        flat.append(i8 + float(a * PEER_TOPK))
    cand.append(v1[8:16] + v2[0:1])
    flat.append((i8 + 8.0) * float(PEER_TOPK))
    best, key = _top16(jnp.concatenate(cand, axis=0), jnp.concatenate(flat, axis=0), float(PEER_TOPK * PEER_TOPK),
                       *top_refs[4:6])
    key = key.astype(jnp.int32)
    a_sel = jnp.right_shift(key, 4)
    b_sel = jnp.bitwise_and(key, PEER_TOPK - 1)
    e1 = jnp.zeros_like(r1)
    e2 = jnp.zeros_like(r2)
    for r in range(PEER_TOPK):
        e1 = jnp.where(a_sel == r, r1[r:r + 1], e1)
        e2 = jnp.where(b_sel == r, r2[r:r + 1], e2)
    e1 = e1.astype(jnp.int32)
    e2 = e2.astype(jnp.int32)
    ex = jnp.exp(best - best[0:1])
    i1_ref[...] = e1
    i2_ref[...] = e2
    g_ref[...] = ex / jnp.sum(ex, axis=0, keepdims=True)


def _route(hb, wq, keys):
    t, d = hb.shape
    bt = next(b for b in ROUTE_TOKENS if t % b == 0)
    slot_spec = pl.BlockSpec((PEER_TOPK, bt), lambda i, h: (h, i))
    return pl.pallas_call(
        _route_kernel,
        grid=(t // bt, PEER_HEADS),
        in_specs=[pl.BlockSpec((bt, d), lambda i, h: (i, 0)),
                  pl.BlockSpec((d, PEER_QDIM), lambda i, h: (0, h)),
                  pl.BlockSpec((1, 2, PEER_NKEYS, PEER_QDIM // 2), lambda i, h: (h, 0, 0, 0))],
        out_specs=[slot_spec, slot_spec, slot_spec],
        out_shape=[jax.ShapeDtypeStruct((PEER_SLOTS, t), jnp.int32),
                   jax.ShapeDtypeStruct((PEER_SLOTS, t), jnp.int32),
                   jax.ShapeDtypeStruct((PEER_SLOTS, t), jnp.float32)],
        scratch_shapes=[pltpu.VMEM((PEER_TOPK, bt), jnp.float32)] * 6,
        compiler_params=pltpu.CompilerParams(
            dimension_semantics=("parallel", "arbitrary"), vmem_limit_bytes=VMEM_LIMIT_BYTES),
        name="peer_route",
    )(hb, wq, keys)


GATE_GROUP = 16


def _gate_matrix_kernel(i1_ref, i2_ref, g_ref, o_ref, i1_t, g_t, stage):
    tb = o_ref.shape[1]
    i1_t[...] = i1_ref[...].T
    g_t[...] = g_ref[...].T
    row_id = lax.broadcasted_iota(jnp.int32, (PEER_NKEYS, PEER_SLOTS), 0)
    lane_id = lax.broadcasted_iota(jnp.int32, (PEER_SLOTS, PEER_NKEYS), 1)
    zero = jnp.zeros((PEER_NKEYS, PEER_SLOTS), jnp.bfloat16)

    for tp in range(tb // 2):
        ta = 2 * tp
        tb_ = ta + 1
        ca = jnp.where(row_id == i1_t[ta:ta + 1, :], g_t[ta:ta + 1, :], 0.0).astype(jnp.bfloat16)
        cb = jnp.where(row_id == i1_t[tb_:tb_ + 1, :], g_t[tb_:tb_ + 1, :], 0.0).astype(jnp.bfloat16)
        oa = jnp.where(i2_ref[:, ta:ta + 1] == lane_id, 1.0, 0.0).astype(jnp.bfloat16)
        ob = jnp.where(i2_ref[:, tb_:tb_ + 1] == lane_id, 1.0, 0.0).astype(jnp.bfloat16)
        lhs = jnp.concatenate([ca, cb], axis=1)
        rhs = jnp.concatenate([jnp.concatenate([oa, zero], axis=1),
                               jnp.concatenate([zero, ob], axis=1)], axis=0)
        out = jnp.dot(lhs, rhs, preferred_element_type=jnp.float32)
        stage[ta % GATE_GROUP] = out[:, :PEER_NKEYS]
        stage[tb_ % GATE_GROUP] = out[:, PEER_NKEYS:]
        if tb_ % GATE_GROUP == GATE_GROUP - 1:
            t0 = tb_ + 1 - GATE_GROUP
            o_ref[:, t0:t0 + GATE_GROUP, :] = jnp.swapaxes(stage[...], 0, 1).astype(o_ref.dtype)


def _gate_matrix(i1, i2, g):
    t = i1.shape[1]
    tb = GATE_TOKENS
    return pl.pallas_call(
        _gate_matrix_kernel,
        grid=(t // tb,),
        in_specs=[pl.BlockSpec((PEER_SLOTS, tb), lambda i: (0, i))] * 3,
        out_specs=pl.BlockSpec((PEER_NKEYS, tb, PEER_NKEYS), lambda i: (0, i, 0)),
        out_shape=jax.ShapeDtypeStruct((PEER_NKEYS, t, PEER_NKEYS), jnp.bfloat16),
        scratch_shapes=[pltpu.VMEM((tb, PEER_SLOTS), jnp.int32),
                        pltpu.VMEM((tb, PEER_SLOTS), jnp.float32),
                        pltpu.VMEM((GATE_GROUP, PEER_NKEYS, PEER_NKEYS), jnp.float32)],
        compiler_params=pltpu.CompilerParams(
            dimension_semantics=("parallel",), vmem_limit_bytes=VMEM_LIMIT_BYTES),
        name="peer_gate_matrix",
    )(i1, i2, g)


def _gelu_tanh(x):
    return 0.5 * x * (1.0 + jnp.tanh(0.7978845608028654 * (x + 0.044715 * x * x * x)))


FP8 = jnp.float8_e4m3fn
PEER_ROW_GROUPS = 2
FP8_TARGET = 240.0


def _row_scale(x):
    amax = jnp.max(jnp.abs(x), axis=-1, keepdims=True)
    pos = amax > 0.0
    return jnp.where(pos, FP8_TARGET / amax, 1.0), jnp.where(pos, amax * (1.0 / FP8_TARGET), 1.0)


def _quant_kernel(x_ref, o_ref, s_ref):
    x = x_ref[0]
    scale, inv = _row_scale(x)
    o_ref[0] = (x * scale).astype(o_ref.dtype)
    s_ref[0] = jnp.broadcast_to(inv, s_ref.shape[1:])


def _quant_rows(x):
    nl, r, c = x.shape
    br = 1024
    q, s = pl.pallas_call(
        _quant_kernel,
        grid=(nl, r // br),
        in_specs=[pl.BlockSpec((1, br, c), lambda l, i: (l, i, 0))],
        out_specs=[pl.BlockSpec((1, br, c), lambda l, i: (l, i, 0)),
                   pl.BlockSpec((1, br, LANES), lambda l, i: (l, i, 0))],
        out_shape=[jax.ShapeDtypeStruct(x.shape, FP8), jax.ShapeDtypeStruct((nl, r, LANES), jnp.float32)],
        compiler_params=pltpu.CompilerParams(
            dimension_semantics=("parallel", "parallel"), vmem_limit_bytes=VMEM_LIMIT_BYTES),
        name="quant_rows",
    )(x)
    return q, s[:, :, 0][:, None, :]


def _peer_dense_kernel(h_ref, u_ref, su_ref, v_ref, sv_ref, g_ref, o_ref, h8_ref, sh_ref):
    j = pl.program_id(1)

    @pl.when(j == 0)
    def _():
        h = h_ref[...].astype(jnp.float32)
        scale, inv = _row_scale(h)
        h8_ref[...] = (h * scale).astype(h8_ref.dtype)
        sh_ref[...] = jnp.broadcast_to(inv, sh_ref.shape)

    nt = (((1,), (1,)), ((), ()))
    bm = h8_ref.shape[0]
    rows = [slice(r * bm // PEER_ROW_GROUPS, (r + 1) * bm // PEER_ROW_GROUPS) for r in range(PEER_ROW_GROUPS)]
    accs = [lax.dot_general(h8_ref[rs, :], u_ref[0], nt, preferred_element_type=jnp.float32) for rs in rows]
    ws = []
    for rs, acc in zip(rows, accs):
        act = _gelu_tanh(acc * sh_ref[rs, 0:1] * su_ref[0])
        gate = jnp.concatenate([g_ref[r, rs, :] for r in range(g_ref.shape[0])], axis=1)
        w = act * gate.astype(jnp.float32) * sv_ref[0]
        scale, inv = _row_scale(w)
        ws.append(((w * scale).astype(FP8), inv))
    contribs = [jnp.dot(w8, v_ref[0], preferred_element_type=jnp.float32) * inv for w8, inv in ws]

    @pl.when(j == 0)
    def _():
        for rs, contrib in zip(rows, contribs):
            o_ref[rs, :] = contrib

    @pl.when(j != 0)
    def _():
        for rs, contrib in zip(rows, contribs):
            o_ref[rs, :] += contrib


def _peer_dense(h, u, v, gmat, layer):
    t, d = h.shape
    (u8, su), (v8, sv) = u, v
    e = v8.shape[1]
    bm = _row_tile(t)
    be = 1024
    tab = pl.BlockSpec((1, be, d), lambda i, j: (layer, j, 0))
    sc = pl.BlockSpec((1, 1, be), lambda i, j: (layer, 0, j))
    return pl.pallas_call(
        _peer_dense_kernel,
        grid=(t // bm, e // be),
        in_specs=[pl.BlockSpec((bm, d), lambda i, j: (i, 0)), tab, sc, tab, sc,
                  pl.BlockSpec((be // PEER_NKEYS, bm, PEER_NKEYS), lambda i, j: (j, i, 0))],
        out_specs=pl.BlockSpec((bm, d), lambda i, j: (i, 0)),
        out_shape=jax.ShapeDtypeStruct((t, d), jnp.float32),
        scratch_shapes=[pltpu.VMEM((bm, d), FP8), pltpu.VMEM((bm, LANES), jnp.float32)],
        compiler_params=pltpu.CompilerParams(
            dimension_semantics=("parallel", "arbitrary"), vmem_limit_bytes=VMEM_LIMIT_BYTES),
        name="peer_dense",
    )(h, u8, su, v8, sv, gmat)


def _peer(h, wq, keys, u, v, layer):
    hb = h
    i1, i2, g = _route(hb, wq, keys)
    return _peer_dense(hb, u, v, _gate_matrix(i1, i2, g), layer)


ROPE_HALF = 32
NA_QROWS = 8
NA_KROWS = 16
SWA_QB = 256
SWA_KB = SWA_QB + 2 * WINDOW

P_CK, P_CV, P_DK, P_DV, P_CQ, P_DQ = 0, 8, 16, 18, 20, 28
QK_CK, QK_CQ, QK_DK, QK_DQ = 0, 8, 16, 20


def _rope_tables(n):
    t = np.arange(n)
    freqs = (np.float32(ROPE_BASE) ** (-np.arange(ROPE_HALF, dtype=np.float32) / np.float32(ROPE_HALF))).astype(np.float32)
    ang_r = (t // GRID_W).astype(np.float32)[:, None] * freqs
    ang_c = (t % GRID_W).astype(np.float32)[:, None] * freqs
    cos = np.concatenate([np.cos(ang_r), np.cos(ang_r), np.cos(ang_c), np.cos(ang_c)], axis=-1)
    sin = np.concatenate([-np.sin(ang_r), np.sin(ang_r), -np.sin(ang_c), np.sin(ang_c)], axis=-1)
    return cos.astype(np.float32), sin.astype(np.float32)


QK_GROUP = 4


def _qk_prep_kernel(tbl_ref, p_ref, g_ref, cos_ref, sin_ref, o_ref):
    del tbl_ref
    for hh in range(QK_GROUP):
        sl = slice(hh * HEAD_DIM, (hh + 1) * HEAD_DIM)
        x = p_ref[:, sl].astype(jnp.float32)
        y = x * lax.rsqrt(jnp.mean(x * x, axis=-1, keepdims=True) + EPS) * g_ref[0]
        lane = lax.broadcasted_iota(jnp.int32, y.shape, 1)
        first = jnp.bitwise_and(lane, 2 * ROPE_HALF - 1) < ROPE_HALF
        partner = jnp.where(first, pltpu.roll(y, LANES - ROPE_HALF, 1), pltpu.roll(y, ROPE_HALF, 1))
        o_ref[:, sl] = (y * cos_ref[0] + partner * sin_ref[0]).astype(o_ref.dtype)


def _qk_prep(p, table, gains, cos, sin):
    n = p.shape[0]
    ngrp = table.shape[1]
    w = QK_GROUP * HEAD_DIM
    bt = 2048 if n % 2048 == 0 else n
    grid_spec = pltpu.PrefetchScalarGridSpec(
        num_scalar_prefetch=1,
        grid=(n // bt, ngrp),
        in_specs=[pl.BlockSpec((bt, w), lambda i, j, tbl: (i, tbl[0, j])),
                  pl.BlockSpec((1, 1, HEAD_DIM), lambda i, j, tbl: (tbl[1, j], 0, 0)),
                  pl.BlockSpec((1, bt, HEAD_DIM), lambda i, j, tbl: (tbl[2, j], i, 0)),
                  pl.BlockSpec((1, bt, HEAD_DIM), lambda i, j, tbl: (tbl[2, j], i, 0))],
        out_specs=pl.BlockSpec((bt, w), lambda i, j, tbl: (i, j)),
    )
    return pl.pallas_call(
        _qk_prep_kernel,
        grid_spec=grid_spec,
        out_shape=jax.ShapeDtypeStruct((n, ngrp * w), jnp.bfloat16),
        compiler_params=pltpu.CompilerParams(
            dimension_semantics=("parallel", "arbitrary"), vmem_limit_bytes=VMEM_LIMIT_BYTES),
        name="qk_prep",
    )(table, p, gains, cos, sin)


def _na_block_types(n):
    rows = n // GRID_W
    return ((NA_QROWS, NA_QROWS - 4), (0, 0), (rows - NA_QROWS, rows - NA_KROWS))


def _na_row_ok(n, r0, rlo, qi, kj):
    rows = n // GRID_W
    rs = min(max(r0 + qi - WIN_H // 2, 0), rows - WIN_H)
    return rs <= rlo + kj < rs + WIN_H


def _na_pair_tables(rpb):
    nc = 2 * WIN_W - 1
    c = np.arange(GRID_W)[:, None]
    kc = np.arange(GRID_W)[None, :]
    cs = np.clip(c - WIN_W // 2, 0, GRID_W - WIN_W)
    col_ok = (kc >= cs) & (kc < cs + WIN_W)
    oh_col = (np.clip(kc - c + WIN_W - 1, 0, nc - 1)[None] == np.arange(nc)[:, None, None]).astype(np.float32)
    colb = jnp.einsum('hab,bcd->hacd', rpb.astype(jnp.float32), jnp.asarray(oh_col), precision=lax.Precision.HIGHEST)
    colb = jnp.where(jnp.asarray(col_ok)[None, None], colb, NEG_INF)
    none = jnp.full_like(colb[:, :1], NEG_INF)
    first = jnp.concatenate([none, colb], axis=1)
    second = jnp.concatenate([colb, none], axis=1)
    gone = jnp.full_like(first, NEG_INF)
    pair = lambda lo, hi: jnp.concatenate([lo, hi], axis=-1)
    return jnp.stack([pair(first, second), pair(first, gone), pair(gone, second)], axis=1)


def _softmax_pv(parts, extra_logit=None):
    m = parts[0][0].max(axis=-1, keepdims=True)
    for s, _ in parts[1:]:
        m = jnp.maximum(m, s.max(axis=-1, keepdims=True))
    if extra_logit is not None:
        m = jnp.maximum(m, extra_logit)
    den = jnp.exp(extra_logit - m) if extra_logit is not None else 0.0
    acc = None
    for s, v in parts:
        p = jnp.exp(s - m)
        den = den + p.sum(axis=-1, keepdims=True)
        pv = jnp.dot(p.astype(jnp.bfloat16), v, preferred_element_type=jnp.float32)
        acc = pv if acc is None else acc + pv
    return acc / den


def _na_kernel(q_ref, k_ref, v_ref, kc_ref, vc_ref, pt_ref, o_ref, bias_ref):
    n = q_ref.shape[0]
    qb = NA_QROWS * GRID_W
    kb = NA_KROWS * GRID_W
    nblk = n // qb
    nt = (((1,), (1,)), ((), ()))
    kc = kc_ref[...]
    vc = vc_ref[...]
    for kind, (r0, rlo) in enumerate(_na_block_types(n)):
        for qi in range(NA_QROWS):
            for kp in range(NA_KROWS // 2):
                ok0 = _na_row_ok(n, r0, rlo, qi, 2 * kp)
                ok1 = _na_row_ok(n, r0, rlo, qi, 2 * kp + 1)
                e = rlo + 2 * kp - (r0 + qi) + WIN_H - 1
                rows, cols = slice(qi * GRID_W, (qi + 1) * GRID_W), slice(kp * LANES, (kp + 1) * LANES)
                if ok0 or ok1:
                    bias_ref[kind, rows, cols] = pt_ref[0, 0 if (ok0 and ok1) else (1 if ok0 else 2), e + 1]
                else:
                    bias_ref[kind, rows, cols] = jnp.full((GRID_W, LANES), NEG_INF, jnp.float32)

    def body(b, carry):
        q0 = pl.multiple_of(b * qb, qb)
        ks = pl.multiple_of(jnp.clip(b * qb - (NA_KROWS - NA_QROWS) // 2 * GRID_W, 0, n - kb), 2 * LANES)
        kind = jnp.where(b == 0, 1, jnp.where(b == nblk - 1, 2, 0))
        q = q_ref[pl.ds(q0, qb), :]
        s_w = lax.dot_general(q, k_ref[pl.ds(ks, kb), :], nt, preferred_element_type=jnp.float32) + bias_ref[kind]
        s_c = lax.dot_general(q, kc, nt, preferred_element_type=jnp.float32)
        o = _softmax_pv([(s_w, v_ref[pl.ds(ks, kb), :]), (s_c, vc)])
        o_ref[pl.ds(q0, qb), :] = o.astype(o_ref.dtype)
        return carry

    lax.fori_loop(0, nblk, body, 0, unroll=2)


def _neighbourhood_attn(qk, p, qk_c, p_c, pair_tables):
    n = qk.shape[0]
    m = qk_c.shape[0]

    def full(col0):
        return pl.BlockSpec((n, HEAD_DIM), lambda h: (0, col0 + h))

    def ctx(col0):
        return pl.BlockSpec((m, HEAD_DIM), lambda h: (0, col0 + h))

    return pl.pallas_call(
        _na_kernel,
        grid=(C_HEADS,),
        in_specs=[full(QK_CQ), full(QK_CK), full(P_CV), ctx(QK_CK), ctx(P_CV),
                  pl.BlockSpec((1, 3, 2 * WIN_H, GRID_W, LANES), lambda h: (h, 0, 0, 0, 0))],
        out_specs=pl.BlockSpec((n, HEAD_DIM), lambda h: (0, h)),
        out_shape=jax.ShapeDtypeStruct((n, C_WIDTH), jnp.bfloat16),
        scratch_shapes=[pltpu.VMEM((3, NA_QROWS * GRID_W, NA_KROWS * GRID_W), jnp.float32)],
        compiler_params=pltpu.CompilerParams(
            dimension_semantics=("parallel",), vmem_limit_bytes=VMEM_LIMIT_BYTES),
        name="neighbourhood_attn",
    )(qk, qk, p, qk_c, p_c, pair_tables)


def _swa_kernel(q_ref, k_ref, v_ref, kc_ref, vc_ref, sink_ref, o_ref, band_ref):
    n = q_ref.shape[0]
    nblk = n // SWA_QB
    nt = (((1,), (1,)), ((), ()))
    kc = kc_ref[...]
    vc = vc_ref[...]
    kv = pl.program_id(0)
    sink = jnp.concatenate(
        [jnp.broadcast_to(sink_ref[pl.ds(kv * D_GROUP + g, 1), 0:1], (SWA_QB, 1)) for g in range(D_GROUP)], axis=0)
    q_off = jnp.bitwise_and(lax.broadcasted_iota(jnp.int32, (D_GROUP * SWA_QB, SWA_KB), 0), SWA_QB - 1)
    k_off = lax.broadcasted_iota(jnp.int32, (D_GROUP * SWA_QB, SWA_KB), 1)
    for kind, lead in enumerate((WINDOW, 0, SWA_KB - SWA_QB)):
        band_ref[kind] = jnp.where(jnp.abs(q_off - k_off + lead) <= WINDOW, 0.0, NEG_INF)

    def body(b, carry):
        q0 = pl.multiple_of(b * SWA_QB, SWA_QB)
        ks = pl.multiple_of(jnp.clip(q0 - WINDOW, 0, n - SWA_KB), LANES)
        kind = jnp.where(b == 0, 1, jnp.where(b == nblk - 1, 2, 0))
        q = jnp.concatenate([q_ref[pl.ds(q0, SWA_QB), g * HEAD_DIM:(g + 1) * HEAD_DIM] for g in range(D_GROUP)], axis=0)
        s_w = lax.dot_general(q, k_ref[pl.ds(ks, SWA_KB), :], nt, preferred_element_type=jnp.float32) + band_ref[kind]
        s_c = lax.dot_general(q, kc, nt, preferred_element_type=jnp.float32)
        o = _softmax_pv([(s_w, v_ref[pl.ds(ks, SWA_KB), :]), (s_c, vc)], extra_logit=sink)
        for g in range(D_GROUP):
            o_ref[pl.ds(q0, SWA_QB), g * HEAD_DIM:(g + 1) * HEAD_DIM] = o[g * SWA_QB:(g + 1) * SWA_QB].astype(o_ref.dtype)
        return carry

    lax.fori_loop(0, nblk, body, 0)


def _window_attn(qk, p, qk_c, p_c, sink_rows):
    n = qk.shape[0]
    m = qk_c.shape[0]
    gw = D_GROUP * HEAD_DIM
    return pl.pallas_call(
        _swa_kernel,
        grid=(D_KV_HEADS,),
        in_specs=[pl.BlockSpec((n, gw), lambda kv: (0, QK_DQ // D_GROUP + kv)),
                  pl.BlockSpec((n, HEAD_DIM), lambda kv: (0, QK_DK + kv)),
                  pl.BlockSpec((n, HEAD_DIM), lambda kv: (0, P_DV + kv)),
                  pl.BlockSpec((m, HEAD_DIM), lambda kv: (0, QK_DK + kv)),
                  pl.BlockSpec((m, HEAD_DIM), lambda kv: (0, P_DV + kv)),
                  pl.BlockSpec((D_HEADS, LANES), lambda kv: (0, 0))],
        out_specs=pl.BlockSpec((n, gw), lambda kv: (0, kv)),
        out_shape=jax.ShapeDtypeStruct((n, D_WIDTH), jnp.bfloat16),
        scratch_shapes=[pltpu.VMEM((3, D_GROUP * SWA_QB, SWA_KB), jnp.float32)],
        compiler_params=pltpu.CompilerParams(
            dimension_semantics=("parallel",), vmem_limit_bytes=VMEM_LIMIT_BYTES),
        name="window_attn",
    )(qk, qk, p, qk_c, p_c, sink_rows)


def _mixer_cd(h_c, h_l, w_in, qk_g, rpb, sink, w_out):
    n = h_l.shape[0]
    m = h_c.shape[0]
    scale = HEAD_DIM ** -0.5
    p_l = _matmul(h_l, w_in, jnp.bfloat16)
    p_c = _matmul(h_c, w_in[:, :CD_KV_COLS], jnp.bfloat16)
    gains = jnp.stack([qk_g[0] * scale, qk_g[1], qk_g[2] * scale, qk_g[3]]).astype(jnp.float32)[:, None, :]
    cos, sin = _rope_tables(n)
    cs_l = (np.stack([np.ones_like(cos), cos]), np.stack([np.zeros_like(sin), sin]))
    cs_c = (np.ones((1, m, HEAD_DIM), np.float32), np.zeros((1, m, HEAD_DIM), np.float32))
    grp = lambda blk: blk // QK_GROUP
    src = [grp(P_CK), grp(P_CK) + 1, grp(P_CQ), grp(P_CQ) + 1, grp(P_DK), grp(P_DQ), grp(P_DQ) + 1]
    gain = [1, 1, 0, 0, 3, 2, 2]
    rope = [0, 0, 0, 0, 1, 1, 1]
    tbl_l = jnp.asarray(np.array([src, gain, rope], np.int32))
    ctx_groups = [0, 1, 0, 1, 4]
    tbl_c = jnp.asarray(np.array([[src[g] for g in ctx_groups], [gain[g] for g in ctx_groups], [0] * 5], np.int32))
    qk_l = _qk_prep(p_l, tbl_l, gains, jnp.asarray(cs_l[0]), jnp.asarray(cs_l[1]))
    qk_c = _qk_prep(p_c, tbl_c, gains, jnp.asarray(cs_c[0]), jnp.asarray(cs_c[1]))
    c_out = _neighbourhood_attn(qk_l, p_l, qk_c, p_c, _na_pair_tables(rpb))
    sink_rows = jnp.broadcast_to(sink.astype(jnp.float32)[:, None], (D_HEADS, LANES))
    d_out = _window_attn(qk_l, p_l, qk_c, p_c, sink_rows)
    return _matmul(jnp.concatenate([c_out, d_out], axis=-1), w_out)


SCAN_L = 128
GLA_SUB = 4
N_SCAN_STATES = 2 * A_HEADS
GATE_LR_LANE = 4 * A_HEADS


def _log_sigmoid(x):
    return jnp.minimum(x, 0.0) - jnp.log(1.0 + jnp.exp(-jnp.abs(x)))


def _tri_mask(rev):
    r = lax.broadcasted_iota(jnp.int32, (SCAN_L, SCAN_L), 0)
    c = lax.broadcasted_iota(jnp.int32, (SCAN_L, SCAN_L), 1)
    return (c >= r) if rev else (c <= r)


def _cumsum_rows(tri, x):
    hi = x.astype(jnp.bfloat16)
    r1 = x - hi.astype(jnp.float32)
    mid = r1.astype(jnp.bfloat16)
    lo = (r1 - mid.astype(jnp.float32)).astype(jnp.bfloat16)
    dot = lambda p: jnp.dot(tri, p, preferred_element_type=jnp.float32)
    return dot(hi) + dot(mid) + dot(lo)


def _conv_kernel(x_ref, w_ref, b_ref, o_ref, *, seg, scale):
    x = x_ref[...].astype(jnp.float32)
    n = x.shape[0]
    t = lax.broadcasted_iota(jnp.int32, x.shape, 0)
    first = (t == 0) | (t == seg)
    last = (t == seg - 1) | (t == n - 1)
    prev = jnp.where(first, 0.0, pltpu.roll(x, 1, 0))
    nxt = jnp.where(last, 0.0, pltpu.roll(x, n - 1, 0))
    y = w_ref[0:1, :] * prev + w_ref[1:2, :] * x + w_ref[2:3, :] * nxt + b_ref[...]
    y = y * jax.nn.sigmoid(y)
    j = pl.program_id(0)
    o_ref[...] = (y * jnp.where(j >= A_HEADS, scale, 1.0)).astype(o_ref.dtype)


def _short_conv_silu(p, conv_w, conv_b, seg):
    t = p.shape[0]
    nblk = 2 * A_WIDTH // LANES
    return pl.pallas_call(
        functools.partial(_conv_kernel, seg=seg, scale=HEAD_DIM ** -0.5),
        grid=(nblk,),
        in_specs=[pl.BlockSpec((t, LANES), lambda j: (0, j)),
                  pl.BlockSpec((8, LANES), lambda j: (0, j)),
                  pl.BlockSpec((1, LANES), lambda j: (0, j))],
        out_specs=pl.BlockSpec((t, LANES), lambda j: (0, j)),
        out_shape=jax.ShapeDtypeStruct((t, 2 * A_WIDTH), jnp.bfloat16),
        compiler_params=pltpu.CompilerParams(
            dimension_semantics=("parallel",), vmem_limit_bytes=VMEM_LIMIT_BYTES),
        name="short_conv_silu",
    )(p, jnp.pad(conv_w, ((0, 8 - CONV_W), (0, 0))), conv_b[None, :])


def _scan_chunk_index(j, n_ctx_chunks, n_chunks, rev):
    if not rev:
        return j
    return jnp.where(j < n_ctx_chunks, n_ctx_chunks - 1 - j, n_chunks + n_ctx_chunks - 1 - j)


def _mlstm_kernel(qf_ref, kf_ref, vf_ref, gf_ref, qb_ref, kb_ref, vb_ref, gb_ref, gbias_ref,
                  yf_ref, yb_ref, c_ref, m_ref):
    @pl.when(pl.program_id(0) == 0)
    def _():
        c_ref[...] = jnp.zeros_like(c_ref)
        m_ref[...] = jnp.zeros_like(m_ref)

    lane = lax.broadcasted_iota(jnp.int32, (SCAN_L, LANES), 1)
    is_forget = jnp.bitwise_and(lane, 2 * A_HEADS - 1) >= A_HEADS
    ones = jnp.ones((SCAN_L, HEAD_DIM), jnp.bfloat16)
    nt = (((1,), (1,)), ((), ()))
    tn = (((0,), (0,)), ((), ()))
    for z, (q_ref, k_ref, v_ref, g_ref, y_ref) in enumerate(
            ((qf_ref, kf_ref, vf_ref, gf_ref, yf_ref), (qb_ref, kb_ref, vb_ref, gb_ref, yb_ref))):
        rev = z == 1
        mask = _tri_mask(rev)
        gates = g_ref[...] + gbias_ref[...]
        gates = jnp.where(is_forget, _log_sigmoid(gates), gates)
        csum = _cumsum_rows(mask.astype(jnp.bfloat16), jnp.where(is_forget, gates, 0.0))
        gates_t = gates.T
        csum_t = csum.T
        end = 0 if rev else SCAN_L - 1
        heads = []
        for h in range(A_HEADS):
            li, lf = z * 2 * A_HEADS + h, z * 2 * A_HEADS + A_HEADS + h
            sl = slice(h * HEAD_DIM, (h + 1) * HEAD_DIM)
            q, k, v = q_ref[:, sl], k_ref[:, sl], v_ref[:, sl]
            b_col, b_row = csum[:, lf:lf + 1], csum_t[lf:lf + 1, :]
            i_col, i_row = gates[:, li:li + 1], gates_t[li:li + 1, :]
            st = z * A_HEADS + h
            m_prev = m_ref[st][0:1, 0:1]
            dlog = jnp.where(mask, b_col - b_row + i_row, NEG_INF)
            m_inter = b_col + m_prev
            m_t = jnp.maximum(m_inter, jnp.max(dlog, axis=-1, keepdims=True))
            s = lax.dot_general(q, k, nt, preferred_element_type=jnp.float32)
            qc = jnp.dot(q, c_ref[st].astype(jnp.bfloat16), preferred_element_type=jnp.float32)
            heads.append(dict(sl=sl, st=st, k=k, v_ext=jnp.concatenate([v, ones], axis=1), b_col=b_col, i_col=i_col,
                              m_prev=m_prev, dlog=dlog, m_inter=m_inter, m_t=m_t, s=s, qc=qc))
        for hd in heads:
            sc = (hd["s"] * jnp.exp(hd["dlog"] - hd["m_t"])).astype(jnp.bfloat16)
            w_inter = jnp.exp(hd["m_inter"] - hd["m_t"])
            r = jnp.dot(sc, hd["v_ext"], preferred_element_type=jnp.float32) + w_inter * hd["qc"]
            num, den = r[:, :HEAD_DIM], r[:, HEAD_DIM:]
            y_ref[:, hd["sl"]] = num / jnp.maximum(jnp.abs(den), jnp.exp(-hd["m_t"]))
        for hd in heads:
            st = hd["st"]
            m_new = hd["m_t"][end:end + 1, :]
            b_end = hd["b_col"][end:end + 1, :]
            w_end = jnp.exp(b_end - hd["b_col"] + hd["i_col"] - m_new)
            decay = jnp.exp(b_end + hd["m_prev"] - m_new)
            kw = (hd["k"].astype(jnp.float32) * w_end).astype(jnp.bfloat16)
            c_ref[st] = decay * c_ref[st] + lax.dot_general(kw, hd["v_ext"], tn, preferred_element_type=jnp.float32)
            m_ref[st] = jnp.broadcast_to(m_new, (8, LANES))


def _scan_specs(n_ctx_chunks, n_chunks, width, col0, rev):
    return pl.BlockSpec((SCAN_L, width),
                        lambda j: (_scan_chunk_index(j, n_ctx_chunks, n_chunks, rev), col0))


def _mlstm(qk, p, gates, gate_bias, seg):
    t = qk.shape[0]
    nck = t // SCAN_L
    ncc = seg // SCAN_L
    ins, specs = [], []
    for rev in (False, True):
        ins += [qk, qk, p, gates]
        specs += [_scan_specs(ncc, nck, A_WIDTH, 0, rev), _scan_specs(ncc, nck, A_WIDTH, 1, rev),
                  _scan_specs(ncc, nck, A_WIDTH, 2, rev), _scan_specs(ncc, nck, LANES, 0, rev)]
    return pl.pallas_call(
        _mlstm_kernel,
        grid=(nck,),
        in_specs=specs + [pl.BlockSpec((1, LANES), lambda j: (0, 0))],
        out_specs=[_scan_specs(ncc, nck, A_WIDTH, 0, False), _scan_specs(ncc, nck, A_WIDTH, 0, True)],
        out_shape=[jax.ShapeDtypeStruct((t, A_WIDTH), jnp.float32)] * 2,
        scratch_shapes=[pltpu.VMEM((N_SCAN_STATES, HEAD_DIM, 2 * HEAD_DIM), jnp.float32),
                        pltpu.VMEM((N_SCAN_STATES, 8, LANES), jnp.float32)],
        compiler_params=pltpu.CompilerParams(
            dimension_semantics=("arbitrary",), vmem_limit_bytes=VMEM_LIMIT_BYTES),
        name="mlstm_scan",
    )(*ins, gate_bias)


def _gla_kernel(qf_ref, kf_ref, vf_ref, gf_ref, qb_ref, kb_ref, vb_ref, gb_ref, w2_ref, ab_ref,
                of_ref, ob_ref, s_ref):
    @pl.when(pl.program_id(0) == 0)
    def _():
        s_ref[...] = jnp.zeros_like(s_ref)

    bf = jnp.bfloat16
    row = lax.broadcasted_iota(jnp.int32, (SCAN_L, LANES), 0)
    lane = lax.broadcasted_iota(jnp.int32, (SCAN_L, LANES), 1)
    head_lane = (lane < B_KEY_DIM, lane >= B_KEY_DIM)
    r_sq = lax.broadcasted_iota(jnp.int32, (SCAN_L, SCAN_L), 0)
    c_sq = lax.broadcasted_iota(jnp.int32, (SCAN_L, SCAN_L), 1)
    in_sub = jnp.bitwise_and(row, GLA_SUB - 1)
    sel_r = lax.broadcasted_iota(jnp.int32, (LANES, 2 * LANES), 0)
    sel_c = lax.broadcasted_iota(jnp.int32, (LANES, 2 * LANES), 1)
    head_sum = ((sel_r < B_KEY_DIM) == (sel_c < LANES)).astype(bf)
    nt = (((1,), (1,)), ((), ()))
    tn = (((0,), (0,)), ((), ()))
    for z, (q_ref, k_ref, v_ref, g_ref, o_ref) in enumerate(
            ((qf_ref, kf_ref, vf_ref, gf_ref, of_ref), (qb_ref, kb_ref, vb_ref, gb_ref, ob_ref))):
        rev = z == 1
        tri = _tri_mask(rev).astype(bf)
        lr = g_ref[...].astype(bf)
        end = 0 if rev else SCAN_L - 1
        levels = []
        m = SCAN_L // 2
        while m >= GLA_SUB:
            shift = (2 * m).bit_length() - 1
            upper = jnp.bitwise_and(row, 2 * m - 1) >= m
            mid = jnp.left_shift(jnp.right_shift(r_sq, shift), shift) + (m if rev else m - 1)
            levels.append(dict(
                keys=upper if rev else ~upper,
                queries=~upper if rev else upper,
                pick_mid=(c_sq == mid).astype(bf),
                same=jnp.right_shift(r_sq, shift) == jnp.right_shift(c_sq, shift)))
            m //= 2
        pick_all = jnp.concatenate([lv["pick_mid"] for lv in levels], axis=0)
        for pr in range(B_HEADS // 2):
            ls = slice(pr * LANES, (pr + 1) * LANES)
            za = jnp.dot(lr, w2_ref[z, :, ls], preferred_element_type=jnp.float32) + ab_ref[z:z + 1, ls]
            b = _cumsum_rows(tri, _log_sigmoid(za) * (1.0 / GLA_TAU))
            q = q_ref[:, ls].astype(jnp.float32) * (B_KEY_DIM ** -0.5)
            k = k_ref[:, ls].astype(jnp.float32)
            v = v_ref[:, 2 * pr * HEAD_DIM:(2 * pr + 2) * HEAD_DIM]
            b_hi = b.astype(bf)
            b_lo = (b - b_hi.astype(jnp.float32)).astype(bf)
            picked = jnp.dot(pick_all, jnp.concatenate([b_hi, b_lo], axis=1), preferred_element_type=jnp.float32)
            q_ms, k_ms = [], []
            for li, lv in enumerate(levels):
                b_mid = picked[li * SCAN_L:(li + 1) * SCAN_L, :LANES] + picked[li * SCAN_L:(li + 1) * SCAN_L, LANES:]
                q_ms.append((q * jnp.exp(jnp.where(lv["queries"], b - b_mid, NEG_INF))).astype(bf))
                k_m = k * jnp.exp(jnp.where(lv["keys"], b_mid - b, NEG_INF))
                k_ms.append([jnp.where(head_lane[hh], k_m, 0.0).astype(bf) for hh in range(2)])
            parts = [[lax.dot_general(q_ms[li], k_ms[li][hh], nt, preferred_element_type=jnp.float32)
                      for hh in range(2)] for li in range(len(levels))]
            fs = []
            for d in range(GLA_SUB):
                shift = (SCAN_L - d) % SCAN_L if rev else d
                k_s = pltpu.roll(k, shift, 0) if d else k
                b_s = pltpu.roll(b, shift, 0) if d else b
                ok_row = (in_sub <= GLA_SUB - 1 - d) if rev else (in_sub >= d)
                fs.append((q * k_s * jnp.exp(jnp.where(ok_row, b - b_s, NEG_INF))).astype(bf))
            red = jnp.dot(jnp.concatenate(fs, axis=0), head_sum, preferred_element_type=jnp.float32)
            attn = []
            for hh in range(2):
                acc = jnp.zeros((SCAN_L, SCAN_L), jnp.float32)
                for li, lv in enumerate(levels):
                    acc = acc + jnp.where(lv["same"], parts[li][hh], 0.0)
                for d in range(GLA_SUB):
                    on_diag = (c_sq == r_sq + d) if rev else (c_sq == r_sq - d)
                    acc = acc + jnp.where(on_diag, red[d * SCAN_L:(d + 1) * SCAN_L, hh * LANES:(hh + 1) * LANES], 0.0)
                attn.append(acc)
            q_in = q * jnp.exp(b)
            b_end = b[end:end + 1, :]
            k_out = (k * jnp.exp(b_end - b)).astype(bf)
            st = z * (B_HEADS // 2) + pr
            s_t = s_ref[st]
            s_bf = s_t.astype(bf)
            for hh in range(2):
                vs = slice(hh * HEAD_DIM, (hh + 1) * HEAD_DIM)
                qh = jnp.where(head_lane[hh], q_in, 0.0).astype(bf)
                o = (jnp.dot(attn[hh].astype(bf), v[:, vs], preferred_element_type=jnp.float32)
                     + lax.dot_general(qh, s_bf[vs, :], nt, preferred_element_type=jnp.float32))
                o_ref[:, (2 * pr + hh) * HEAD_DIM:(2 * pr + hh + 1) * HEAD_DIM] = o
            s_ref[st] = s_t * jnp.exp(b_end) + lax.dot_general(v, k_out, tn, preferred_element_type=jnp.float32)


def _gla(p, gates, w2, alpha_b, seg):
    t = p.shape[0]
    nck = t // SCAN_L
    ncc = seg // SCAN_L
    ins, specs = [], []
    for rev in (False, True):
        ins += [p, p, p, gates]
        specs += [_scan_specs(ncc, nck, B_KEY_WIDTH, P_BQ // B_KEY_WIDTH, rev),
                  _scan_specs(ncc, nck, B_KEY_WIDTH, P_BK // B_KEY_WIDTH, rev),
                  _scan_specs(ncc, nck, B_VAL_WIDTH, P_BV // B_VAL_WIDTH, rev),
                  _scan_specs(ncc, nck, LANES, 0, rev)]
    return pl.pallas_call(
        _gla_kernel,
        grid=(nck,),
        in_specs=specs + [pl.BlockSpec((2, LANES, B_KEY_WIDTH), lambda j: (0, 0, 0)),
                          pl.BlockSpec((2, B_KEY_WIDTH), lambda j: (0, 0))],
        out_specs=[_scan_specs(ncc, nck, B_VAL_WIDTH, 0, False), _scan_specs(ncc, nck, B_VAL_WIDTH, 0, True)],
        out_shape=[jax.ShapeDtypeStruct((t, B_VAL_WIDTH), jnp.float32)] * 2,
        scratch_shapes=[pltpu.VMEM((B_HEADS, 2 * HEAD_DIM, LANES), jnp.float32)],
        compiler_params=pltpu.CompilerParams(
            dimension_semantics=("arbitrary",), vmem_limit_bytes=VMEM_LIMIT_BYTES),
        name="gla_scan",
    )(*ins, w2, alpha_b)


HEAD_OUT_GROUP = 4


def _head_out_kernel(af_ref, ab_ref, bf_ref, bb_ref, gate_ref, g_ref, o_ref):
    is_gla = pl.program_id(1) >= A_HEADS // HEAD_OUT_GROUP
    for hh in range(HEAD_OUT_GROUP):
        sl = slice(hh * HEAD_DIM, (hh + 1) * HEAD_DIM)
        y = jnp.where(is_gla, bf_ref[:, sl] + bb_ref[:, sl], af_ref[:, sl] + ab_ref[:, sl])
        yn = y * lax.rsqrt(jnp.mean(y * y, axis=-1, keepdims=True) + EPS) * g_ref[0]
        gate = gate_ref[:, sl].astype(jnp.float32)
        o_ref[:, sl] = (yn * jax.nn.sigmoid(gate) * jnp.where(is_gla, gate, 1.0)).astype(o_ref.dtype)


def _head_out(ya, yb, p, head_g):
    t = p.shape[0]
    bt = _row_tile(t)
    w = HEAD_OUT_GROUP * HEAD_DIM
    na, nb = A_HEADS // HEAD_OUT_GROUP, B_HEADS // HEAD_OUT_GROUP
    a_spec = pl.BlockSpec((bt, w), lambda i, j: (i, jnp.minimum(j, na - 1)))
    b_spec = pl.BlockSpec((bt, w), lambda i, j: (i, jnp.maximum(j - na, 0)))

    def gate_col(i, j):
        return (i, jnp.where(j < na, P_AO // HEAD_OUT_GROUP + j, P_BR // HEAD_OUT_GROUP + j - na))

    return pl.pallas_call(
        _head_out_kernel,
        grid=(t // bt, na + nb),
        in_specs=[a_spec, a_spec, b_spec, b_spec,
                  pl.BlockSpec((bt, w), gate_col),
                  pl.BlockSpec((1, 1, HEAD_DIM), lambda i, j: (j // na, 0, 0))],
        out_specs=pl.BlockSpec((bt, w), lambda i, j: (i, j)),
        out_shape=jax.ShapeDtypeStruct((t, (A_HEADS + B_HEADS) * HEAD_DIM), jnp.bfloat16),
        compiler_params=pltpu.CompilerParams(
            dimension_semantics=("parallel", "arbitrary"), vmem_limit_bytes=VMEM_LIMIT_BYTES),
        name="head_out",
    )(ya[0], ya[1], yb[0], yb[1], p, head_g.astype(jnp.float32)[:, None, :])


P_AO = 24
P_BQ, P_BK, P_BV = 4096, 4608, 5120
P_BR = 48


def _mixer_ab(h, w_in, conv_w, conv_b, gate_b, alpha_w2, alpha_b, head_g, w_out, seg):
    bf = jnp.bfloat16
    cols = np.cumsum((0,) + AB_SPLITS)
    pick = lambda *ids: jnp.concatenate([w_in[:, cols[i]:cols[i + 1]] for i in ids], axis=1)
    w_main = pick(0, 1, 2, 3, 5, 6, 7, 8).astype(bf)
    w_gate = jnp.pad(pick(4, 9), ((0, 0), (0, LANES - 4 * A_HEADS - 2 * GLA_RANK))).astype(bf)
    hb = h
    p = _matmul(hb, w_main, bf)
    gates = _matmul(hb, w_gate, jnp.float32)
    gate_bias = jnp.pad(gate_b.reshape(1, -1).astype(jnp.float32), ((0, 0), (0, LANES - 4 * A_HEADS)))
    w2 = jnp.zeros((2, LANES, B_KEY_WIDTH), jnp.float32)
    for z in range(2):
        w2 = w2.at[z, GATE_LR_LANE + z * GLA_RANK:GATE_LR_LANE + (z + 1) * GLA_RANK].set(alpha_w2[z])
    qk = _short_conv_silu(p, conv_w, conv_b, seg)
    ya = _mlstm(qk, p, gates, gate_bias, seg)
    yb = _gla(p, gates, w2.astype(bf), alpha_b.astype(jnp.float32), seg)
    return _matmul(_head_out(ya, yb, p, head_g), w_out)


MOD_ROWS = 256


def _res_mod_kernel(x_ref, y_ref, p_ref, xo_ref, h_ref):
    p = p_ref[0]
    x = x_ref[...] + p[0:1, :] * y_ref[...]
    xo_ref[...] = x
    r = x * lax.rsqrt(jnp.mean(x * x, axis=-1, keepdims=True) + EPS) * p[1:2, :]
    h_ref[...] = (r * (1.0 + p[3:4, :]) + p[2:3, :]).astype(h_ref.dtype)


def _mod_kernel(x_ref, p_ref, h_ref):
    p = p_ref[0]
    x = x_ref[...]
    r = x * lax.rsqrt(jnp.mean(x * x, axis=-1, keepdims=True) + EPS) * p[1:2, :]
    h_ref[...] = (r * (1.0 + p[3:4, :]) + p[2:3, :]).astype(h_ref.dtype)


def _mod_params(gate_c, gate_l, gain, shift_c, shift_l, scale_c, scale_l):
    rows = lambda g, sh, sc: jnp.concatenate(
        [g.reshape(1, -1), gain.reshape(1, -1), sh.reshape(1, -1), sc.reshape(1, -1),
         jnp.zeros((4, gain.shape[-1]), jnp.float32)], axis=0)
    return jnp.stack([rows(gate_c, shift_c, scale_c), rows(gate_l, shift_l, scale_l)]).astype(jnp.float32)


def _res_mod(x, y, params, seg, row0=0, n_rows=None):
    d = x.shape[1]
    n_rows = x.shape[0] - row0 if n_rows is None else n_rows
    bt = MOD_ROWS
    off, seg_blocks = row0 // bt, seg // bt
    row_spec = pl.BlockSpec((bt, d), lambda i: (i + off, 0))
    out_spec = pl.BlockSpec((bt, d), lambda i: (i, 0))
    p_spec = pl.BlockSpec((1, 8, d), lambda i: (jnp.where(i + off < seg_blocks, 0, 1), 0, 0))
    cp = pltpu.CompilerParams(dimension_semantics=("parallel",), vmem_limit_bytes=VMEM_LIMIT_BYTES)
    h_shape = jax.ShapeDtypeStruct((n_rows, d), jnp.bfloat16)
    if y is None:
        return pl.pallas_call(_mod_kernel, grid=(n_rows // bt,), in_specs=[row_spec, p_spec], out_specs=out_spec,
                              out_shape=h_shape, compiler_params=cp, name="modulate")(x, params)
    return pl.pallas_call(
        _res_mod_kernel, grid=(n_rows // bt,), in_specs=[row_spec, row_spec, p_spec],
        out_specs=[out_spec, out_spec],
        out_shape=[jax.ShapeDtypeStruct((n_rows, d), jnp.float32), h_shape],
        compiler_params=cp, name="residual_modulate")(x, y, params)


def _modulation_kernel(c_ref, w_ref, b_ref, o_ref):
    act = c_ref[...]
    act = (act * jax.nn.sigmoid(act)).astype(jnp.bfloat16)
    o_ref[0] = jnp.dot(act, w_ref[0].astype(jnp.bfloat16), preferred_element_type=jnp.float32) + b_ref[0]


def _modulation(c, c_ctx, mod_w, mod_b):
    nl, d, n = mod_w.shape
    cond = jnp.concatenate([c_ctx.reshape(1, d), c.reshape(1, d), jnp.zeros((6, d), jnp.float32)], axis=0)
    bn = n // 8
    return pl.pallas_call(
        _modulation_kernel,
        grid=(nl, n // bn),
        in_specs=[pl.BlockSpec((8, d), lambda l, j: (0, 0)),
                  pl.BlockSpec((1, d, bn), lambda l, j: (l, 0, j)),
                  pl.BlockSpec((1, 1, bn), lambda l, j: (l, 0, j))],
        out_specs=pl.BlockSpec((1, 8, bn), lambda l, j: (l, 0, j)),
        out_shape=jax.ShapeDtypeStruct((nl, 8, n), jnp.float32),
        compiler_params=pltpu.CompilerParams(
            dimension_semantics=("parallel", "parallel"), vmem_limit_bytes=VMEM_LIMIT_BYTES),
        name="modulation",
    )(cond, mod_w, mod_b[:, None, :])


def kernel(x, c, ctx, c_ctx, mod_w, mod_b, norm_g, ab_w_in, ab_conv_w, ab_conv_b, ab_gate_b, ab_alpha_w2,
           ab_alpha_b, ab_head_g, ab_w_out, cd_w_in, cd_qk_g, cd_rpb, cd_sink, cd_w_out, peer_w_q,
           peer_sub_keys, peer_u, peer_v):
    assert DEPTH == 2
    bf = jnp.bfloat16
    m = ctx.shape[1]
    xs = jnp.concatenate([ctx[0], x[0]], axis=0)
    ub = _quant_rows(peer_u)
    vb = _quant_rows(peer_v)
    mod_all = _modulation(c, c_ctx, mod_w, mod_b)
    mods = [(jnp.split(mod_all[l, 0], 6), jnp.split(mod_all[l, 1], 6)) for l in range(DEPTH)]
    one = jnp.ones_like(mods[0][0][0])

    def params(l, sub, gate_c, gate_l):
        mc, ml = mods[l]
        return _mod_params(gate_c, gate_l, norm_g[l, sub], mc[3 * sub], ml[3 * sub], mc[3 * sub + 1], ml[3 * sub + 1])

    (mc0, ml0), (mc1, ml1) = mods
    h = _res_mod(xs, None, params(0, 0, one, one), m)
    y = _mixer_ab(h, ab_w_in[0], ab_conv_w[0], ab_conv_b[0], ab_gate_b[0], ab_alpha_w2[0], ab_alpha_b[0],
                  ab_head_g[0], ab_w_out[0].astype(bf), m)
    xs, h = _res_mod(xs, y, params(0, 1, mc0[2], ml0[2]), m)
    y = _peer(h, peer_w_q[0].astype(bf), peer_sub_keys[0].astype(bf), ub, vb, 0)
    p10 = params(1, 0, mc0[5], ml0[5])
    _, hc = _res_mod(xs, y, p10, m, 0, m)
    xl, hl = _res_mod(xs, y, p10, m, m)
    y = _mixer_cd(hc, hl, cd_w_in[0].astype(bf), cd_qk_g[0], cd_rpb[0], cd_sink[0], cd_w_out[0].astype(bf))
    xl, h = _res_mod(xl, y, params(1, 1, ml1[2], ml1[2]), 0)
    y = _peer(h, peer_w_q[1].astype(bf), peer_sub_keys[1].astype(bf), ub, vb, 1)
    return (xl + ml1[5] * y)[None]
```

```python
import functools

import numpy as np
import jax
import jax.numpy as jnp
from jax import lax
from jax.experimental import pallas as pl
from jax.experimental.pallas import tpu as pltpu

D_MODEL = 2048
SEQ = 8192
DEPTH = 2
GRID_W = 64
CTX_LEN = 256
HEAD_DIM = 128
N_GROUP_HEADS = 8
EPS = 1e-6
A_HEADS = 8
A_WIDTH = 1024
CONV_W = 3
B_HEADS = 8
B_KEY_DIM = 64
B_KEY_WIDTH = 512
B_VAL_WIDTH = 1024
GLA_RANK = 16
GLA_TAU = 16.0
SCAN_CHUNK = 64
C_HEADS = 8
C_WIDTH = 1024
WIN_H = 8
WIN_W = 16
D_HEADS = 8
D_KV_HEADS = 2
D_GROUP = 4
D_WIDTH = 1024
D_KV_WIDTH = 256
WINDOW = 128
ROPE_BASE = 10000.0
PEER_HEADS = 8
PEER_NKEYS = 128
PEER_EXPERTS = PEER_NKEYS * PEER_NKEYS
PEER_QDIM = 256
PEER_TOPK = 16
PEER_SLOTS = PEER_HEADS * PEER_TOPK
AB_SPLITS = (A_WIDTH, A_WIDTH, A_WIDTH, A_WIDTH, 4 * A_HEADS,
             B_KEY_WIDTH, B_KEY_WIDTH, B_VAL_WIDTH, B_VAL_WIDTH, 2 * GLA_RANK)
CD_SPLITS = (C_WIDTH, C_WIDTH, D_KV_WIDTH, D_KV_WIDTH, C_WIDTH, D_WIDTH)
CD_KV_COLS = sum(CD_SPLITS[:4])

VMEM_LIMIT_BYTES = 56 * 1024 * 1024
LANES = 128


def _row_tile(m):
    for t in (1024, 768, 512, 256, 128):
        if m % t == 0:
            return t
    raise ValueError(f"unsupported row count {m}")


def _mm_kernel(x_ref, w_ref, o_ref):
    o_ref[...] = jnp.dot(x_ref[...], w_ref[...], preferred_element_type=jnp.float32).astype(o_ref.dtype)


def _matmul(x, w, out_dtype=jnp.float32):
    m, k = x.shape
    n = w.shape[1]
    bn = next((b for b in (1536, 1024, 512) if n % b == 0), n)
    n_pad = -(-n // bn) * bn
    if n_pad != n:
        w = jnp.pad(w, ((0, 0), (0, n_pad - n)))
    bm = _row_tile(m)
    out = pl.pallas_call(
        _mm_kernel,
        grid=(m // bm, n_pad // bn),
        in_specs=[pl.BlockSpec((bm, k), lambda i, j: (i, 0)),
                  pl.BlockSpec((k, bn), lambda i, j: (0, j))],
        out_specs=pl.BlockSpec((bm, bn), lambda i, j: (i, j)),
        out_shape=jax.ShapeDtypeStruct((m, n_pad), out_dtype),
        compiler_params=pltpu.CompilerParams(
            dimension_semantics=("parallel", "arbitrary"), vmem_limit_bytes=VMEM_LIMIT_BYTES),
        name="matmul",
    )(x.astype(jnp.bfloat16), w.astype(jnp.bfloat16))
    return out[:, :n] if n_pad != n else out


ROUTE_TOKENS = (1024, 768, 512, 256, 128)
GATE_TOKENS = 128
NEG_INF = float("-inf")


def _top16(s, key, big, vals_ref, keys_ref):
    for it in range(PEER_TOPK):
        m = jnp.max(s, axis=0, keepdims=True)
        k = jnp.min(jnp.where(s == m, key, big), axis=0, keepdims=True)
        vals_ref[it:it + 1, :] = m
        keys_ref[it:it + 1, :] = k
        s = jnp.where(key == k, NEG_INF, s)
    return vals_ref[...], keys_ref[...]


def _route_kernel(h_ref, wq_ref, k_ref, i1_ref, i2_ref, g_ref, *top_refs):
    bt = h_ref.shape[0]
    q = jnp.dot(h_ref[...], wq_ref[...], preferred_element_type=jnp.float32).astype(jnp.bfloat16)
    nt = (((1,), (1,)), ((), ()))
    s1 = lax.dot_general(k_ref[0, 0], q[:, :PEER_NKEYS], nt, preferred_element_type=jnp.float32)
    s2 = lax.dot_general(k_ref[0, 1], q[:, PEER_NKEYS:], nt, preferred_element_type=jnp.float32)
    row = lax.broadcasted_iota(jnp.int32, (PEER_NKEYS, bt), 0).astype(jnp.float32)
    v1, r1 = _top16(s1, row, float(PEER_NKEYS), *top_refs[0:2])
    v2, r2 = _top16(s2, row, float(PEER_NKEYS), *top_refs[2:4])
    i16 = lax.broadcasted_iota(jnp.int32, (16, bt), 0).astype(jnp.float32)
    i8 = lax.broadcasted_iota(jnp.int32, (8, bt), 0).astype(jnp.float32)
    cand = [v1[0:1] + v2]
    flat = [i16]
    for a in range(1, 8):
        cand.append(v1[a:a + 1] + v2[0:8])
        flat.append(i8 + float(a * PEER_TOPK))
    cand.append(v1[8:16] + v2[0:1])
    flat.append((i8 + 8.0) * float(PEER_TOPK))
    best, key = _top16(jnp.concatenate(cand, axis=0), jnp.concatenate(flat, axis=0), float(PEER_TOPK * PEER_TOPK),
                       *top_refs[4:6])
    key = key.astype(jnp.int32)
    a_sel = jnp.right_shift(key, 4)
    b_sel = jnp.bitwise_and(key, PEER_TOPK - 1)
    e1 = jnp.zeros_like(r1)
    e2 = jnp.zeros_like(r2)
    for r in range(PEER_TOPK):
        e1 = jnp.where(a_sel == r, r1[r:r + 1], e1)
        e2 = jnp.where(b_sel == r, r2[r:r + 1], e2)
    e1 = e1.astype(jnp.int32)
    e2 = e2.astype(jnp.int32)
    ex = jnp.exp(best - best[0:1])
    i1_ref[...] = e1
    i2_ref[...] = e2
    g_ref[...] = ex / jnp.sum(ex, axis=0, keepdims=True)


def _route(hb, wq, keys):
    t, d = hb.shape
    bt = next(b for b in ROUTE_TOKENS if t % b == 0)
    slot_spec = pl.BlockSpec((PEER_TOPK, bt), lambda i, h: (h, i))
    return pl.pallas_call(
        _route_kernel,
        grid=(t // bt, PEER_HEADS),
        in_specs=[pl.BlockSpec((bt, d), lambda i, h: (i, 0)),
                  pl.BlockSpec((d, PEER_QDIM), lambda i, h: (0, h)),
                  pl.BlockSpec((1, 2, PEER_NKEYS, PEER_QDIM // 2), lambda i, h: (h, 0, 0, 0))],
        out_specs=[slot_spec, slot_spec, slot_spec],
        out_shape=[jax.ShapeDtypeStruct((PEER_SLOTS, t), jnp.int32),
                   jax.ShapeDtypeStruct((PEER_SLOTS, t), jnp.int32),
                   jax.ShapeDtypeStruct((PEER_SLOTS, t), jnp.float32)],
        scratch_shapes=[pltpu.VMEM((PEER_TOPK, bt), jnp.float32)] * 6,
        compiler_params=pltpu.CompilerParams(
            dimension_semantics=("parallel", "arbitrary"), vmem_limit_bytes=VMEM_LIMIT_BYTES),
        name="peer_route",
    )(hb, wq, keys)


GATE_GROUP = 16


def _gate_matrix_kernel(i1_ref, i2_ref, g_ref, o_ref, i1_t, g_t, stage):
    tb = o_ref.shape[1]
    i1_t[...] = i1_ref[...].T
    g_t[...] = g_ref[...].T
    row_id = lax.broadcasted_iota(jnp.int32, (PEER_NKEYS, PEER_SLOTS), 0)
    lane_id = lax.broadcasted_iota(jnp.int32, (PEER_SLOTS, PEER_NKEYS), 1)
    zero = jnp.zeros((PEER_NKEYS, PEER_SLOTS), jnp.bfloat16)

    for tp in range(tb // 2):
        ta = 2 * tp
        tb_ = ta + 1
        ca = jnp.where(row_id == i1_t[ta:ta + 1, :], g_t[ta:ta + 1, :], 0.0).astype(jnp.bfloat16)
        cb = jnp.where(row_id == i1_t[tb_:tb_ + 1, :], g_t[tb_:tb_ + 1, :], 0.0).astype(jnp.bfloat16)
        oa = jnp.where(i2_ref[:, ta:ta + 1] == lane_id, 1.0, 0.0).astype(jnp.bfloat16)
        ob = jnp.where(i2_ref[:, tb_:tb_ + 1] == lane_id, 1.0, 0.0).astype(jnp.bfloat16)
        lhs = jnp.concatenate([ca, cb], axis=1)
        rhs = jnp.concatenate([jnp.concatenate([oa, zero], axis=1),
                               jnp.concatenate([zero, ob], axis=1)], axis=0)
        out = jnp.dot(lhs, rhs, preferred_element_type=jnp.float32)
        stage[ta % GATE_GROUP] = out[:, :PEER_NKEYS]
        stage[tb_ % GATE_GROUP] = out[:, PEER_NKEYS:]
        if tb_ % GATE_GROUP == GATE_GROUP - 1:
            t0 = tb_ + 1 - GATE_GROUP
            o_ref[:, t0:t0 + GATE_GROUP, :] = jnp.swapaxes(stage[...], 0, 1).astype(o_ref.dtype)


def _gate_matrix(i1, i2, g):
    t = i1.shape[1]
    tb = GATE_TOKENS
    return pl.pallas_call(
        _gate_matrix_kernel,
        grid=(t // tb,),
        in_specs=[pl.BlockSpec((PEER_SLOTS, tb), lambda i: (0, i))] * 3,
        out_specs=pl.BlockSpec((PEER_NKEYS, tb, PEER_NKEYS), lambda i: (0, i, 0)),
        out_shape=jax.ShapeDtypeStruct((PEER_NKEYS, t, PEER_NKEYS), jnp.bfloat16),
        scratch_shapes=[pltpu.VMEM((tb, PEER_SLOTS), jnp.int32),
                        pltpu.VMEM((tb, PEER_SLOTS), jnp.float32),
                        pltpu.VMEM((GATE_GROUP, PEER_NKEYS, PEER_NKEYS), jnp.float32)],
        compiler_params=pltpu.CompilerParams(
            dimension_semantics=("parallel",), vmem_limit_bytes=VMEM_LIMIT_BYTES),
        name="peer_gate_matrix",
    )(i1, i2, g)


def _gelu_tanh(x):
    return 0.5 * x * (1.0 + jnp.tanh(0.7978845608028654 * (x + 0.044715 * x * x * x)))


FP8 = jnp.float8_e4m3fn
PEER_ROW_GROUPS = 2
FP8_TARGET = 240.0


def _row_scale(x):
    amax = jnp.max(jnp.abs(x), axis=-1, keepdims=True)
    pos = amax > 0.0
    return jnp.where(pos, FP8_TARGET / amax, 1.0), jnp.where(pos, amax * (1.0 / FP8_TARGET), 1.0)


def _quant_kernel(x_ref, o_ref, s_ref):
    x = x_ref[0]
    scale, inv = _row_scale(x)
    o_ref[0] = (x * scale).astype(o_ref.dtype)
    s_ref[0] = jnp.broadcast_to(inv, s_ref.shape[1:])


def _quant_rows(x):
    nl, r, c = x.shape
    br = 1024
    q, s = pl.pallas_call(
        _quant_kernel,
        grid=(nl, r // br),
        in_specs=[pl.BlockSpec((1, br, c), lambda l, i: (l, i, 0))],
        out_specs=[pl.BlockSpec((1, br, c), lambda l, i: (l, i, 0)),
                   pl.BlockSpec((1, br, LANES), lambda l, i: (l, i, 0))],
        out_shape=[jax.ShapeDtypeStruct(x.shape, FP8), jax.ShapeDtypeStruct((nl, r, LANES), jnp.float32)],
        compiler_params=pltpu.CompilerParams(
            dimension_semantics=("parallel", "parallel"), vmem_limit_bytes=VMEM_LIMIT_BYTES),
        name="quant_rows",
    )(x)
    return q, s[:, :, 0][:, None, :]


def _peer_dense_kernel(h_ref, u_ref, su_ref, v_ref, sv_ref, g_ref, o_ref, h8_ref, sh_ref):
    j = pl.program_id(1)

    @pl.when(j == 0)
    def _():
        h = h_ref[...].astype(jnp.float32)
        scale, inv = _row_scale(h)
        h8_ref[...] = (h * scale).astype(h8_ref.dtype)
        sh_ref[...] = jnp.broadcast_to(inv, sh_ref.shape)

    nt = (((1,), (1,)), ((), ()))
    bm = h8_ref.shape[0]
    rows = [slice(r * bm // PEER_ROW_GROUPS, (r + 1) * bm // PEER_ROW_GROUPS) for r in range(PEER_ROW_GROUPS)]
    accs = [lax.dot_general(h8_ref[rs, :], u_ref[0], nt, preferred_element_type=jnp.float32) for rs in rows]
    ws = []
    for rs, acc in zip(rows, accs):
        act = _gelu_tanh(acc * sh_ref[rs, 0:1] * su_ref[0])
        gate = jnp.concatenate([g_ref[r, rs, :] for r in range(g_ref.shape[0])], axis=1)
        w = act * gate.astype(jnp.float32) * sv_ref[0]
        scale, inv = _row_scale(w)
        ws.append(((w * scale).astype(FP8), inv))
    contribs = [jnp.dot(w8, v_ref[0], preferred_element_type=jnp.float32) * inv for w8, inv in ws]

    @pl.when(j == 0)
    def _():
        for rs, contrib in zip(rows, contribs):
            o_ref[rs, :] = contrib

    @pl.when(j != 0)
    def _():
        for rs, contrib in zip(rows, contribs):
            o_ref[rs, :] += contrib


def _peer_dense(h, u, v, gmat, layer):
    t, d = h.shape
    (u8, su), (v8, sv) = u, v
    e = v8.shape[1]
    bm = _row_tile(t)
    be = 1024
    tab = pl.BlockSpec((1, be, d), lambda i, j: (layer, j, 0))
    sc = pl.BlockSpec((1, 1, be), lambda i, j: (layer, 0, j))
    return pl.pallas_call(
        _peer_dense_kernel,
        grid=(t // bm, e // be),
        in_specs=[pl.BlockSpec((bm, d), lambda i, j: (i, 0)), tab, sc, tab, sc,
                  pl.BlockSpec((be // PEER_NKEYS, bm, PEER_NKEYS), lambda i, j: (j, i, 0))],
        out_specs=pl.BlockSpec((bm, d), lambda i, j: (i, 0)),
        out_shape=jax.ShapeDtypeStruct((t, d), jnp.float32),
        scratch_shapes=[pltpu.VMEM((bm, d), FP8), pltpu.VMEM((bm, LANES), jnp.float32)],
        compiler_params=pltpu.CompilerParams(
            dimension_semantics=("parallel", "arbitrary"), vmem_limit_bytes=VMEM_LIMIT_BYTES),
        name="peer_dense",
    )(h, u8, su, v8, sv, gmat)


def _peer(h, wq, keys, u, v, layer):
    hb = h
    i1, i2, g = _route(hb, wq, keys)
    return _peer_dense(hb, u, v, _gate_matrix(i1, i2, g), layer)


ROPE_HALF = 32
NA_QROWS = 8
NA_KROWS = 16
SWA_QB = 256
SWA_KB = SWA_QB + 2 * WINDOW

P_CK, P_CV, P_DK, P_DV, P_CQ, P_DQ = 0, 8, 16, 18, 20, 28
QK_CK, QK_CQ, QK_DK, QK_DQ = 0, 8, 16, 20


def _rope_tables(n):
    t = np.arange(n)
    freqs = (np.float32(ROPE_BASE) ** (-np.arange(ROPE_HALF, dtype=np.float32) / np.float32(ROPE_HALF))).astype(np.float32)
    ang_r = (t // GRID_W).astype(np.float32)[:, None] * freqs
    ang_c = (t % GRID_W).astype(np.float32)[:, None] * freqs
    cos = np.concatenate([np.cos(ang_r), np.cos(ang_r), np.cos(ang_c), np.cos(ang_c)], axis=-1)
    sin = np.concatenate([-np.sin(ang_r), np.sin(ang_r), -np.sin(ang_c), np.sin(ang_c)], axis=-1)
    return cos.astype(np.float32), sin.astype(np.float32)


QK_GROUP = 4


def _qk_prep_kernel(tbl_ref, p_ref, g_ref, cos_ref, sin_ref, o_ref):
    del tbl_ref
    for hh in range(QK_GROUP):
        sl = slice(hh * HEAD_DIM, (hh + 1) * HEAD_DIM)
        x = p_ref[:, sl].astype(jnp.float32)
        y = x * lax.rsqrt(jnp.mean(x * x, axis=-1, keepdims=True) + EPS) * g_ref[0]
        lane = lax.broadcasted_iota(jnp.int32, y.shape, 1)
        first = jnp.bitwise_and(lane, 2 * ROPE_HALF - 1) < ROPE_HALF
        partner = jnp.where(first, pltpu.roll(y, LANES - ROPE_HALF, 1), pltpu.roll(y, ROPE_HALF, 1))
        o_ref[:, sl] = (y * cos_ref[0] + partner * sin_ref[0]).astype(o_ref.dtype)


def _qk_prep(p, table, gains, cos, sin):
    n = p.shape[0]
    ngrp = table.shape[1]
    w = QK_GROUP * HEAD_DIM
    bt = 2048 if n % 2048 == 0 else n
    grid_spec = pltpu.PrefetchScalarGridSpec(
        num_scalar_prefetch=1,
        grid=(n // bt, ngrp),
        in_specs=[pl.BlockSpec((bt, w), lambda i, j, tbl: (i, tbl[0, j])),
                  pl.BlockSpec((1, 1, HEAD_DIM), lambda i, j, tbl: (tbl[1, j], 0, 0)),
                  pl.BlockSpec((1, bt, HEAD_DIM), lambda i, j, tbl: (tbl[2, j], i, 0)),
                  pl.BlockSpec((1, bt, HEAD_DIM), lambda i, j, tbl: (tbl[2, j], i, 0))],
        out_specs=pl.BlockSpec((bt, w), lambda i, j, tbl: (i, j)),
    )
    return pl.pallas_call(
        _qk_prep_kernel,
        grid_spec=grid_spec,
        out_shape=jax.ShapeDtypeStruct((n, ngrp * w), jnp.bfloat16),
        compiler_params=pltpu.CompilerParams(
            dimension_semantics=("parallel", "arbitrary"), vmem_limit_bytes=VMEM_LIMIT_BYTES),
        name="qk_prep",
    )(table, p, gains, cos, sin)


def _na_block_types(n):
    rows = n // GRID_W
    return ((NA_QROWS, NA_QROWS - 4), (0, 0), (rows - NA_QROWS, rows - NA_KROWS))


def _na_row_ok(n, r0, rlo, qi, kj):
    rows = n // GRID_W
    rs = min(max(r0 + qi - WIN_H // 2, 0), rows - WIN_H)
    return rs <= rlo + kj < rs + WIN_H


def _na_pair_tables(rpb):
    nc = 2 * WIN_W - 1
    c = np.arange(GRID_W)[:, None]
    kc = np.arange(GRID_W)[None, :]
    cs = np.clip(c - WIN_W // 2, 0, GRID_W - WIN_W)
    col_ok = (kc >= cs) & (kc < cs + WIN_W)
    oh_col = (np.clip(kc - c + WIN_W - 1, 0, nc - 1)[None] == np.arange(nc)[:, None, None]).astype(np.float32)
    colb = jnp.einsum('hab,bcd->hacd', rpb.astype(jnp.float32), jnp.asarray(oh_col), precision=lax.Precision.HIGHEST)
    colb = jnp.where(jnp.asarray(col_ok)[None, None], colb, NEG_INF)
    none = jnp.full_like(colb[:, :1], NEG_INF)
    first = jnp.concatenate([none, colb], axis=1)
    second = jnp.concatenate([colb, none], axis=1)
    gone = jnp.full_like(first, NEG_INF)
    pair = lambda lo, hi: jnp.concatenate([lo, hi], axis=-1)
    return jnp.stack([pair(first, second), pair(first, gone), pair(gone, second)], axis=1)


def _softmax_pv(parts, extra_logit=None):
    m = parts[0][0].max(axis=-1, keepdims=True)
    for s, _ in parts[1:]:
        m = jnp.maximum(m, s.max(axis=-1, keepdims=True))
    if extra_logit is not None:
        m = jnp.maximum(m, extra_logit)
    den = jnp.exp(extra_logit - m) if extra_logit is not None else 0.0
    acc = None
    for s, v in parts:
        p = jnp.exp(s - m)
        den = den + p.sum(axis=-1, keepdims=True)
        pv = jnp.dot(p.astype(jnp.bfloat16), v, preferred_element_type=jnp.float32)
        acc = pv if acc is None else acc + pv
    return acc / den


def _na_kernel(q_ref, k_ref, v_ref, kc_ref, vc_ref, pt_ref, o_ref, bias_ref):
    n = q_ref.shape[0]
    qb = NA_QROWS * GRID_W
    kb = NA_KROWS * GRID_W
    nblk = n // qb
    nt = (((1,), (1,)), ((), ()))
    kc = kc_ref[...]
    vc = vc_ref[...]
    for kind, (r0, rlo) in enumerate(_na_block_types(n)):
        for qi in range(NA_QROWS):
            for kp in range(NA_KROWS // 2):
                ok0 = _na_row_ok(n, r0, rlo, qi, 2 * kp)
                ok1 = _na_row_ok(n, r0, rlo, qi, 2 * kp + 1)
                e = rlo + 2 * kp - (r0 + qi) + WIN_H - 1
                rows, cols = slice(qi * GRID_W, (qi + 1) * GRID_W), slice(kp * LANES, (kp + 1) * LANES)
                if ok0 or ok1:
                    bias_ref[kind, rows, cols] = pt_ref[0, 0 if (ok0 and ok1) else (1 if ok0 else 2), e + 1]
                else:
                    bias_ref[kind, rows, cols] = jnp.full((GRID_W, LANES), NEG_INF, jnp.float32)

    def body(b, carry):
        q0 = pl.multiple_of(b * qb, qb)
        ks = pl.multiple_of(jnp.clip(b * qb - (NA_KROWS - NA_QROWS) // 2 * GRID_W, 0, n - kb), 2 * LANES)
        kind = jnp.where(b == 0, 1, jnp.where(b == nblk - 1, 2, 0))
        q = q_ref[pl.ds(q0, qb), :]
        s_w = lax.dot_general(q, k_ref[pl.ds(ks, kb), :], nt, preferred_element_type=jnp.float32) + bias_ref[kind]
        s_c = lax.dot_general(q, kc, nt, preferred_element_type=jnp.float32)
        o = _softmax_pv([(s_w, v_ref[pl.ds(ks, kb), :]), (s_c, vc)])
        o_ref[pl.ds(q0, qb), :] = o.astype(o_ref.dtype)
        return carry

    lax.fori_loop(0, nblk, body, 0, unroll=4)


def _neighbourhood_attn(qk, p, qk_c, p_c, pair_tables):
    n = qk.shape[0]
    m = qk_c.shape[0]

    def full(col0):
        return pl.BlockSpec((n, HEAD_DIM), lambda h: (0, col0 + h))

    def ctx(col0):
        return pl.BlockSpec((m, HEAD_DIM), lambda h: (0, col0 + h))

    return pl.pallas_call(
        _na_kernel,
        grid=(C_HEADS,),
        in_specs=[full(QK_CQ), full(QK_CK), full(P_CV), ctx(QK_CK), ctx(P_CV),
                  pl.BlockSpec((1, 3, 2 * WIN_H, GRID_W, LANES), lambda h: (h, 0, 0, 0, 0))],
        out_specs=pl.BlockSpec((n, HEAD_DIM), lambda h: (0, h)),
        out_shape=jax.ShapeDtypeStruct((n, C_WIDTH), jnp.bfloat16),
        scratch_shapes=[pltpu.VMEM((3, NA_QROWS * GRID_W, NA_KROWS * GRID_W), jnp.float32)],
        compiler_params=pltpu.CompilerParams(
            dimension_semantics=("parallel",), vmem_limit_bytes=VMEM_LIMIT_BYTES),
        name="neighbourhood_attn",
    )(qk, qk, p, qk_c, p_c, pair_tables)


def _swa_kernel(q_ref, k_ref, v_ref, kc_ref, vc_ref, sink_ref, o_ref, band_ref):
    n = q_ref.shape[0]
    nblk = n // SWA_QB
    nt = (((1,), (1,)), ((), ()))
    kc = kc_ref[...]
    vc = vc_ref[...]
    kv = pl.program_id(0)
    sink = jnp.concatenate(
        [jnp.broadcast_to(sink_ref[pl.ds(kv * D_GROUP + g, 1), 0:1], (SWA_QB, 1)) for g in range(D_GROUP)], axis=0)
    q_off = jnp.bitwise_and(lax.broadcasted_iota(jnp.int32, (D_GROUP * SWA_QB, SWA_KB), 0), SWA_QB - 1)
    k_off = lax.broadcasted_iota(jnp.int32, (D_GROUP * SWA_QB, SWA_KB), 1)
    for kind, lead in enumerate((WINDOW, 0, SWA_KB - SWA_QB)):
        band_ref[kind] = jnp.where(jnp.abs(q_off - k_off + lead) <= WINDOW, 0.0, NEG_INF)

    def body(b, carry):
        q0 = pl.multiple_of(b * SWA_QB, SWA_QB)
        ks = pl.multiple_of(jnp.clip(q0 - WINDOW, 0, n - SWA_KB), LANES)
        kind = jnp.where(b == 0, 1, jnp.where(b == nblk - 1, 2, 0))
        q = jnp.concatenate([q_ref[pl.ds(q0, SWA_QB), g * HEAD_DIM:(g + 1) * HEAD_DIM] for g in range(D_GROUP)], axis=0)
        s_w = lax.dot_general(q, k_ref[pl.ds(ks, SWA_KB), :], nt, preferred_element_type=jnp.float32) + band_ref[kind]
        s_c = lax.dot_general(q, kc, nt, preferred_element_type=jnp.float32)
        o = _softmax_pv([(s_w, v_ref[pl.ds(ks, SWA_KB), :]), (s_c, vc)], extra_logit=sink)
        for g in range(D_GROUP):
            o_ref[pl.ds(q0, SWA_QB), g * HEAD_DIM:(g + 1) * HEAD_DIM] = o[g * SWA_QB:(g + 1) * SWA_QB].astype(o_ref.dtype)
        return carry

    lax.fori_loop(0, nblk, body, 0)


def _window_attn(qk, p, qk_c, p_c, sink_rows):
    n = qk.shape[0]
    m = qk_c.shape[0]
    gw = D_GROUP * HEAD_DIM
    return pl.pallas_call(
        _swa_kernel,
        grid=(D_KV_HEADS,),
        in_specs=[pl.BlockSpec((n, gw), lambda kv: (0, QK_DQ // D_GROUP + kv)),
                  pl.BlockSpec((n, HEAD_DIM), lambda kv: (0, QK_DK + kv)),
                  pl.BlockSpec((n, HEAD_DIM), lambda kv: (0, P_DV + kv)),
                  pl.BlockSpec((m, HEAD_DIM), lambda kv: (0, QK_DK + kv)),
                  pl.BlockSpec((m, HEAD_DIM), lambda kv: (0, P_DV + kv)),
                  pl.BlockSpec((D_HEADS, LANES), lambda kv: (0, 0))],
        out_specs=pl.BlockSpec((n, gw), lambda kv: (0, kv)),
        out_shape=jax.ShapeDtypeStruct((n, D_WIDTH), jnp.bfloat16),
        scratch_shapes=[pltpu.VMEM((3, D_GROUP * SWA_QB, SWA_KB), jnp.float32)],
        compiler_params=pltpu.CompilerParams(
            dimension_semantics=("parallel",), vmem_limit_bytes=VMEM_LIMIT_BYTES),
        name="window_attn",
    )(qk, qk, p, qk_c, p_c, sink_rows)


def _mixer_cd(h_c, h_l, w_in, qk_g, rpb, sink, w_out):
    n = h_l.shape[0]
    m = h_c.shape[0]
    scale = HEAD_DIM ** -0.5
    p_l = _matmul(h_l, w_in, jnp.bfloat16)
    p_c = _matmul(h_c, w_in[:, :CD_KV_COLS], jnp.bfloat16)
    gains = jnp.stack([qk_g[0] * scale, qk_g[1], qk_g[2] * scale, qk_g[3]]).astype(jnp.float32)[:, None, :]
    cos, sin = _rope_tables(n)
    cs_l = (np.stack([np.ones_like(cos), cos]), np.stack([np.zeros_like(sin), sin]))
    cs_c = (np.ones((1, m, HEAD_DIM), np.float32), np.zeros((1, m, HEAD_DIM), np.float32))
    grp = lambda blk: blk // QK_GROUP
    src = [grp(P_CK), grp(P_CK) + 1, grp(P_CQ), grp(P_CQ) + 1, grp(P_DK), grp(P_DQ), grp(P_DQ) + 1]
    gain = [1, 1, 0, 0, 3, 2, 2]
    rope = [0, 0, 0, 0, 1, 1, 1]
    tbl_l = jnp.asarray(np.array([src, gain, rope], np.int32))
    ctx_groups = [0, 1, 0, 1, 4]
    tbl_c = jnp.asarray(np.array([[src[g] for g in ctx_groups], [gain[g] for g in ctx_groups], [0] * 5], np.int32))
    qk_l = _qk_prep(p_l, tbl_l, gains, jnp.asarray(cs_l[0]), jnp.asarray(cs_l[1]))
    qk_c = _qk_prep(p_c, tbl_c, gains, jnp.asarray(cs_c[0]), jnp.asarray(cs_c[1]))
    c_out = _neighbourhood_attn(qk_l, p_l, qk_c, p_c, _na_pair_tables(rpb))
    sink_rows = jnp.broadcast_to(sink.astype(jnp.float32)[:, None], (D_HEADS, LANES))
    d_out = _window_attn(qk_l, p_l, qk_c, p_c, sink_rows)
    return _matmul(jnp.concatenate([c_out, d_out], axis=-1), w_out)


SCAN_L = 128
GLA_SUB = 4
N_SCAN_STATES = 2 * A_HEADS
GATE_LR_LANE = 4 * A_HEADS


def _log_sigmoid(x):
    return jnp.minimum(x, 0.0) - jnp.log(1.0 + jnp.exp(-jnp.abs(x)))


def _tri_mask(rev):
    r = lax.broadcasted_iota(jnp.int32, (SCAN_L, SCAN_L), 0)
    c = lax.broadcasted_iota(jnp.int32, (SCAN_L, SCAN_L), 1)
    return (c >= r) if rev else (c <= r)


def _cumsum_rows(tri, x):
    hi = x.astype(jnp.bfloat16)
    r1 = x - hi.astype(jnp.float32)
    mid = r1.astype(jnp.bfloat16)
    lo = (r1 - mid.astype(jnp.float32)).astype(jnp.bfloat16)
    dot = lambda p: jnp.dot(tri, p, preferred_element_type=jnp.float32)
    return dot(hi) + dot(mid) + dot(lo)


def _conv_kernel(x_ref, w_ref, b_ref, o_ref, *, seg, scale):
    x = x_ref[...].astype(jnp.float32)
    n = x.shape[0]
    t = lax.broadcasted_iota(jnp.int32, x.shape, 0)
    first = (t == 0) | (t == seg)
    last = (t == seg - 1) | (t == n - 1)
    prev = jnp.where(first, 0.0, pltpu.roll(x, 1, 0))
    nxt = jnp.where(last, 0.0, pltpu.roll(x, n - 1, 0))
    y = w_ref[0:1, :] * prev + w_ref[1:2, :] * x + w_ref[2:3, :] * nxt + b_ref[...]
    y = y * jax.nn.sigmoid(y)
    j = pl.program_id(0)
    o_ref[...] = (y * jnp.where(j >= A_HEADS, scale, 1.0)).astype(o_ref.dtype)


def _short_conv_silu(p, conv_w, conv_b, seg):
    t = p.shape[0]
    nblk = 2 * A_WIDTH // LANES
    return pl.pallas_call(
        functools.partial(_conv_kernel, seg=seg, scale=HEAD_DIM ** -0.5),
        grid=(nblk,),
        in_specs=[pl.BlockSpec((t, LANES), lambda j: (0, j)),
                  pl.BlockSpec((8, LANES), lambda j: (0, j)),
                  pl.BlockSpec((1, LANES), lambda j: (0, j))],
        out_specs=pl.BlockSpec((t, LANES), lambda j: (0, j)),
        out_shape=jax.ShapeDtypeStruct((t, 2 * A_WIDTH), jnp.bfloat16),
        compiler_params=pltpu.CompilerParams(
            dimension_semantics=("parallel",), vmem_limit_bytes=VMEM_LIMIT_BYTES),
        name="short_conv_silu",
    )(p, jnp.pad(conv_w, ((0, 8 - CONV_W), (0, 0))), conv_b[None, :])


def _scan_chunk_index(j, n_ctx_chunks, n_chunks, rev):
    if not rev:
        return j
    return jnp.where(j < n_ctx_chunks, n_ctx_chunks - 1 - j, n_chunks + n_ctx_chunks - 1 - j)


def _mlstm_kernel(qf_ref, kf_ref, vf_ref, gf_ref, qb_ref, kb_ref, vb_ref, gb_ref, gbias_ref,
                  yf_ref, yb_ref, c_ref, m_ref):
    @pl.when(pl.program_id(0) == 0)
    def _():
        c_ref[...] = jnp.zeros_like(c_ref)
        m_ref[...] = jnp.zeros_like(m_ref)

    lane = lax.broadcasted_iota(jnp.int32, (SCAN_L, LANES), 1)
    is_forget = jnp.bitwise_and(lane, 2 * A_HEADS - 1) >= A_HEADS
    ones = jnp.ones((SCAN_L, HEAD_DIM), jnp.bfloat16)
    nt = (((1,), (1,)), ((), ()))
    tn = (((0,), (0,)), ((), ()))
    for z, (q_ref, k_ref, v_ref, g_ref, y_ref) in enumerate(
            ((qf_ref, kf_ref, vf_ref, gf_ref, yf_ref), (qb_ref, kb_ref, vb_ref, gb_ref, yb_ref))):
        rev = z == 1
        mask = _tri_mask(rev)
        gates = g_ref[...] + gbias_ref[...]
        gates = jnp.where(is_forget, _log_sigmoid(gates), gates)
        csum = _cumsum_rows(mask.astype(jnp.bfloat16), jnp.where(is_forget, gates, 0.0))
        gates_t = gates.T
        csum_t = csum.T
        end = 0 if rev else SCAN_L - 1
        heads = []
        for h in range(A_HEADS):
            li, lf = z * 2 * A_HEADS + h, z * 2 * A_HEADS + A_HEADS + h
            sl = slice(h * HEAD_DIM, (h + 1) * HEAD_DIM)
            q, k, v = q_ref[:, sl], k_ref[:, sl], v_ref[:, sl]
            b_col, b_row = csum[:, lf:lf + 1], csum_t[lf:lf + 1, :]
            i_col, i_row = gates[:, li:li + 1], gates_t[li:li + 1, :]
            st = z * A_HEADS + h
            m_prev = m_ref[st][0:1, 0:1]
            dlog = jnp.where(mask, b_col - b_row + i_row, NEG_INF)
            m_inter = b_col + m_prev
            m_t = jnp.maximum(m_inter, jnp.max(dlog, axis=-1, keepdims=True))
            s = lax.dot_general(q, k, nt, preferred_element_type=jnp.float32)
            qc = jnp.dot(q, c_ref[st].astype(jnp.bfloat16), preferred_element_type=jnp.float32)
            heads.append(dict(sl=sl, st=st, k=k, v_ext=jnp.concatenate([v, ones], axis=1), b_col=b_col, i_col=i_col,
                              m_prev=m_prev, dlog=dlog, m_inter=m_inter, m_t=m_t, s=s, qc=qc))
        for hd in heads:
            sc = (hd["s"] * jnp.exp(hd["dlog"] - hd["m_t"])).astype(jnp.bfloat16)
            w_inter = jnp.exp(hd["m_inter"] - hd["m_t"])
            r = jnp.dot(sc, hd["v_ext"], preferred_element_type=jnp.float32) + w_inter * hd["qc"]
            num, den = r[:, :HEAD_DIM], r[:, HEAD_DIM:]
            y_ref[:, hd["sl"]] = num / jnp.maximum(jnp.abs(den), jnp.exp(-hd["m_t"]))
        for hd in heads:
            st = hd["st"]
            m_new = hd["m_t"][end:end + 1, :]
            b_end = hd["b_col"][end:end + 1, :]
            w_end = jnp.exp(b_end - hd["b_col"] + hd["i_col"] - m_new)
            decay = jnp.exp(b_end + hd["m_prev"] - m_new)
            kw = (hd["k"].astype(jnp.float32) * w_end).astype(jnp.bfloat16)
            c_ref[st] = decay * c_ref[st] + lax.dot_general(kw, hd["v_ext"], tn, preferred_element_type=jnp.float32)
            m_ref[st] = jnp.broadcast_to(m_new, (8, LANES))


def _scan_specs(n_ctx_chunks, n_chunks, width, col0, rev):
    return pl.BlockSpec((SCAN_L, width),
                        lambda j: (_scan_chunk_index(j, n_ctx_chunks, n_chunks, rev), col0))


def _mlstm(qk, p, gates, gate_bias, seg):
    t = qk.shape[0]
    nck = t // SCAN_L
    ncc = seg // SCAN_L
    ins, specs = [], []
    for rev in (False, True):
        ins += [qk, qk, p, gates]
        specs += [_scan_specs(ncc, nck, A_WIDTH, 0, rev), _scan_specs(ncc, nck, A_WIDTH, 1, rev),
                  _scan_specs(ncc, nck, A_WIDTH, 2, rev), _scan_specs(ncc, nck, LANES, 0, rev)]
    return pl.pallas_call(
        _mlstm_kernel,
        grid=(nck,),
        in_specs=specs + [pl.BlockSpec((1, LANES), lambda j: (0, 0))],
        out_specs=[_scan_specs(ncc, nck, A_WIDTH, 0, False), _scan_specs(ncc, nck, A_WIDTH, 0, True)],
        out_shape=[jax.ShapeDtypeStruct((t, A_WIDTH), jnp.float32)] * 2,
        scratch_shapes=[pltpu.VMEM((N_SCAN_STATES, HEAD_DIM, 2 * HEAD_DIM), jnp.float32),
                        pltpu.VMEM((N_SCAN_STATES, 8, LANES), jnp.float32)],
        compiler_params=pltpu.CompilerParams(
            dimension_semantics=("arbitrary",), vmem_limit_bytes=VMEM_LIMIT_BYTES),
        name="mlstm_scan",
    )(*ins, gate_bias)


def _gla_kernel(qf_ref, kf_ref, vf_ref, gf_ref, qb_ref, kb_ref, vb_ref, gb_ref, w2_ref, ab_ref,
                of_ref, ob_ref, s_ref):
    @pl.when(pl.program_id(0) == 0)
    def _():
        s_ref[...] = jnp.zeros_like(s_ref)

    bf = jnp.bfloat16
    row = lax.broadcasted_iota(jnp.int32, (SCAN_L, LANES), 0)
    lane = lax.broadcasted_iota(jnp.int32, (SCAN_L, LANES), 1)
    head_lane = (lane < B_KEY_DIM, lane >= B_KEY_DIM)
    r_sq = lax.broadcasted_iota(jnp.int32, (SCAN_L, SCAN_L), 0)
    c_sq = lax.broadcasted_iota(jnp.int32, (SCAN_L, SCAN_L), 1)
    in_sub = jnp.bitwise_and(row, GLA_SUB - 1)
    sel_r = lax.broadcasted_iota(jnp.int32, (LANES, 2 * LANES), 0)
    sel_c = lax.broadcasted_iota(jnp.int32, (LANES, 2 * LANES), 1)
    head_sum = ((sel_r < B_KEY_DIM) == (sel_c < LANES)).astype(bf)
    nt = (((1,), (1,)), ((), ()))
    tn = (((0,), (0,)), ((), ()))
    for z, (q_ref, k_ref, v_ref, g_ref, o_ref) in enumerate(
            ((qf_ref, kf_ref, vf_ref, gf_ref, of_ref), (qb_ref, kb_ref, vb_ref, gb_ref, ob_ref))):
        rev = z == 1
        tri = _tri_mask(rev).astype(bf)
        lr = g_ref[...].astype(bf)
        end = 0 if rev else SCAN_L - 1
        levels = []
        m = SCAN_L // 2
        while m >= GLA_SUB:
            shift = (2 * m).bit_length() - 1
            upper = jnp.bitwise_and(row, 2 * m - 1) >= m
            mid = jnp.left_shift(jnp.right_shift(r_sq, shift), shift) + (m if rev else m - 1)
            levels.append(dict(
                keys=upper if rev else ~upper,
                queries=~upper if rev else upper,
                pick_mid=(c_sq == mid).astype(bf),
                same=jnp.right_shift(r_sq, shift) == jnp.right_shift(c_sq, shift)))
            m //= 2
        pick_all = jnp.concatenate([lv["pick_mid"] for lv in levels], axis=0)
        for pr in range(B_HEADS // 2):
            ls = slice(pr * LANES, (pr + 1) * LANES)
            za = jnp.dot(lr, w2_ref[z, :, ls], preferred_element_type=jnp.float32) + ab_ref[z:z + 1, ls]
            b = _cumsum_rows(tri, _log_sigmoid(za) * (1.0 / GLA_TAU))
            q = q_ref[:, ls].astype(jnp.float32) * (B_KEY_DIM ** -0.5)
            k = k_ref[:, ls].astype(jnp.float32)
            v = v_ref[:, 2 * pr * HEAD_DIM:(2 * pr + 2) * HEAD_DIM]
            b_hi = b.astype(bf)
            b_lo = (b - b_hi.astype(jnp.float32)).astype(bf)
            picked = jnp.dot(pick_all, jnp.concatenate([b_hi, b_lo], axis=1), preferred_element_type=jnp.float32)
            q_ms, k_ms = [], []
            for li, lv in enumerate(levels):
                b_mid = picked[li * SCAN_L:(li + 1) * SCAN_L, :LANES] + picked[li * SCAN_L:(li + 1) * SCAN_L, LANES:]
                q_ms.append((q * jnp.exp(jnp.where(lv["queries"], b - b_mid, NEG_INF))).astype(bf))
                k_m = k * jnp.exp(jnp.where(lv["keys"], b_mid - b, NEG_INF))
                k_ms.append([jnp.where(head_lane[hh], k_m, 0.0).astype(bf) for hh in range(2)])
            parts = [[lax.dot_general(q_ms[li], k_ms[li][hh], nt, preferred_element_type=jnp.float32)
                      for hh in range(2)] for li in range(len(levels))]
            fs = []
            for d in range(GLA_SUB):
                shift = (SCAN_L - d) % SCAN_L if rev else d
                k_s = pltpu.roll(k, shift, 0) if d else k
                b_s = pltpu.roll(b, shift, 0) if d else b
                ok_row = (in_sub <= GLA_SUB - 1 - d) if rev else (in_sub >= d)
                fs.append((q * k_s * jnp.exp(jnp.where(ok_row, b - b_s, NEG_INF))).astype(bf))
            red = jnp.dot(jnp.concatenate(fs, axis=0), head_sum, preferred_element_type=jnp.float32)
            attn = []
            for hh in range(2):
                acc = jnp.zeros((SCAN_L, SCAN_L), jnp.float32)
                for li, lv in enumerate(levels):
                    acc = acc + jnp.where(lv["same"], parts[li][hh], 0.0)
                for d in range(GLA_SUB):
                    on_diag = (c_sq == r_sq + d) if rev else (c_sq == r_sq - d)
                    acc = acc + jnp.where(on_diag, red[d * SCAN_L:(d + 1) * SCAN_L, hh * LANES:(hh + 1) * LANES], 0.0)
                attn.append(acc)
            q_in = q * jnp.exp(b)
            b_end = b[end:end + 1, :]
            k_out = (k * jnp.exp(b_end - b)).astype(bf)
            st = z * (B_HEADS // 2) + pr
            s_t = s_ref[st]
            s_bf = s_t.astype(bf)
            for hh in range(2):
                vs = slice(hh * HEAD_DIM, (hh + 1) * HEAD_DIM)
                qh = jnp.where(head_lane[hh], q_in, 0.0).astype(bf)
                o = (jnp.dot(attn[hh].astype(bf), v[:, vs], preferred_element_type=jnp.float32)
                     + lax.dot_general(qh, s_bf[vs, :], nt, preferred_element_type=jnp.float32))
                o_ref[:, (2 * pr + hh) * HEAD_DIM:(2 * pr + hh + 1) * HEAD_DIM] = o
            s_ref[st] = s_t * jnp.exp(b_end) + lax.dot_general(v, k_out, tn, preferred_element_type=jnp.float32)


def _gla(p, gates, w2, alpha_b, seg):
    t = p.shape[0]
    nck = t // SCAN_L
    ncc = seg // SCAN_L
    ins, specs = [], []
    for rev in (False, True):
        ins += [p, p, p, gates]
        specs += [_scan_specs(ncc, nck, B_KEY_WIDTH, P_BQ // B_KEY_WIDTH, rev),
                  _scan_specs(ncc, nck, B_KEY_WIDTH, P_BK // B_KEY_WIDTH, rev),
                  _scan_specs(ncc, nck, B_VAL_WIDTH, P_BV // B_VAL_WIDTH, rev),
                  _scan_specs(ncc, nck, LANES, 0, rev)]
    return pl.pallas_call(
        _gla_kernel,
        grid=(nck,),
        in_specs=specs + [pl.BlockSpec((2, LANES, B_KEY_WIDTH), lambda j: (0, 0, 0)),
                          pl.BlockSpec((2, B_KEY_WIDTH), lambda j: (0, 0))],
        out_specs=[_scan_specs(ncc, nck, B_VAL_WIDTH, 0, False), _scan_specs(ncc, nck, B_VAL_WIDTH, 0, True)],
        out_shape=[jax.ShapeDtypeStruct((t, B_VAL_WIDTH), jnp.float32)] * 2,
        scratch_shapes=[pltpu.VMEM((B_HEADS, 2 * HEAD_DIM, LANES), jnp.float32)],
        compiler_params=pltpu.CompilerParams(
            dimension_semantics=("arbitrary",), vmem_limit_bytes=VMEM_LIMIT_BYTES),
        name="gla_scan",
    )(*ins, w2, alpha_b)


HEAD_OUT_GROUP = 4


def _head_out_kernel(af_ref, ab_ref, bf_ref, bb_ref, gate_ref, g_ref, o_ref):
    is_gla = pl.program_id(1) >= A_HEADS // HEAD_OUT_GROUP
    for hh in range(HEAD_OUT_GROUP):
        sl = slice(hh * HEAD_DIM, (hh + 1) * HEAD_DIM)
        y = jnp.where(is_gla, bf_ref[:, sl] + bb_ref[:, sl], af_ref[:, sl] + ab_ref[:, sl])
        yn = y * lax.rsqrt(jnp.mean(y * y, axis=-1, keepdims=True) + EPS) * g_ref[0]
        gate = gate_ref[:, sl].astype(jnp.float32)
        o_ref[:, sl] = (yn * jax.nn.sigmoid(gate) * jnp.where(is_gla, gate, 1.0)).astype(o_ref.dtype)


def _head_out(ya, yb, p, head_g):
    t = p.shape[0]
    bt = _row_tile(t)
    w = HEAD_OUT_GROUP * HEAD_DIM
    na, nb = A_HEADS // HEAD_OUT_GROUP, B_HEADS // HEAD_OUT_GROUP
    a_spec = pl.BlockSpec((bt, w), lambda i, j: (i, jnp.minimum(j, na - 1)))
    b_spec = pl.BlockSpec((bt, w), lambda i, j: (i, jnp.maximum(j - na, 0)))

    def gate_col(i, j):
        return (i, jnp.where(j < na, P_AO // HEAD_OUT_GROUP + j, P_BR // HEAD_OUT_GROUP + j - na))

    return pl.pallas_call(
        _head_out_kernel,
        grid=(t // bt, na + nb),
        in_specs=[a_spec, a_spec, b_spec, b_spec,
                  pl.BlockSpec((bt, w), gate_col),
                  pl.BlockSpec((1, 1, HEAD_DIM), lambda i, j: (j // na, 0, 0))],
        out_specs=pl.BlockSpec((bt, w), lambda i, j: (i, j)),
        out_shape=jax.ShapeDtypeStruct((t, (A_HEADS + B_HEADS) * HEAD_DIM), jnp.bfloat16),
        compiler_params=pltpu.CompilerParams(
            dimension_semantics=("parallel", "arbitrary"), vmem_limit_bytes=VMEM_LIMIT_BYTES),
        name="head_out",
    )(ya[0], ya[1], yb[0], yb[1], p, head_g.astype(jnp.float32)[:, None, :])


P_AO = 24
P_BQ, P_BK, P_BV = 4096, 4608, 5120
P_BR = 48


def _mixer_ab(h, w_in, conv_w, conv_b, gate_b, alpha_w2, alpha_b, head_g, w_out, seg):
    bf = jnp.bfloat16
    cols = np.cumsum((0,) + AB_SPLITS)
    pick = lambda *ids: jnp.concatenate([w_in[:, cols[i]:cols[i + 1]] for i in ids], axis=1)
    w_main = pick(0, 1, 2, 3, 5, 6, 7, 8).astype(bf)
    w_gate = jnp.pad(pick(4, 9), ((0, 0), (0, LANES - 4 * A_HEADS - 2 * GLA_RANK))).astype(bf)
    hb = h
    p = _matmul(hb, w_main, bf)
    gates = _matmul(hb, w_gate, jnp.float32)
    gate_bias = jnp.pad(gate_b.reshape(1, -1).astype(jnp.float32), ((0, 0), (0, LANES - 4 * A_HEADS)))
    w2 = jnp.zeros((2, LANES, B_KEY_WIDTH), jnp.float32)
    for z in range(2):
        w2 = w2.at[z, GATE_LR_LANE + z * GLA_RANK:GATE_LR_LANE + (z + 1) * GLA_RANK].set(alpha_w2[z])
    qk = _short_conv_silu(p, conv_w, conv_b, seg)
    ya = _mlstm(qk, p, gates, gate_bias, seg)
    yb = _gla(p, gates, w2.astype(bf), alpha_b.astype(jnp.float32), seg)
    return _matmul(_head_out(ya, yb, p, head_g), w_out)


MOD_ROWS = 256


def _res_mod_kernel(x_ref, y_ref, p_ref, xo_ref, h_ref):
    p = p_ref[0]
    x = x_ref[...] + p[0:1, :] * y_ref[...]
    xo_ref[...] = x
    r = x * lax.rsqrt(jnp.mean(x * x, axis=-1, keepdims=True) + EPS) * p[1:2, :]
    h_ref[...] = (r * (1.0 + p[3:4, :]) + p[2:3, :]).astype(h_ref.dtype)


def _mod_kernel(x_ref, p_ref, h_ref):
    p = p_ref[0]
    x = x_ref[...]
    r = x * lax.rsqrt(jnp.mean(x * x, axis=-1, keepdims=True) + EPS) * p[1:2, :]
    h_ref[...] = (r * (1.0 + p[3:4, :]) + p[2:3, :]).astype(h_ref.dtype)


def _mod_params(gate_c, gate_l, gain, shift_c, shift_l, scale_c, scale_l):
    rows = lambda g, sh, sc: jnp.concatenate(
        [g.reshape(1, -1), gain.reshape(1, -1), sh.reshape(1, -1), sc.reshape(1, -1),
         jnp.zeros((4, gain.shape[-1]), jnp.float32)], axis=0)
    return jnp.stack([rows(gate_c, shift_c, scale_c), rows(gate_l, shift_l, scale_l)]).astype(jnp.float32)


def _res_mod(x, y, params, seg, row0=0, n_rows=None):
    d = x.shape[1]
    n_rows = x.shape[0] - row0 if n_rows is None else n_rows
    bt = MOD_ROWS
    off, seg_blocks = row0 // bt, seg // bt
    row_spec = pl.BlockSpec((bt, d), lambda i: (i + off, 0))
    out_spec = pl.BlockSpec((bt, d), lambda i: (i, 0))
    p_spec = pl.BlockSpec((1, 8, d), lambda i: (jnp.where(i + off < seg_blocks, 0, 1), 0, 0))
    cp = pltpu.CompilerParams(dimension_semantics=("parallel",), vmem_limit_bytes=VMEM_LIMIT_BYTES)
    h_shape = jax.ShapeDtypeStruct((n_rows, d), jnp.bfloat16)
    if y is None:
        return pl.pallas_call(_mod_kernel, grid=(n_rows // bt,), in_specs=[row_spec, p_spec], out_specs=out_spec,
                              out_shape=h_shape, compiler_params=cp, name="modulate")(x, params)
    return pl.pallas_call(
        _res_mod_kernel, grid=(n_rows // bt,), in_specs=[row_spec, row_spec, p_spec],
        out_specs=[out_spec, out_spec],
        out_shape=[jax.ShapeDtypeStruct((n_rows, d), jnp.float32), h_shape],
        compiler_params=cp, name="residual_modulate")(x, y, params)


def _modulation_kernel(c_ref, w_ref, b_ref, o_ref):
    act = c_ref[...]
    act = (act * jax.nn.sigmoid(act)).astype(jnp.bfloat16)
    o_ref[0] = jnp.dot(act, w_ref[0].astype(jnp.bfloat16), preferred_element_type=jnp.float32) + b_ref[0]


def _modulation(c, c_ctx, mod_w, mod_b):
    nl, d, n = mod_w.shape
    cond = jnp.concatenate([c_ctx.reshape(1, d), c.reshape(1, d), jnp.zeros((6, d), jnp.float32)], axis=0)
    bn = n // 8
    return pl.pallas_call(
        _modulation_kernel,
        grid=(nl, n // bn),
        in_specs=[pl.BlockSpec((8, d), lambda l, j: (0, 0)),
                  pl.BlockSpec((1, d, bn), lambda l, j: (l, 0, j)),
                  pl.BlockSpec((1, 1, bn), lambda l, j: (l, 0, j))],
        out_specs=pl.BlockSpec((1, 8, bn), lambda l, j: (l, 0, j)),
        out_shape=jax.ShapeDtypeStruct((nl, 8, n), jnp.float32),
        compiler_params=pltpu.CompilerParams(
            dimension_semantics=("parallel", "parallel"), vmem_limit_bytes=VMEM_LIMIT_BYTES),
        name="modulation",
    )(cond, mod_w, mod_b[:, None, :])


def kernel(x, c, ctx, c_ctx, mod_w, mod_b, norm_g, ab_w_in, ab_conv_w, ab_conv_b, ab_gate_b, ab_alpha_w2,
           ab_alpha_b, ab_head_g, ab_w_out, cd_w_in, cd_qk_g, cd_rpb, cd_sink, cd_w_out, peer_w_q,
           peer_sub_keys, peer_u, peer_v):
    assert DEPTH == 2
    bf = jnp.bfloat16
    m = ctx.shape[1]
    xs = jnp.concatenate([ctx[0], x[0]], axis=0)
    ub = _quant_rows(peer_u)
    vb = _quant_rows(peer_v)
    mod_all = _modulation(c, c_ctx, mod_w, mod_b)
    mods = [(jnp.split(mod_all[l, 0], 6), jnp.split(mod_all[l, 1], 6)) for l in range(DEPTH)]
    one = jnp.ones_like(mods[0][0][0])

    def params(l, sub, gate_c, gate_l):
        mc, ml = mods[l]
        return _mod_params(gate_c, gate_l, norm_g[l, sub], mc[3 * sub], ml[3 * sub], mc[3 * sub + 1], ml[3 * sub + 1])

    (mc0, ml0), (mc1, ml1) = mods
    h = _res_mod(xs, None, params(0, 0, one, one), m)
    y = _mixer_ab(h, ab_w_in[0], ab_conv_w[0], ab_conv_b[0], ab_gate_b[0], ab_alpha_w2[0], ab_alpha_b[0],
                  ab_head_g[0], ab_w_out[0].astype(bf), m)
    xs, h = _res_mod(xs, y, params(0, 1, mc0[2], ml0[2]), m)
    y = _peer(h, peer_w_q[0].astype(bf), peer_sub_keys[0].astype(bf), ub, vb, 0)
    p10 = params(1, 0, mc0[5], ml0[5])
    _, hc = _res_mod(xs, y, p10, m, 0, m)
    xl, hl = _res_mod(xs, y, p10, m, m)
    y = _mixer_cd(hc, hl, cd_w_in[0].astype(bf), cd_qk_g[0], cd_rpb[0], cd_sink[0], cd_w_out[0].astype(bf))
    xl, h = _res_mod(xl, y, params(1, 1, ml1[2], ml1[2]), 0)
    y = _peer(h, peer_w_q[1].astype(bf), peer_sub_keys[1].astype(bf), ub, vb, 1)
    return (xl + ml1[5] * y)[None]
```
